```python
import jax, jax.numpy as jnp
from jax import lax
import numpy as np

D_MODEL = 2048
BATCH = 4
SEQ = 2048
DEPTH = 2
DEC_BATCH = 128
DEC_SEQ = 8
PAST_LEN = 16384
PAGE_SIZE = 128

N_META = 16
CONV_WIDTH = 3
CONV_HIST = CONV_WIDTH - 1
POOL_WINDOWS = (2, 4, 8, 16)
N_POOL_GROUPS = len(POOL_WINDOWS)
POOL_GROUP_DIM = D_MODEL // N_POOL_GROUPS
POOL_HIST = max(POOL_WINDOWS) - 1
D_FF = ((8 * D_MODEL + 3 * 256 - 1) // (3 * 256)) * 256
N_CONV_LAYERS = (DEPTH + 1) // 2
N_POOL_LAYERS = DEPTH // 2
EPS = 1e-6

kernel_name = "hybrid_shortconv_pool_decoder_step"


def rmsnorm(x, g):
    xf = x.astype(jnp.float32)
    y = xf * lax.rsqrt(jnp.mean(xf * xf, axis=-1, keepdims=True) + EPS)
    return (y * g.astype(jnp.float32)).astype(x.dtype)


def conv_mixer(h, hist, w_in, w_dw, w_out):
    T = h.shape[1]
    bcv = h @ w_in
    b, c, v = jnp.split(bcv, 3, axis=-1)
    u = c * v
    u_full = jnp.concatenate([hist.astype(u.dtype), u], axis=1)
    conv = sum(w_dw[k] * u_full[:, k:k + T] for k in range(CONV_WIDTH))
    y = (b * conv) @ w_out
    return y, u_full[:, -CONV_HIST:]


def pool_mixer(h, hist, t0, w_pool, scale):
    B, T, D = h.shape
    u = jnp.concatenate([hist.astype(h.dtype), h], axis=1)
    cs = jnp.cumsum(u.astype(jnp.float32), axis=1)
    cs = jnp.concatenate([jnp.zeros((B, 1, D), jnp.float32), cs], axis=1)
    end = cs[:, POOL_HIST + 1:POOL_HIST + 1 + T]
    start = jnp.concatenate(
        [cs[:, POOL_HIST + 1 - w:POOL_HIST + 1 - w + T, g * POOL_GROUP_DIM:(g + 1) * POOL_GROUP_DIM]
         for g, w in enumerate(POOL_WINDOWS)], axis=-1)
    win = jnp.repeat(jnp.asarray(POOL_WINDOWS, jnp.float32), POOL_GROUP_DIM)
    pos = (t0 + jnp.arange(T, dtype=jnp.int32)).astype(jnp.float32)
    count = jnp.minimum(win[None, :], pos[:, None] + 1.0)
    p = ((end - start) / count - h.astype(jnp.float32)).astype(h.dtype)
    pg = p.reshape(B, T, N_POOL_GROUPS, POOL_GROUP_DIM)
    out = jnp.einsum('btgc,gcd->btgd', pg, w_pool).reshape(B, T, D)
    return out * scale, u[:, -POOL_HIST:]


def swiglu(h, w_gu, w_down):
    g, u = jnp.split(h @ w_gu, 2, axis=-1)
    return (jax.nn.silu(g) * u) @ w_down


def trunk(x, conv_states, pool_states, t0, norm_mix, norm_ffn, norm_final,
          conv_w_in, conv_w_dw, conv_w_out, pool_w, pool_scale, ffn_w_gate_up, ffn_w_down):
    h = x
    new_conv, new_pool = [], []
    for i in range(DEPTH):
        n = rmsnorm(h, norm_mix[i])
        j = i // 2
        if i % 2 == 0:
            y, s = conv_mixer(n, conv_states[j], conv_w_in[j], conv_w_dw[j], conv_w_out[j])
            new_conv.append(s)
        else:
            y, s = pool_mixer(n, pool_states[j], t0, pool_w[j], pool_scale[j])
            new_pool.append(s)
        h = h + y
        h = h + swiglu(rmsnorm(h, norm_ffn[i]), ffn_w_gate_up[i], ffn_w_down[i])
    return rmsnorm(h, norm_final), jnp.stack(new_conv), jnp.stack(new_pool)


def setup_inputs(seed: int = 0) -> dict:
    key = jax.random.key(seed)
    ks = jax.random.split(key, 16)
    f32 = jnp.float32
    D, F = D_MODEL, D_FF
    nrm = lambda k, s, sc: jax.random.normal(k, s, f32) * sc
    return {
        "x_prompt": nrm(ks[0], (BATCH, SEQ, D), 1.0),
        "x_sample": nrm(ks[1], (DEC_BATCH, DEC_SEQ, D), 1.0),
        "state_conv": nrm(ks[2], (N_CONV_LAYERS, DEC_BATCH, CONV_HIST, D), 1.0),
        "state_pool": nrm(ks[3], (N_POOL_LAYERS, DEC_BATCH, POOL_HIST, D), 1.0),
        "meta_tokens": nrm(ks[4], (N_META, D), 1.0),
        "norm_mix": 1.0 + nrm(ks[5], (DEPTH, D), 0.05),
        "norm_ffn": 1.0 + nrm(ks[6], (DEPTH, D), 0.05),
        "norm_final": 1.0 + nrm(ks[7], (D,), 0.05),
        "conv_w_in": nrm(ks[8], (N_CONV_LAYERS, D, 3 * D), D ** -0.5),
        "conv_w_dw": nrm(ks[9], (N_CONV_LAYERS, CONV_WIDTH, D), CONV_WIDTH ** -0.5),
        "conv_w_out": nrm(ks[10], (N_CONV_LAYERS, D, D), D ** -0.5),
        "pool_w": nrm(ks[11], (N_POOL_LAYERS, N_POOL_GROUPS, POOL_GROUP_DIM, POOL_GROUP_DIM), POOL_GROUP_DIM ** -0.5),
        "pool_scale": 1.0 + nrm(ks[12], (N_POOL_LAYERS, D), 0.1),
        "ffn_w_gate_up": nrm(ks[13], (DEPTH, D, 2 * F), D ** -0.5),
        "ffn_w_down": nrm(ks[14], (DEPTH, F, D), F ** -0.5),
    }


def reference(x_prompt, x_sample, state_conv, state_pool, meta_tokens, norm_mix, norm_ffn, norm_final,
              conv_w_in, conv_w_dw, conv_w_out, pool_w, pool_scale, ffn_w_gate_up, ffn_w_down):
    weights = (norm_mix, norm_ffn, norm_final, conv_w_in, conv_w_dw, conv_w_out,
               pool_w, pool_scale, ffn_w_gate_up, ffn_w_down)
    B = x_prompt.shape[0]
    meta = jnp.broadcast_to(meta_tokens.astype(x_prompt.dtype)[None], (B, N_META, D_MODEL))
    xp = jnp.concatenate([meta, x_prompt], axis=1)
    zc = jnp.zeros((N_CONV_LAYERS, B, CONV_HIST, D_MODEL), x_prompt.dtype)
    zp = jnp.zeros((N_POOL_LAYERS, B, POOL_HIST, D_MODEL), x_prompt.dtype)
    yp, new_conv_prompt, new_pool_prompt = trunk(xp, zc, zp, 0, *weights)
    y_prompt = yp[:, N_META:]
    y_sample, new_conv_sample, new_pool_sample = trunk(x_sample, state_conv, state_pool, PAST_LEN, *weights)
    return (y_prompt, y_sample, new_conv_prompt, new_pool_prompt, new_conv_sample, new_pool_sample)
```

```python
import functools

import jax
import jax.numpy as jnp
from jax import lax
from jax.experimental import pallas as pl
from jax.experimental.pallas import tpu as pltpu

D_MODEL = 2048
D_FF = 5632
N_META = 16
SEQ = 2048
N_SAMPLE_SEQ = 128
SAMPLE_T = 8
CONV_HIST = 2
POOL_WINDOWS = (2, 4, 8, 16)
POOL_GROUP = D_MODEL // len(POOL_WINDOWS)
POOL_HIST = 15
EPS = 1e-6

N_SAMPLE_ROWS = N_SAMPLE_SEQ * SAMPLE_T
S_ROWS = N_SAMPLE_ROWS + 2 * N_META
META_ROW0 = N_SAMPLE_ROWS
TAIL = 8
HALO = 16

V7X_SCOPED_VMEM_BYTES = 60000 * 1024

BF16 = jnp.bfloat16
F32 = jnp.float32


def _rms(x, g):
    ms = jnp.mean(x * x, axis=-1, keepdims=True)
    return (x * lax.rsqrt(ms + EPS)) * g


def _dot(a, b):
    return jnp.dot(a, b, preferred_element_type=F32)


def _params(n_axes):
    return pltpu.CompilerParams(
        dimension_semantics=("arbitrary",) * n_axes,
        vmem_limit_bytes=V7X_SCOPED_VMEM_BYTES,
    )


def _conv3(w, u, ubuf, off):
    t = u.shape[0]
    return (w[2:3] * u + w[1:2] * ubuf[off - 1:off - 1 + t, :] + w[0:1] * ubuf[off - 2:off - 2 + t, :])


def _mix_p_kernel(tiles_per_seq, h_ref, g_ref, wb_ref, wc_ref, wv_ref, wdw_ref, umeta_ref,
                  z_ref, utail_ref, n_sc, ubuf, carry):
    i = pl.program_id(0)
    j = pl.program_id(1)
    bm = h_ref.shape[0]

    @pl.when(j == 0)
    def _():
        n_sc[...] = _rms(h_ref[...], g_ref[...]).astype(BF16)

    n = n_sc[...]
    b = _dot(n, wb_ref[...])
    u = _dot(n, wc_ref[...]) * _dot(n, wv_ref[...])

    @pl.when(i % tiles_per_seq == 0)
    def _():
        ubuf[0:TAIL, :] = umeta_ref[...]

    @pl.when(i % tiles_per_seq != 0)
    def _():
        ubuf[0:TAIL, :] = carry[j]

    ubuf[TAIL:TAIL + bm, :] = u
    tail = u[bm - TAIL:, :]
    carry[j] = tail
    utail_ref[...] = tail
    z_ref[...] = (b * _conv3(wdw_ref[...], u, ubuf, TAIL)).astype(BF16)


def _mix_s_kernel(h_ref, g_ref, wb_ref, wc_ref, wv_ref, wdw_ref, hc_ref,
                  z_ref, uts_ref, umeta_ref, n_sc, ubuf, mbuf):
    j = pl.program_id(0)
    ns = N_SAMPLE_ROWS
    nb = N_SAMPLE_SEQ
    nm = S_ROWS - ns

    @pl.when(j == 0)
    def _():
        n_sc[...] = _rms(h_ref[...], g_ref[...]).astype(BF16)

    n = n_sc[...]
    b = _dot(n, wb_ref[...])
    u = _dot(n, wc_ref[...]) * _dot(n, wv_ref[...])
    w = wdw_ref[...]

    us = u[0:ns, :]
    ubuf[0:CONV_HIST * nb, :] = hc_ref[...]
    ubuf[CONV_HIST * nb:CONV_HIST * nb + ns, :] = us
    conv_s = (w[2:3] * us + w[1:2] * ubuf[nb:nb + ns, :] + w[0:1] * ubuf[0:ns, :])
    z_ref[0:ns, :] = (b[0:ns, :] * conv_s).astype(BF16)
    uts_ref[...] = us[ns - CONV_HIST * nb:, :]

    um = u[ns:, :]
    mbuf[0:TAIL, :] = jnp.zeros((TAIL, um.shape[1]), F32)
    mbuf[TAIL:TAIL + nm, :] = um
    z_ref[ns:, :] = (b[ns:, :] * _conv3(w, um, mbuf, TAIL)).astype(BF16)
    umeta_ref[...] = um[N_META - TAIL:N_META, :]


def _mix_conv_p(h, g, w_in, w_dw, umeta, bm, bn):
    rows = h.shape[0]
    n_i, n_j = rows // bm, D_MODEL // bn
    wspec = lambda part: pl.BlockSpec((D_MODEL, bn), lambda i, j: (0, part * n_j + j))
    return pl.pallas_call(
        functools.partial(_mix_p_kernel, SEQ // bm),
        grid=(n_i, n_j),
        in_specs=[
            pl.BlockSpec((bm, D_MODEL), lambda i, j: (i, 0)),
            pl.BlockSpec((1, D_MODEL), lambda i, j: (0, 0)),
            wspec(0), wspec(1), wspec(2),
            pl.BlockSpec((3, bn), lambda i, j: (0, j)),
            pl.BlockSpec((TAIL, bn), lambda i, j: (0, j)),
        ],
        out_specs=[
            pl.BlockSpec((bm, bn), lambda i, j: (i, j)),
            pl.BlockSpec((TAIL, bn), lambda i, j: (i, j)),
        ],
        out_shape=[
            jax.ShapeDtypeStruct((rows, D_MODEL), BF16),
            jax.ShapeDtypeStruct((n_i * TAIL, D_MODEL), F32),
        ],
        scratch_shapes=[
            pltpu.VMEM((bm, D_MODEL), BF16),
            pltpu.VMEM((TAIL + bm, bn), F32),
            pltpu.VMEM((n_j, TAIL, bn), F32),
        ],
        compiler_params=_params(2),
        name="mix_conv_p",
    )(h, g, w_in, w_in, w_in, w_dw, umeta)


def _mix_conv_s(h, g, w_in, w_dw, hc, bn):
    n_j = D_MODEL // bn
    nm = S_ROWS - N_SAMPLE_ROWS
    wspec = lambda part: pl.BlockSpec((D_MODEL, bn), lambda j: (0, part * n_j + j))
    return pl.pallas_call(
        _mix_s_kernel,
        grid=(n_j,),
        in_specs=[
            pl.BlockSpec((S_ROWS, D_MODEL), lambda j: (0, 0)),
            pl.BlockSpec((1, D_MODEL), lambda j: (0, 0)),
            wspec(0), wspec(1), wspec(2),
            pl.BlockSpec((3, bn), lambda j: (0, j)),
            pl.BlockSpec((CONV_HIST * N_SAMPLE_SEQ, bn), lambda j: (0, j)),
        ],
        out_specs=[
            pl.BlockSpec((S_ROWS, bn), lambda j: (0, j)),
            pl.BlockSpec((CONV_HIST * N_SAMPLE_SEQ, bn), lambda j: (0, j)),
            pl.BlockSpec((TAIL, bn), lambda j: (0, j)),
        ],
        out_shape=[
            jax.ShapeDtypeStruct((S_ROWS, D_MODEL), BF16),
            jax.ShapeDtypeStruct((CONV_HIST * N_SAMPLE_SEQ, D_MODEL), F32),
            jax.ShapeDtypeStruct((TAIL, D_MODEL), F32),
        ],
        scratch_shapes=[
            pltpu.VMEM((S_ROWS, D_MODEL), BF16),
            pltpu.VMEM((CONV_HIST * N_SAMPLE_SEQ + N_SAMPLE_ROWS, bn), F32),
            pltpu.VMEM((TAIL + nm, bn), F32),
        ],
        compiler_params=_params(1),
        name="mix_conv_s",
    )(h, g, w_in, w_in, w_in, w_dw, hc)


def _proj_res_kernel(final_norm, x_ref, w_ref, h_ref, g_ref, o_ref):
    k = pl.program_id(1)
    acc = _dot(x_ref[...], w_ref[...])

    @pl.when(k == 0)
    def _():
        o_ref[...] = h_ref[...] + acc

    @pl.when(k != 0)
    def _():
        o_ref[...] += acc

    if final_norm:
        @pl.when(k == pl.num_programs(1) - 1)
        def _():
            o_ref[...] = _rms(o_ref[...], g_ref[...])


def _proj_res(x, w, h, g, bm, bk, final_norm, name):
    rows, kdim = x.shape
    n_i, n_k = rows // bm, kdim // bk
    return pl.pallas_call(
        functools.partial(_proj_res_kernel, final_norm),
        grid=(n_i, n_k),
        in_specs=[
            pl.BlockSpec((bm, bk), lambda i, k: (i, k)),
            pl.BlockSpec((bk, D_MODEL), lambda i, k: (k, 0)),
            pl.BlockSpec((bm, D_MODEL), lambda i, k: (i, 0)),
            pl.BlockSpec((1, D_MODEL), lambda i, k: (0, 0)),
        ],
        out_specs=pl.BlockSpec((bm, D_MODEL), lambda i, k: (i, 0)),
        out_shape=jax.ShapeDtypeStruct((rows, D_MODEL), F32),
        compiler_params=_params(2),
        name=name,
    )(x, w, h, g)


def _gate_up_kernel(h_ref, g_ref, wg_ref, wu_ref, a_ref, m_sc):
    @pl.when(pl.program_id(1) == 0)
    def _():
        m_sc[...] = _rms(h_ref[...], g_ref[...]).astype(BF16)

    m = m_sc[...]
    gate = _dot(m, wg_ref[...])
    up = _dot(m, wu_ref[...])
    a_ref[...] = (gate * (1.0 / (1.0 + jnp.exp(-gate))) * up).astype(BF16)


def _gate_up(h, g, w_gu, bm, bn, name):
    rows = h.shape[0]
    n_i, n_j = rows // bm, D_FF // bn
    return pl.pallas_call(
        _gate_up_kernel,
        grid=(n_i, n_j),
        in_specs=[
            pl.BlockSpec((bm, D_MODEL), lambda i, j: (i, 0)),
            pl.BlockSpec((1, D_MODEL), lambda i, j: (0, 0)),
            pl.BlockSpec((D_MODEL, bn), lambda i, j: (0, j)),
            pl.BlockSpec((D_MODEL, bn), lambda i, j: (0, n_j + j)),
        ],
        out_specs=pl.BlockSpec((bm, bn), lambda i, j: (i, j)),
        out_shape=jax.ShapeDtypeStruct((rows, D_FF), BF16),
        scratch_shapes=[pltpu.VMEM((bm, D_MODEL), BF16)],
        compiler_params=_params(2),
        name=name,
    )(h, g, w_gu, w_gu)


def _pool_p_kernel(tiles_per_seq, h_ref, halo_p_ref, halo_s_ref, g_ref, wp_ref, sc_ref,
                   o_ref, ntail_ref, nbuf):
    i = pl.program_id(0)
    bm = h_ref.shape[0]
    g = g_ref[...]
    x = h_ref[...]
    n = _rms(x, g)

    @pl.when(i % tiles_per_seq == 0)
    def _():
        nbuf[0:HALO, :] = _rms(halo_s_ref[...], g)

    @pl.when(i % tiles_per_seq != 0)
    def _():
        nbuf[0:HALO, :] = _rms(halo_p_ref[...], g)

    nbuf[HALO:HALO + bm, :] = n
    ntail_ref[...] = n[bm - HALO:, :]
    for gi, win in enumerate(POOL_WINDOWS):
        c0 = gi * POOL_GROUP
        ng = n[:, c0:c0 + POOL_GROUP]
        acc = ng
        for k in range(1, win):
            acc = acc + nbuf[HALO - k:HALO - k + bm, c0:c0 + POOL_GROUP]
        p = acc * (1.0 / win) - ng
        y = _dot(p.astype(BF16), wp_ref[gi]) * sc_ref[:, c0:c0 + POOL_GROUP]
        o_ref[:, c0:c0 + POOL_GROUP] = x[:, c0:c0 + POOL_GROUP] + y


def _pool_p(h_p, h_s, g, wp, scale, bm):
    rows = h_p.shape[0]
    n_i = rows // bm
    return pl.pallas_call(
        functools.partial(_pool_p_kernel, SEQ // bm),
        grid=(n_i,),
        in_specs=[
            pl.BlockSpec((bm, D_MODEL), lambda i: (i, 0)),
            pl.BlockSpec((HALO, D_MODEL), lambda i: (jnp.maximum(i * (bm // HALO) - 1, 0), 0)),
            pl.BlockSpec((HALO, D_MODEL), lambda i: (META_ROW0 // HALO, 0)),
            pl.BlockSpec((1, D_MODEL), lambda i: (0, 0)),
            pl.BlockSpec((len(POOL_WINDOWS), POOL_GROUP, POOL_GROUP), lambda i: (0, 0, 0)),
            pl.BlockSpec((1, D_MODEL), lambda i: (0, 0)),
        ],
        out_specs=[
            pl.BlockSpec((bm, D_MODEL), lambda i: (i, 0)),
            pl.BlockSpec((HALO, D_MODEL), lambda i: (i, 0)),
        ],
        out_shape=[
            jax.ShapeDtypeStruct((rows, D_MODEL), F32),
            jax.ShapeDtypeStruct((n_i * HALO, D_MODEL), F32),
        ],
        scratch_shapes=[pltpu.VMEM((HALO + bm, D_MODEL), F32)],
        compiler_params=_params(1),
        name="pool_p",
    )(h_p, h_p, h_s, g, wp, scale)


def _pool_s_kernel(hfull_ref, hcol_ref, hp_ref, g_ref, wp_ref, sc_ref,
                   o_ref, n_ref, inv_sc, nbuf, sum_sc):
    j = pl.program_id(0)
    ns = N_SAMPLE_ROWS
    nb = N_SAMPLE_SEQ
    nh = (POOL_HIST + 1) * nb

    @pl.when(j == 0)
    def _():
        xf = hfull_ref[...]
        inv_sc[...] = lax.rsqrt(jnp.mean(xf * xf, axis=-1, keepdims=True) + EPS)

    x = hcol_ref[...]
    n = ((x * inv_sc[...]) * g_ref[...])[0:ns, :]
    n_ref[...] = n
    nbuf[0:nh, :] = hp_ref[...]
    nbuf[nh:nh + ns, :] = n

    for gi, win in enumerate(POOL_WINDOWS):
        @pl.when(j == gi)
        def _(win=win):
            acc = n
            for k in range(1, win):
                acc = acc + nbuf[nh - k * nb:nh - k * nb + ns, :]
            sum_sc[...] = acc * (1.0 / win)

    p = sum_sc[...] - n
    y = _dot(p.astype(BF16), wp_ref[...]) * sc_ref[...]
    o_ref[0:ns, :] = x[0:ns, :] + y
    o_ref[ns:, :] = x[ns:, :]


def _pool_s(h_s, hp, g, wp, scale):
    ns = N_SAMPLE_ROWS
    nh = (POOL_HIST + 1) * N_SAMPLE_SEQ
    pg = POOL_GROUP
    return pl.pallas_call(
        _pool_s_kernel,
        grid=(len(POOL_WINDOWS),),
        in_specs=[
            pl.BlockSpec((S_ROWS, D_MODEL), lambda j: (0, 0)),
            pl.BlockSpec((S_ROWS, pg), lambda j: (0, j)),
            pl.BlockSpec((nh, pg), lambda j: (0, j)),
            pl.BlockSpec((1, pg), lambda j: (0, j)),
            pl.BlockSpec((None, pg, pg), lambda j: (j, 0, 0)),
            pl.BlockSpec((1, pg), lambda j: (0, j)),
        ],
        out_specs=[
            pl.BlockSpec((S_ROWS, pg), lambda j: (0, j)),
            pl.BlockSpec((ns, pg), lambda j: (0, j)),
        ],
        out_shape=[
            jax.ShapeDtypeStruct((S_ROWS, D_MODEL), F32),
            jax.ShapeDtypeStruct((ns, D_MODEL), F32),
        ],
        scratch_shapes=[
            pltpu.VMEM((S_ROWS, 1), F32),
            pltpu.VMEM((nh + ns, pg), F32),
            pltpu.VMEM((ns, pg), F32),
        ],
        compiler_params=_params(1),
        name="pool_s",
    )(h_s, h_s, hp, g, wp, scale)


BM_P = 1024
BM_S = S_ROWS // 2
BM_POOL = 512
BN_MIX = 512
BN_FF = 512
BK = 512


def kernel(x_prompt, x_sample, state_conv, state_pool, meta_tokens, norm_mix, norm_ffn, norm_final,
           conv_w_in, conv_w_dw, conv_w_out, pool_w, pool_scale, ffn_w_gate_up, ffn_w_down):
    d = D_MODEL
    nb, nt = N_SAMPLE_SEQ, SAMPLE_T
    batch = x_prompt.shape[0]

    h_p = x_prompt.reshape(batch * SEQ, d)
    h_s = jnp.concatenate([
        x_sample.transpose(1, 0, 2).reshape(nb * nt, d),
        meta_tokens.astype(F32),
        jnp.zeros((S_ROWS - nb * nt - N_META, d), F32),
    ], axis=0)
    hc = state_conv[0].transpose(1, 0, 2).reshape(CONV_HIST * nb, d)
    hp = jnp.concatenate([jnp.zeros((1, nb, d), F32), state_pool[0].transpose(1, 0, 2)],
                         axis=0).reshape((POOL_HIST + 1) * nb, d)

    row = lambda v: v.reshape(1, d)
    w_in = conv_w_in[0].astype(BF16)
    w_out = conv_w_out[0].astype(BF16)
    w_gu = [ffn_w_gate_up[l].astype(BF16) for l in range(2)]
    w_dn = [ffn_w_down[l].astype(BF16) for l in range(2)]
    w_pool = pool_w[0].astype(BF16)

    z_s, uts, umeta = _mix_conv_s(h_s, row(norm_mix[0]), w_in, conv_w_dw[0], hc, BN_MIX)
    z_p, utail_p = _mix_conv_p(h_p, row(norm_mix[0]), w_in, conv_w_dw[0], umeta, BM_P, BN_MIX)
    gdummy = row(norm_final)
    h_s = _proj_res(z_s, w_out, h_s, gdummy, BM_S, BK, False, "conv_out_s")
    h_p = _proj_res(z_p, w_out, h_p, gdummy, BM_P, BK, False, "conv_out_p")
    a_s = _gate_up(h_s, row(norm_ffn[0]), w_gu[0], BM_S, BN_FF, "gate_up0_s")
    a_p = _gate_up(h_p, row(norm_ffn[0]), w_gu[0], BM_P, BN_FF, "gate_up0_p")
    h_s = _proj_res(a_s, w_dn[0], h_s, gdummy, BM_S, BK, False, "down0_s")
    h_p = _proj_res(a_p, w_dn[0], h_p, gdummy, BM_P, BK, False, "down0_p")
    h_p1, ntail_p = _pool_p(h_p, h_s, row(norm_mix[1]), w_pool, row(pool_scale[0]), BM_POOL)
    h_s1, n_s = _pool_s(h_s, hp, row(norm_mix[1]), w_pool, row(pool_scale[0]))
    a_s = _gate_up(h_s1, row(norm_ffn[1]), w_gu[1], BM_S, BN_FF, "gate_up1_s")
    a_p = _gate_up(h_p1, row(norm_ffn[1]), w_gu[1], BM_P, BN_FF, "gate_up1_p")
    y_s = _proj_res(a_s, w_dn[1], h_s1, row(norm_final), BM_S, BK, True, "down1_s")
    y_p = _proj_res(a_p, w_dn[1], h_p1, row(norm_final), BM_P, BK, True, "down1_p")

    y_prompt = y_p.reshape(batch, SEQ, d)
    y_sample = y_s[:nb * nt].reshape(nt, nb, d).transpose(1, 0, 2)
    tps = SEQ // BM_P
    new_conv_prompt = utail_p.reshape(batch, tps, TAIL, d)[:, tps - 1, TAIL - CONV_HIST:, :][None]
    tpp = SEQ // BM_POOL
    new_pool_prompt = ntail_p.reshape(batch, tpp, HALO, d)[:, tpp - 1, HALO - POOL_HIST:, :][None]
    new_conv_sample = uts.reshape(CONV_HIST, nb, d).transpose(1, 0, 2)[None]
    new_pool_sample = jnp.concatenate(
        [state_pool[0][:, nt:, :], n_s.reshape(nt, nb, d).transpose(1, 0, 2)], axis=1)[None]
    return (y_prompt, y_sample, new_conv_prompt, new_pool_prompt, new_conv_sample, new_pool_sample)
```

```python
import functools

import jax
import jax.numpy as jnp
from jax import lax
from jax.experimental import pallas as pl
from jax.experimental.pallas import tpu as pltpu

D_MODEL = 2048
D_FF = 5632
N_META = 16
SEQ = 2048
N_SAMPLE_SEQ = 128
SAMPLE_T = 8
CONV_HIST = 2
POOL_WINDOWS = (2, 4, 8, 16)
POOL_GROUP = D_MODEL // len(POOL_WINDOWS)
POOL_HIST = 15
EPS = 1e-6

N_SAMPLE_ROWS = N_SAMPLE_SEQ * SAMPLE_T
S_ROWS = N_SAMPLE_ROWS + 2 * N_META
META_ROW0 = N_SAMPLE_ROWS
TAIL = 8
HALO = 16

V7X_SCOPED_VMEM_BYTES = 60000 * 1024

BF16 = jnp.bfloat16
F32 = jnp.float32


def _rms(x, g):
    ms = jnp.mean(x * x, axis=-1, keepdims=True)
    return (x * lax.rsqrt(ms + EPS)) * g


def _dot(a, b):
    return jnp.dot(a, b, preferred_element_type=F32)


def _params(n_axes):
    return pltpu.CompilerParams(
        dimension_semantics=("arbitrary",) * n_axes,
        vmem_limit_bytes=V7X_SCOPED_VMEM_BYTES,
    )


def _resident(shape, index_map):
    return pl.BlockSpec(shape, index_map, pipeline_mode=pl.Buffered(1))


def _conv3(w, u, ubuf, off):
    t = u.shape[0]
    return (w[2:3] * u + w[1:2] * ubuf[off - 1:off - 1 + t, :] + w[0:1] * ubuf[off - 2:off - 2 + t, :])


def _mix_p_kernel(tiles_per_seq, h_ref, g_ref, wb_ref, wc_ref, wv_ref, wdw_ref, umeta_ref,
                  z_ref, utail_ref, n_sc, ubuf, carry):
    i = pl.program_id(0)
    j = pl.program_id(1)
    bm = h_ref.shape[0]

    @pl.when(j == 0)
    def _():
        n_sc[...] = _rms(h_ref[...], g_ref[...]).astype(BF16)

    n = n_sc[...]
    b = _dot(n, wb_ref[...])
    u = _dot(n, wc_ref[...]) * _dot(n, wv_ref[...])

    @pl.when(i % tiles_per_seq == 0)
    def _():
        ubuf[0:TAIL, :] = umeta_ref[...]

    @pl.when(i % tiles_per_seq != 0)
    def _():
        ubuf[0:TAIL, :] = carry[j]

    ubuf[TAIL:TAIL + bm, :] = u
    tail = u[bm - TAIL:, :]
    carry[j] = tail
    utail_ref[...] = tail
    z_ref[...] = (b * _conv3(wdw_ref[...], u, ubuf, TAIL)).astype(BF16)


def _mix_s_kernel(h_ref, g_ref, wb_ref, wc_ref, wv_ref, wdw_ref, hc_ref,
                  z_ref, uts_ref, umeta_ref, wb_o, wc_o, wv_o, n_sc, ubuf, mbuf):
    j = pl.program_id(0)
    ns = N_SAMPLE_ROWS
    nb = N_SAMPLE_SEQ
    nm = S_ROWS - ns

    @pl.when(j == 0)
    def _():
        n_sc[...] = _rms(h_ref[...], g_ref[...]).astype(BF16)

    wb_o[...] = wb_ref[...].astype(BF16)
    wc_o[...] = wc_ref[...].astype(BF16)
    wv_o[...] = wv_ref[...].astype(BF16)
    n = n_sc[...]
    b = _dot(n, wb_o[...])
    u = _dot(n, wc_o[...]) * _dot(n, wv_o[...])
    w = wdw_ref[...]

    us = u[0:ns, :]
    ubuf[0:CONV_HIST * nb, :] = hc_ref[...]
    ubuf[CONV_HIST * nb:CONV_HIST * nb + ns, :] = us
    conv_s = (w[2:3] * us + w[1:2] * ubuf[nb:nb + ns, :] + w[0:1] * ubuf[0:ns, :])
    z_ref[0:ns, :] = (b[0:ns, :] * conv_s).astype(BF16)
    uts_ref[...] = us[ns - CONV_HIST * nb:, :]

    um = u[ns:, :]
    mbuf[0:TAIL, :] = jnp.zeros((TAIL, um.shape[1]), F32)
    mbuf[TAIL:TAIL + nm, :] = um
    z_ref[ns:, :] = (b[ns:, :] * _conv3(w, um, mbuf, TAIL)).astype(BF16)
    umeta_ref[...] = um[N_META - TAIL:N_META, :]


def _mix_conv_p(h, g, wb, wc, wv, w_dw, umeta, bm, bn):
    rows = h.shape[0]
    n_i, n_j = rows // bm, D_MODEL // bn
    wspec = pl.BlockSpec((D_MODEL, bn), lambda i, j: (0, j))
    return pl.pallas_call(
        functools.partial(_mix_p_kernel, SEQ // bm),
        grid=(n_i, n_j),
        in_specs=[
            pl.BlockSpec((bm, D_MODEL), lambda i, j: (i, 0)),
            pl.BlockSpec((1, D_MODEL), lambda i, j: (0, 0)),
            wspec, wspec, wspec,
            pl.BlockSpec((3, bn), lambda i, j: (0, j)),
            pl.BlockSpec((TAIL, bn), lambda i, j: (0, j)),
        ],
        out_specs=[
            pl.BlockSpec((bm, bn), lambda i, j: (i, j)),
            pl.BlockSpec((TAIL, bn), lambda i, j: (i, j)),
        ],
        out_shape=[
            jax.ShapeDtypeStruct((rows, D_MODEL), BF16),
            jax.ShapeDtypeStruct((n_i * TAIL, D_MODEL), F32),
        ],
        scratch_shapes=[
            pltpu.VMEM((bm, D_MODEL), BF16),
            pltpu.VMEM((TAIL + bm, bn), F32),
            pltpu.VMEM((n_j, TAIL, bn), F32),
        ],
        compiler_params=_params(2),
        name="mix_conv_p",
    )(h, g, wb, wc, wv, w_dw, umeta)


def _mix_conv_s(h, g, w_in, w_dw, hc, bn):
    n_j = D_MODEL // bn
    nm = S_ROWS - N_SAMPLE_ROWS
    wspec = lambda part: pl.BlockSpec((None, D_MODEL, bn), lambda j: (0, 0, part * n_j + j))
    wout = pl.BlockSpec((D_MODEL, bn), lambda j: (0, j))
    wshape = jax.ShapeDtypeStruct((D_MODEL, D_MODEL), BF16)
    return pl.pallas_call(
        _mix_s_kernel,
        grid=(n_j,),
        in_specs=[
            _resident((S_ROWS, D_MODEL), lambda j: (0, 0)),
            pl.BlockSpec((1, D_MODEL), lambda j: (0, 0)),
            wspec(0), wspec(1), wspec(2),
            pl.BlockSpec((3, bn), lambda j: (0, j)),
            pl.BlockSpec((CONV_HIST * N_SAMPLE_SEQ, bn), lambda j: (0, j)),
        ],
        out_specs=[
            pl.BlockSpec((S_ROWS, bn), lambda j: (0, j)),
            pl.BlockSpec((CONV_HIST * N_SAMPLE_SEQ, bn), lambda j: (0, j)),
            pl.BlockSpec((TAIL, bn), lambda j: (0, j)),
            wout, wout, wout,
        ],
        out_shape=[
            jax.ShapeDtypeStruct((S_ROWS, D_MODEL), BF16),
            jax.ShapeDtypeStruct((CONV_HIST * N_SAMPLE_SEQ, D_MODEL), F32),
            jax.ShapeDtypeStruct((TAIL, D_MODEL), F32),
            wshape, wshape, wshape,
        ],
        scratch_shapes=[
            pltpu.VMEM((S_ROWS, D_MODEL), BF16),
            pltpu.VMEM((CONV_HIST * N_SAMPLE_SEQ + N_SAMPLE_ROWS, bn), F32),
            pltpu.VMEM((TAIL + nm, bn), F32),
        ],
        compiler_params=_params(1),
        name="mix_conv_s",
    )(h, g, w_in, w_in, w_in, w_dw, hc)


def _proj_res_body(final_norm, k, n_k, x_ref, w_ref, h_ref, g_ref, o_ref):
    @pl.when(k == 0)
    def _():
        o_ref[...] = h_ref[...]

    o_ref[...] += _dot(x_ref[...], w_ref[...])

    if final_norm:
        @pl.when(k == n_k - 1)
        def _():
            o_ref[...] = _rms(o_ref[...], g_ref[...])


def _proj_res_p_kernel(final_norm, x_ref, w_ref, h_ref, g_ref, o_ref):
    _proj_res_body(final_norm, pl.program_id(1), pl.num_programs(1), x_ref, w_ref, h_ref, g_ref, o_ref)


def _proj_res_s_kernel(final_norm, x_ref, w_ref, h_ref, g_ref, o_ref, wo_ref):
    wo_ref[...] = w_ref[...].astype(BF16)
    _proj_res_body(final_norm, pl.program_id(0), pl.num_programs(0), x_ref, wo_ref, h_ref, g_ref, o_ref)


def _proj_res_p(x, w, h, g, bm, bk, final_norm, name):
    rows, kdim = x.shape
    n_i, n_k = rows // bm, kdim // bk
    return pl.pallas_call(
        functools.partial(_proj_res_p_kernel, final_norm),
        grid=(n_i, n_k),
        in_specs=[
            pl.BlockSpec((bm, bk), lambda i, k: (i, k)),
            pl.BlockSpec((bk, D_MODEL), lambda i, k: (k, 0)),
            pl.BlockSpec((bm, D_MODEL), lambda i, k: (i, 0)),
            pl.BlockSpec((1, D_MODEL), lambda i, k: (0, 0)),
        ],
        out_specs=pl.BlockSpec((bm, D_MODEL), lambda i, k: (i, 0)),
        out_shape=jax.ShapeDtypeStruct((rows, D_MODEL), F32),
        compiler_params=_params(2),
        name=name,
    )(x, w, h, g)


def _proj_res_s(x, w_stack, layer, h, g, bk, final_norm, name):
    rows, kdim = x.shape
    n_k = kdim // bk
    return pl.pallas_call(
        functools.partial(_proj_res_s_kernel, final_norm),
        grid=(n_k,),
        in_specs=[
            pl.BlockSpec((rows, bk), lambda k: (0, k)),
            pl.BlockSpec((None, bk, D_MODEL), lambda k: (layer, k, 0)),
            _resident((rows, D_MODEL), lambda k: (0, 0)),
            pl.BlockSpec((1, D_MODEL), lambda k: (0, 0)),
        ],
        out_specs=[
            pl.BlockSpec((rows, D_MODEL), lambda k: (0, 0)),
            pl.BlockSpec((bk, D_MODEL), lambda k: (k, 0)),
        ],
        out_shape=[
            jax.ShapeDtypeStruct((rows, D_MODEL), F32),
            jax.ShapeDtypeStruct((kdim, D_MODEL), BF16),
        ],
        compiler_params=_params(1),
        name=name,
    )(x, w_stack, h, g)


def _gate_up_body(j, h_ref, g_ref, wg_ref, wu_ref, a_ref, m_sc):
    @pl.when(j == 0)
    def _():
        m_sc[...] = _rms(h_ref[...], g_ref[...]).astype(BF16)

    m = m_sc[...]
    gate = _dot(m, wg_ref[...])
    up = _dot(m, wu_ref[...])
    a_ref[...] = (gate * (1.0 / (1.0 + jnp.exp(-gate))) * up).astype(BF16)


def _gate_up_p_kernel(h_ref, g_ref, wg_ref, wu_ref, a_ref, m_sc):
    _gate_up_body(pl.program_id(1), h_ref, g_ref, wg_ref, wu_ref, a_ref, m_sc)


def _gate_up_s_kernel(h_ref, g_ref, wg_ref, wu_ref, a_ref, wg_o, wu_o, m_sc):
    wg_o[...] = wg_ref[...].astype(BF16)
    wu_o[...] = wu_ref[...].astype(BF16)
    _gate_up_body(pl.program_id(0), h_ref, g_ref, wg_o, wu_o, a_ref, m_sc)


def _gate_up_p(h, g, wg, wu, bm, bn, name):
    rows = h.shape[0]
    n_i, n_j = rows // bm, D_FF // bn
    wspec = pl.BlockSpec((D_MODEL, bn), lambda i, j: (0, j))
    return pl.pallas_call(
        _gate_up_p_kernel,
        grid=(n_i, n_j),
        in_specs=[
            pl.BlockSpec((bm, D_MODEL), lambda i, j: (i, 0)),
            pl.BlockSpec((1, D_MODEL), lambda i, j: (0, 0)),
            wspec, wspec,
        ],
        out_specs=pl.BlockSpec((bm, bn), lambda i, j: (i, j)),
        out_shape=jax.ShapeDtypeStruct((rows, D_FF), BF16),
        scratch_shapes=[pltpu.VMEM((bm, D_MODEL), BF16)],
        compiler_params=_params(2),
        name=name,
    )(h, g, wg, wu)


def _gate_up_s(h, g, w_gu, layer, bn, name):
    rows = h.shape[0]
    n_j = D_FF // bn
    wout = pl.BlockSpec((D_MODEL, bn), lambda j: (0, j))
    wshape = jax.ShapeDtypeStruct((D_MODEL, D_FF), BF16)
    return pl.pallas_call(
        _gate_up_s_kernel,
        grid=(n_j,),
        in_specs=[
            _resident((rows, D_MODEL), lambda j: (0, 0)),
            pl.BlockSpec((1, D_MODEL), lambda j: (0, 0)),
            pl.BlockSpec((None, D_MODEL, bn), lambda j: (layer, 0, j)),
            pl.BlockSpec((None, D_MODEL, bn), lambda j: (layer, 0, n_j + j)),
        ],
        out_specs=[pl.BlockSpec((rows, bn), lambda j: (0, j)), wout, wout],
        out_shape=[jax.ShapeDtypeStruct((rows, D_FF), BF16), wshape, wshape],
        scratch_shapes=[pltpu.VMEM((rows, D_MODEL), BF16)],
        compiler_params=_params(1),
        name=name,
    )(h, g, w_gu, w_gu)


def _pool_p_kernel(tiles_per_seq, h_ref, halo_p_ref, halo_s_ref, g_ref, wp_ref, sc_ref,
                   o_ref, ntail_ref, nbuf):
    i = pl.program_id(0)
    bm = h_ref.shape[0]
    g = g_ref[...]
    x = h_ref[...]
    n = _rms(x, g)

    @pl.when(i % tiles_per_seq == 0)
    def _():
        nbuf[0:HALO, :] = _rms(halo_s_ref[...], g)

    @pl.when(i % tiles_per_seq != 0)
    def _():
        nbuf[0:HALO, :] = _rms(halo_p_ref[...], g)

    nbuf[HALO:HALO + bm, :] = n
    ntail_ref[...] = n[bm - HALO:, :]
    for gi, win in enumerate(POOL_WINDOWS):
        c0 = gi * POOL_GROUP
        ng = n[:, c0:c0 + POOL_GROUP]
        acc = ng
        for k in range(1, win):
            acc = acc + nbuf[HALO - k:HALO - k + bm, c0:c0 + POOL_GROUP]
        p = acc * (1.0 / win) - ng
        y = _dot(p.astype(BF16), wp_ref[gi].astype(BF16)) * sc_ref[:, c0:c0 + POOL_GROUP]
        o_ref[:, c0:c0 + POOL_GROUP] = x[:, c0:c0 + POOL_GROUP] + y


def _pool_p(h_p, h_s, g, wp, scale, bm):
    rows = h_p.shape[0]
    n_i = rows // bm
    ng = len(POOL_WINDOWS)
    return pl.pallas_call(
        functools.partial(_pool_p_kernel, SEQ // bm),
        grid=(n_i,),
        in_specs=[
            pl.BlockSpec((bm, D_MODEL), lambda i: (i, 0)),
            pl.BlockSpec((HALO, D_MODEL), lambda i: (jnp.maximum(i * (bm // HALO) - 1, 0), 0)),
            pl.BlockSpec((HALO, D_MODEL), lambda i: (META_ROW0 // HALO, 0)),
            pl.BlockSpec((1, D_MODEL), lambda i: (0, 0)),
            pl.BlockSpec((None, ng, POOL_GROUP, POOL_GROUP), lambda i: (0, 0, 0, 0)),
            pl.BlockSpec((1, D_MODEL), lambda i: (0, 0)),
        ],
        out_specs=[
            pl.BlockSpec((bm, D_MODEL), lambda i: (i, 0)),
            pl.BlockSpec((HALO, D_MODEL), lambda i: (i, 0)),
        ],
        out_shape=[
            jax.ShapeDtypeStruct((rows, D_MODEL), F32),
            jax.ShapeDtypeStruct((n_i * HALO, D_MODEL), F32),
        ],
        scratch_shapes=[pltpu.VMEM((HALO + bm, D_MODEL), F32)],
        compiler_params=_params(1),
        name="pool_p",
    )(h_p, h_p, h_s, g, wp, scale)


def _pool_s_kernel(hfull_ref, hcol_ref, hp_ref, g_ref, wp_ref, sc_ref,
                   o_ref, n_ref, inv_sc, nbuf, sum_sc):
    j = pl.program_id(0)
    ns = N_SAMPLE_ROWS
    nb = N_SAMPLE_SEQ
    nh = (POOL_HIST + 1) * nb

    @pl.when(j == 0)
    def _():
        xf = hfull_ref[...]
        inv_sc[...] = lax.rsqrt(jnp.mean(xf * xf, axis=-1, keepdims=True) + EPS)

    x = hcol_ref[...]
    n = ((x * inv_sc[...]) * g_ref[...])[0:ns, :]
    n_ref[...] = n
    nbuf[0:nh, :] = hp_ref[...]
    nbuf[nh:nh + ns, :] = n

    for gi, win in enumerate(POOL_WINDOWS):
        @pl.when(j == gi)
        def _(win=win):
            acc = n
            for k in range(1, win):
                acc = acc + nbuf[nh - k * nb:nh - k * nb + ns, :]
            sum_sc[...] = acc * (1.0 / win)

    p = sum_sc[...] - n
    y = _dot(p.astype(BF16), wp_ref[...].astype(BF16)) * sc_ref[...]
    o_ref[0:ns, :] = x[0:ns, :] + y
    o_ref[ns:, :] = x[ns:, :]


def _pool_s(h_s, hp, g, wp, scale):
    ns = N_SAMPLE_ROWS
    nh = (POOL_HIST + 1) * N_SAMPLE_SEQ
    pg = POOL_GROUP
    return pl.pallas_call(
        _pool_s_kernel,
        grid=(len(POOL_WINDOWS),),
        in_specs=[
            _resident((S_ROWS, D_MODEL), lambda j: (0, 0)),
            pl.BlockSpec((S_ROWS, pg), lambda j: (0, j)),
            pl.BlockSpec((nh, pg), lambda j: (0, j)),
            pl.BlockSpec((1, pg), lambda j: (0, j)),
            pl.BlockSpec((None, None, pg, pg), lambda j: (0, j, 0, 0)),
            pl.BlockSpec((1, pg), lambda j: (0, j)),
        ],
        out_specs=[
            pl.BlockSpec((S_ROWS, pg), lambda j: (0, j)),
            pl.BlockSpec((ns, pg), lambda j: (0, j)),
        ],
        out_shape=[
            jax.ShapeDtypeStruct((S_ROWS, D_MODEL), F32),
            jax.ShapeDtypeStruct((ns, D_MODEL), F32),
        ],
        scratch_shapes=[
            pltpu.VMEM((S_ROWS, 1), F32),
            pltpu.VMEM((nh + ns, pg), F32),
            pltpu.VMEM((ns, pg), F32),
        ],
        compiler_params=_params(1),
        name="pool_s",
    )(h_s, h_s, hp, g, wp, scale)


BM_P = 1024
BM_POOL = 512
BN_P = 512
BN_S = 256


def kernel(x_prompt, x_sample, state_conv, state_pool, meta_tokens, norm_mix, norm_ffn, norm_final,
           conv_w_in, conv_w_dw, conv_w_out, pool_w, pool_scale, ffn_w_gate_up, ffn_w_down):
    d = D_MODEL
    nb, nt = N_SAMPLE_SEQ, SAMPLE_T
    batch = x_prompt.shape[0]

    h_p = x_prompt.reshape(batch * SEQ, d)
    h_s = jnp.concatenate([
        x_sample.transpose(1, 0, 2).reshape(nb * nt, d),
        meta_tokens.astype(F32),
        jnp.zeros((S_ROWS - nb * nt - N_META, d), F32),
    ], axis=0)
    hc = state_conv[0].transpose(1, 0, 2).reshape(CONV_HIST * nb, d)
    hp = jnp.concatenate([jnp.zeros((1, nb, d), F32), state_pool[0].transpose(1, 0, 2)],
                         axis=0).reshape((POOL_HIST + 1) * nb, d)
    row = lambda v: v.reshape(1, d)
    g_mix, g_ffn, g_fin = norm_mix, norm_ffn, row(norm_final)

    z_s, uts, umeta, wb, wc, wv = _mix_conv_s(h_s, row(g_mix[0]), conv_w_in, conv_w_dw[0], hc, BN_S)
    z_p, utail_p = _mix_conv_p(h_p, row(g_mix[0]), wb, wc, wv, conv_w_dw[0], umeta, BM_P, BN_P)
    h_s, w_out = _proj_res_s(z_s, conv_w_out, 0, h_s, g_fin, BN_S, False, "conv_out_s")
    h_p = _proj_res_p(z_p, w_out, h_p, g_fin, BM_P, BN_P, False, "conv_out_p")
    a_s, wg, wu = _gate_up_s(h_s, row(g_ffn[0]), ffn_w_gate_up, 0, BN_S, "gate_up0_s")
    a_p = _gate_up_p(h_p, row(g_ffn[0]), wg, wu, BM_P, BN_P, "gate_up0_p")
    h_s, w_dn = _proj_res_s(a_s, ffn_w_down, 0, h_s, g_fin, BN_S, False, "down0_s")
    h_p = _proj_res_p(a_p, w_dn, h_p, g_fin, BM_P, BN_P, False, "down0_p")
    h_p1, ntail_p = _pool_p(h_p, h_s, row(g_mix[1]), pool_w, row(pool_scale[0]), BM_POOL)
    h_s1, n_s = _pool_s(h_s, hp, row(g_mix[1]), pool_w, row(pool_scale[0]))
    a_s, wg, wu = _gate_up_s(h_s1, row(g_ffn[1]), ffn_w_gate_up, 1, BN_S, "gate_up1_s")
    a_p = _gate_up_p(h_p1, row(g_ffn[1]), wg, wu, BM_P, BN_P, "gate_up1_p")
    y_s, w_dn = _proj_res_s(a_s, ffn_w_down, 1, h_s1, g_fin, BN_S, True, "down1_s")
    y_p = _proj_res_p(a_p, w_dn, h_p1, g_fin, BM_P, BN_P, True, "down1_p")

    y_prompt = y_p.reshape(batch, SEQ, d)
    y_sample = y_s[:nb * nt].reshape(nt, nb, d).transpose(1, 0, 2)
    tps = SEQ // BM_P
    new_conv_prompt = utail_p.reshape(batch, tps, TAIL, d)[:, tps - 1, TAIL - CONV_HIST:, :][None]
    tpp = SEQ // BM_POOL
    new_pool_prompt = ntail_p.reshape(batch, tpp, HALO, d)[:, tpp - 1, HALO - POOL_HIST:, :][None]
    new_conv_sample = uts.reshape(CONV_HIST, nb, d).transpose(1, 0, 2)[None]
    new_pool_sample = jnp.concatenate(
        [state_pool[0][:, nt:, :], n_s.reshape(nt, nb, d).transpose(1, 0, 2)], axis=1)[None]
    return (y_prompt, y_sample, new_conv_prompt, new_pool_prompt, new_conv_sample, new_pool_sample)
```

```python
import functools

import jax
import jax.numpy as jnp
from jax import lax
from jax.experimental import pallas as pl
from jax.experimental.pallas import tpu as pltpu

D_MODEL = 2048
D_FF = 5632
N_META = 16
SEQ = 2048
N_SAMPLE_SEQ = 128
SAMPLE_T = 8
CONV_HIST = 2
POOL_WINDOWS = (2, 4, 8, 16)
POOL_GROUP = D_MODEL // len(POOL_WINDOWS)
POOL_HIST = 15
EPS = 1e-6

N_SAMPLE_ROWS = N_SAMPLE_SEQ * SAMPLE_T
S_ROWS = N_SAMPLE_ROWS + 2 * N_META
META_ROW0 = N_SAMPLE_ROWS
TAIL = 8
HALO = 16
MXU_COLS = 256
FRONT = 8

V7X_SCOPED_VMEM_BYTES = 60000 * 1024

BF16 = jnp.bfloat16
F32 = jnp.float32


def _rms(x, g):
    ms = jnp.mean(x * x, axis=-1, keepdims=True)
    return (x * lax.rsqrt(ms + EPS)) * g


def _dot(a, b):
    return jnp.dot(a, b, preferred_element_type=F32)


def _params(n_axes):
    return pltpu.CompilerParams(
        dimension_semantics=("arbitrary",) * n_axes,
        vmem_limit_bytes=V7X_SCOPED_VMEM_BYTES,
    )


def _resident(shape, index_map):
    return pl.BlockSpec(shape, index_map, pipeline_mode=pl.Buffered(1))


def _conv3(w, u, ubuf, off):
    t = u.shape[0]
    return (w[2:3] * u + w[1:2] * ubuf[off - 1:off - 1 + t, :] + w[0:1] * ubuf[off - 2:off - 2 + t, :])


def _mix_p_kernel(tiles_per_seq, h_ref, g_ref, wb_ref, wc_ref, wv_ref, wdw_ref, umeta_ref,
                  z_ref, utail_ref, n_sc, ubuf, carry):
    i = pl.program_id(0)
    j = pl.program_id(1)
    bm = h_ref.shape[0]

    @pl.when(j == 0)
    def _():
        n_sc[...] = _rms(h_ref[...], g_ref[...]).astype(BF16)

    @pl.when(i % tiles_per_seq == 0)
    def _():
        ubuf[0:TAIL, :] = umeta_ref[...]

    @pl.when(i % tiles_per_seq != 0)
    def _():
        ubuf[0:TAIL, :] = carry[j]

    n = n_sc[...]
    for c0 in range(0, z_ref.shape[1], MXU_COLS):
        cols = slice(c0, c0 + MXU_COLS)
        b = _dot(n, wb_ref[:, cols])
        u = _dot(n, wc_ref[:, cols]) * _dot(n, wv_ref[:, cols])
        ubuf[TAIL:TAIL + bm, cols] = u
        z_ref[:, cols] = (b * _conv3(wdw_ref[:, cols], u, ubuf.at[:, cols], TAIL)).astype(BF16)
    tail = ubuf[bm:bm + TAIL, :]
    carry[j] = tail
    utail_ref[...] = tail


def _mix_s_kernel(h_ref, g_ref, wb_ref, wc_ref, wv_ref, wdw_ref, hc_ref,
                  z_ref, uts_ref, umeta_ref, wb_o, wc_o, wv_o, n_sc, ubuf, mbuf):
    j = pl.program_id(0)
    ns = N_SAMPLE_ROWS
    nb = N_SAMPLE_SEQ
    nm = S_ROWS - ns

    @pl.when(j == 0)
    def _():
        n_sc[...] = _rms(h_ref[...], g_ref[...]).astype(BF16)

    wb_o[...] = wb_ref[...].astype(BF16)
    wc_o[...] = wc_ref[...].astype(BF16)
    wv_o[...] = wv_ref[...].astype(BF16)
    n = n_sc[...]
    nh = CONV_HIST * nb
    ubuf[0:nh, :] = hc_ref[...]
    mbuf[0:TAIL, :] = jnp.zeros((TAIL, mbuf.shape[1]), F32)
    for c0 in range(0, z_ref.shape[1], MXU_COLS):
        cols = slice(c0, c0 + MXU_COLS)
        b = _dot(n, wb_o[:, cols])
        u = _dot(n, wc_o[:, cols]) * _dot(n, wv_o[:, cols])
        w = wdw_ref[:, cols]

        us = u[0:ns, :]
        ubuf[nh:nh + ns, cols] = us
        conv_s = (w[2:3] * us + w[1:2] * ubuf[nb:nb + ns, cols] + w[0:1] * ubuf[0:ns, cols])
        z_ref[0:ns, cols] = (b[0:ns, :] * conv_s).astype(BF16)

        um = u[ns:, :]
        mbuf[TAIL:TAIL + nm, cols] = um
        z_ref[ns:, cols] = (b[ns:, :] * _conv3(w, um, mbuf.at[:, cols], TAIL)).astype(BF16)
    uts_ref[...] = ubuf[ns:ns + nh, :]
    umeta_ref[...] = mbuf[N_META:N_META + TAIL, :]


def _mix_conv_p(h, g, wb, wc, wv, w_dw, umeta, bm, bn):
    rows = h.shape[0]
    n_i, n_j = rows // bm, D_MODEL // bn
    wspec = pl.BlockSpec((D_MODEL, bn), lambda i, j: (0, j))
    return pl.pallas_call(
        functools.partial(_mix_p_kernel, SEQ // bm),
        grid=(n_i, n_j),
        in_specs=[
            pl.BlockSpec((bm, D_MODEL), lambda i, j: (i, 0)),
            pl.BlockSpec((1, D_MODEL), lambda i, j: (0, 0)),
            wspec, wspec, wspec,
            pl.BlockSpec((3, bn), lambda i, j: (0, j)),
            pl.BlockSpec((TAIL, bn), lambda i, j: (0, j)),
        ],
        out_specs=[
            pl.BlockSpec((bm, bn), lambda i, j: (i, j)),
            pl.BlockSpec((TAIL, bn), lambda i, j: (i, j)),
        ],
        out_shape=[
            jax.ShapeDtypeStruct((rows, D_MODEL), BF16),
            jax.ShapeDtypeStruct((n_i * TAIL, D_MODEL), F32),
        ],
        scratch_shapes=[
            pltpu.VMEM((bm, D_MODEL), BF16),
            pltpu.VMEM((TAIL + bm, bn), F32),
            pltpu.VMEM((n_j, TAIL, bn), F32),
        ],
        compiler_params=_params(2),
        name="mix_conv_p",
    )(h, g, wb, wc, wv, w_dw, umeta)


def _mix_conv_s(h, g, w_in, w_dw, hc, bn):
    n_j = D_MODEL // bn
    nm = S_ROWS - N_SAMPLE_ROWS
    wspec = lambda part: pl.BlockSpec((None, D_MODEL, bn), lambda j: (0, 0, part * n_j + j))
    wout = pl.BlockSpec((D_MODEL, bn), lambda j: (0, j))
    wshape = jax.ShapeDtypeStruct((D_MODEL, D_MODEL), BF16)
    return pl.pallas_call(
        _mix_s_kernel,
        grid=(n_j,),
        in_specs=[
            _resident((S_ROWS, D_MODEL), lambda j: (0, 0)),
            pl.BlockSpec((1, D_MODEL), lambda j: (0, 0)),
            wspec(0), wspec(1), wspec(2),
            pl.BlockSpec((3, bn), lambda j: (0, j)),
            pl.BlockSpec((CONV_HIST * N_SAMPLE_SEQ, bn), lambda j: (0, j)),
        ],
        out_specs=[
            pl.BlockSpec((S_ROWS, bn), lambda j: (0, j)),
            pl.BlockSpec((CONV_HIST * N_SAMPLE_SEQ, bn), lambda j: (0, j)),
            pl.BlockSpec((TAIL, bn), lambda j: (0, j)),
            wout, wout, wout,
        ],
        out_shape=[
            jax.ShapeDtypeStruct((S_ROWS, D_MODEL), BF16),
            jax.ShapeDtypeStruct((CONV_HIST * N_SAMPLE_SEQ, D_MODEL), F32),
            jax.ShapeDtypeStruct((TAIL, D_MODEL), F32),
            wshape, wshape, wshape,
        ],
        scratch_shapes=[
            pltpu.VMEM((S_ROWS, D_MODEL), BF16),
            pltpu.VMEM((CONV_HIST * N_SAMPLE_SEQ + N_SAMPLE_ROWS, bn), F32),
            pltpu.VMEM((TAIL + nm, bn), F32),
        ],
        compiler_params=_params(1),
        name="mix_conv_s",
    )(h, g, w_in, w_in, w_in, w_dw, hc)


def _proj_res_body(final_norm, k, n_k, x_ref, w_ref, h_ref, g_ref, o_ref):
    if n_k == 1:
        acc = h_ref[...] + _dot(x_ref[...], w_ref[...])
        o_ref[...] = _rms(acc, g_ref[...]) if final_norm else acc
        return

    @pl.when(k == 0)
    def _():
        o_ref[...] = h_ref[...] + _dot(x_ref[...], w_ref[...])

    @pl.when(k != 0)
    def _():
        o_ref[...] += _dot(x_ref[...], w_ref[...])

    if final_norm:
        @pl.when(k == n_k - 1)
        def _():
            o_ref[...] = _rms(o_ref[...], g_ref[...])


def _proj_res_p_kernel(final_norm, n_k, x_ref, w_ref, h_ref, g_ref, o_ref):
    _proj_res_body(final_norm, pl.program_id(1), n_k, x_ref, w_ref, h_ref, g_ref, o_ref)


def _proj_res_s_kernel(final_norm, n_k, x_ref, w_ref, h_ref, g_ref, o_ref, wo_ref):
    wo_ref[...] = w_ref[...].astype(BF16)
    _proj_res_body(final_norm, pl.program_id(0), n_k, x_ref, wo_ref, h_ref, g_ref, o_ref)


def _proj_res_p(x, w, h, g, bm, bk, final_norm, name):
    rows, kdim = x.shape
    n_i, n_k = rows // bm, kdim // bk
    wspec = _resident if n_k == 1 else pl.BlockSpec
    return pl.pallas_call(
        functools.partial(_proj_res_p_kernel, final_norm, n_k),
        grid=(n_i, n_k),
        in_specs=[
            pl.BlockSpec((bm, bk), lambda i, k: (i, k)),
            wspec((bk, D_MODEL), lambda i, k: (k, 0)),
            pl.BlockSpec((bm, D_MODEL), lambda i, k: (i, 0)),
            pl.BlockSpec((1, D_MODEL), lambda i, k: (0, 0)),
        ],
        out_specs=pl.BlockSpec((bm, D_MODEL), lambda i, k: (i, 0)),
        out_shape=jax.ShapeDtypeStruct((rows, D_MODEL), F32),
        compiler_params=_params(2),
        name=name,
    )(x, w, h, g)


def _proj_res_s(x, w_stack, layer, h, g, bk, final_norm, name):
    rows, kdim = x.shape
    n_k = kdim // bk
    return pl.pallas_call(
        functools.partial(_proj_res_s_kernel, final_norm, n_k),
        grid=(n_k,),
        in_specs=[
            pl.BlockSpec((rows, bk), lambda k: (0, k)),
            pl.BlockSpec((None, bk, D_MODEL), lambda k: (layer, k, 0)),
            _resident((rows, D_MODEL), lambda k: (0, 0)),
            pl.BlockSpec((1, D_MODEL), lambda k: (0, 0)),
        ],
        out_specs=[
            pl.BlockSpec((rows, D_MODEL), lambda k: (0, 0)),
            pl.BlockSpec((bk, D_MODEL), lambda k: (k, 0)),
        ],
        out_shape=[
            jax.ShapeDtypeStruct((rows, D_MODEL), F32),
            jax.ShapeDtypeStruct((kdim, D_MODEL), BF16),
        ],
        compiler_params=_params(1),
        name=name,
    )(x, w_stack, h, g)


def _gate_up_body(j, h_ref, g_ref, wg_ref, wu_ref, a_ref, m_sc):
    @pl.when(j == 0)
    def _():
        m_sc[...] = _rms(h_ref[...], g_ref[...]).astype(BF16)

    m = m_sc[...]
    for c0 in range(0, a_ref.shape[1], MXU_COLS):
        cols = slice(c0, c0 + MXU_COLS)
        gate = _dot(m, wg_ref[:, cols])
        up = _dot(m, wu_ref[:, cols])
        a_ref[:, cols] = (gate * (1.0 / (1.0 + jnp.exp(-gate))) * up).astype(BF16)


def _gate_up_p_kernel(h_ref, g_ref, wg_ref, wu_ref, a_ref, m_sc):
    _gate_up_body(pl.program_id(1), h_ref, g_ref, wg_ref, wu_ref, a_ref, m_sc)


def _gate_up_s_kernel(h_ref, g_ref, wg_ref, wu_ref, a_ref, wg_o, wu_o, m_sc):
    wg_o[...] = wg_ref[...].astype(BF16)
    wu_o[...] = wu_ref[...].astype(BF16)
    _gate_up_body(pl.program_id(0), h_ref, g_ref, wg_o, wu_o, a_ref, m_sc)


def _gate_up_p(h, g, wg, wu, bm, bn, name):
    rows = h.shape[0]
    n_i, n_j = rows // bm, D_FF // bn
    wspec = pl.BlockSpec((D_MODEL, bn), lambda i, j: (0, j))
    return pl.pallas_call(
        _gate_up_p_kernel,
        grid=(n_i, n_j),
        in_specs=[
            pl.BlockSpec((bm, D_MODEL), lambda i, j: (i, 0)),
            pl.BlockSpec((1, D_MODEL), lambda i, j: (0, 0)),
            wspec, wspec,
        ],
        out_specs=pl.BlockSpec((bm, bn), lambda i, j: (i, j)),
        out_shape=jax.ShapeDtypeStruct((rows, D_FF), BF16),
        scratch_shapes=[pltpu.VMEM((bm, D_MODEL), BF16)],
        compiler_params=_params(2),
        name=name,
    )(h, g, wg, wu)


def _gate_up_s(h, g, w_gu, layer, bn, name):
    rows = h.shape[0]
    n_j = D_FF // bn
    wout = pl.BlockSpec((D_MODEL, bn), lambda j: (0, j))
    wshape = jax.ShapeDtypeStruct((D_MODEL, D_FF), BF16)
    return pl.pallas_call(
        _gate_up_s_kernel,
        grid=(n_j,),
        in_specs=[
            _resident((rows, D_MODEL), lambda j: (0, 0)),
            pl.BlockSpec((1, D_MODEL), lambda j: (0, 0)),
            pl.BlockSpec((None, D_MODEL, bn), lambda j: (layer, 0, j)),
            pl.BlockSpec((None, D_MODEL, bn), lambda j: (layer, 0, n_j + j)),
        ],
        out_specs=[pl.BlockSpec((rows, bn), lambda j: (0, j)), wout, wout],
        out_shape=[jax.ShapeDtypeStruct((rows, D_FF), BF16), wshape, wshape],
        scratch_shapes=[pltpu.VMEM((rows, D_MODEL), BF16)],
        compiler_params=_params(1),
        name=name,
    )(h, g, w_gu, w_gu)


def _pool_p_kernel(tiles_per_seq, h_ref, halo_p_ref, halo_s_ref, g_ref, wp_ref, sc_ref,
                   o_ref, ntail_ref, nbuf, pbuf, qbuf):
    i = pl.program_id(0)
    bm = h_ref.shape[0]
    g = g_ref[...]
    x = h_ref[...]
    n = _rms(x, g)
    rows = HALO + bm
    r0 = FRONT + HALO

    for buf in (nbuf, pbuf, qbuf):
        buf[0:FRONT, :] = jnp.zeros((FRONT, buf.shape[1]), F32)

    @pl.when(i % tiles_per_seq == 0)
    def _():
        nbuf[FRONT:r0, :] = _rms(halo_s_ref[...], g)

    @pl.when(i % tiles_per_seq != 0)
    def _():
        nbuf[FRONT:r0, :] = _rms(halo_p_ref[...], g)

    nbuf[r0:r0 + bm, :] = n
    ntail_ref[...] = n[bm - HALO:, :]
    for gi, win in enumerate(POOL_WINDOWS):
        cols = slice(gi * POOL_GROUP, (gi + 1) * POOL_GROUP)
        src, shift, level = nbuf, 1, 0
        while shift < win:
            dst = (pbuf, qbuf)[level % 2]
            dst[FRONT:FRONT + rows, cols] = (src[FRONT:FRONT + rows, cols]
                                             + src[FRONT - shift:FRONT - shift + rows, cols])
            src, shift, level = dst, 2 * shift, level + 1
        ng = n[:, cols]
        p = src[r0:r0 + bm, cols] * (1.0 / win) - ng
        y = _dot(p.astype(BF16), wp_ref[gi].astype(BF16)) * sc_ref[:, cols]
        o_ref[:, cols] = x[:, cols] + y


def _pool_p(h_p, h_s, g, wp, scale, bm):
    rows = h_p.shape[0]
    n_i = rows // bm
    ng = len(POOL_WINDOWS)
    return pl.pallas_call(
        functools.partial(_pool_p_kernel, SEQ // bm),
        grid=(n_i,),
        in_specs=[
            pl.BlockSpec((bm, D_MODEL), lambda i: (i, 0)),
            pl.BlockSpec((HALO, D_MODEL), lambda i: (jnp.maximum(i * (bm // HALO) - 1, 0), 0)),
            pl.BlockSpec((HALO, D_MODEL), lambda i: (META_ROW0 // HALO, 0)),
            pl.BlockSpec((1, D_MODEL), lambda i: (0, 0)),
            pl.BlockSpec((None, ng, POOL_GROUP, POOL_GROUP), lambda i: (0, 0, 0, 0)),
            pl.BlockSpec((1, D_MODEL), lambda i: (0, 0)),
        ],
        out_specs=[
            pl.BlockSpec((bm, D_MODEL), lambda i: (i, 0)),
            pl.BlockSpec((HALO, D_MODEL), lambda i: (i, 0)),
        ],
        out_shape=[
            jax.ShapeDtypeStruct((rows, D_MODEL), F32),
            jax.ShapeDtypeStruct((n_i * HALO, D_MODEL), F32),
        ],
        scratch_shapes=[pltpu.VMEM((FRONT + HALO + bm, D_MODEL), F32)] * 3,
        compiler_params=_params(1),
        name="pool_p",
    )(h_p, h_p, h_s, g, wp, scale)


def _pool_s_kernel(hfull_ref, hcol_ref, hp_ref, g_ref, wp_ref, sc_ref,
                   o_ref, n_ref, inv_sc, nbuf, sum_sc):
    j = pl.program_id(0)
    ns = N_SAMPLE_ROWS
    nb = N_SAMPLE_SEQ
    nh = (POOL_HIST + 1) * nb

    @pl.when(j == 0)
    def _():
        xf = hfull_ref[...]
        inv_sc[...] = lax.rsqrt(jnp.mean(xf * xf, axis=-1, keepdims=True) + EPS)

    x = hcol_ref[...]
    n = ((x * inv_sc[...]) * g_ref[...])[0:ns, :]
    n_ref[...] = n
    nbuf[0:nh, :] = hp_ref[...]
    nbuf[nh:nh + ns, :] = n

    for gi, win in enumerate(POOL_WINDOWS):
        @pl.when(j == gi)
        def _(win=win):
            acc = n
            for k in range(1, win):
                acc = acc + nbuf[nh - k * nb:nh - k * nb + ns, :]
            sum_sc[...] = acc * (1.0 / win)

    p = sum_sc[...] - n
    y = _dot(p.astype(BF16), wp_ref[...].astype(BF16)) * sc_ref[...]
    o_ref[0:ns, :] = x[0:ns, :] + y
    o_ref[ns:, :] = x[ns:, :]


def _pool_s(h_s, hp, g, wp, scale):
    ns = N_SAMPLE_ROWS
    nh = (POOL_HIST + 1) * N_SAMPLE_SEQ
    pg = POOL_GROUP
    return pl.pallas_call(
        _pool_s_kernel,
        grid=(len(POOL_WINDOWS),),
        in_specs=[
            _resident((S_ROWS, D_MODEL), lambda j: (0, 0)),
            pl.BlockSpec((S_ROWS, pg), lambda j: (0, j)),
            pl.BlockSpec((nh, pg), lambda j: (0, j)),
            pl.BlockSpec((1, pg), lambda j: (0, j)),
            pl.BlockSpec((None, None, pg, pg), lambda j: (0, j, 0, 0)),
            pl.BlockSpec((1, pg), lambda j: (0, j)),
        ],
        out_specs=[
            pl.BlockSpec((S_ROWS, pg), lambda j: (0, j)),
            pl.BlockSpec((ns, pg), lambda j: (0, j)),
        ],
        out_shape=[
            jax.ShapeDtypeStruct((S_ROWS, D_MODEL), F32),
            jax.ShapeDtypeStruct((ns, D_MODEL), F32),
        ],
        scratch_shapes=[
            pltpu.VMEM((S_ROWS, 1), F32),
            pltpu.VMEM((nh + ns, pg), F32),
            pltpu.VMEM((ns, pg), F32),
        ],
        compiler_params=_params(1),
        name="pool_s",
    )(h_s, h_s, hp, g, wp, scale)


BM_P = 1024
BM_POOL = 512
BM_CONV_OUT = 512
BN_P = 512
BM_DOWN = 512
BK_DOWN = D_FF // 2
BN_S = 256
BN_S_FF = 512
BK_S = 512


def kernel(x_prompt, x_sample, state_conv, state_pool, meta_tokens, norm_mix, norm_ffn, norm_final,
           conv_w_in, conv_w_dw, conv_w_out, pool_w, pool_scale, ffn_w_gate_up, ffn_w_down):
    d = D_MODEL
    nb, nt = N_SAMPLE_SEQ, SAMPLE_T
    batch = x_prompt.shape[0]

    h_p = x_prompt.reshape(batch * SEQ, d)
    h_s = jnp.concatenate([
        x_sample.transpose(1, 0, 2).reshape(nb * nt, d),
        meta_tokens.astype(F32),
        jnp.zeros((S_ROWS - nb * nt - N_META, d), F32),
    ], axis=0)
    hc = state_conv[0].transpose(1, 0, 2).reshape(CONV_HIST * nb, d)
    hp = jnp.concatenate([jnp.zeros((1, nb, d), F32), state_pool[0].transpose(1, 0, 2)],
                         axis=0).reshape((POOL_HIST + 1) * nb, d)
    row = lambda v: v.reshape(1, d)
    g_mix, g_ffn, g_fin = norm_mix, norm_ffn, row(norm_final)

    z_s, uts, umeta, wb, wc, wv = _mix_conv_s(h_s, row(g_mix[0]), conv_w_in, conv_w_dw[0], hc, BN_S)
    z_p, utail_p = _mix_conv_p(h_p, row(g_mix[0]), wb, wc, wv, conv_w_dw[0], umeta, BM_P, BN_P)
    h_s, w_out = _proj_res_s(z_s, conv_w_out, 0, h_s, g_fin, BK_S, False, "conv_out_s")
    h_p = _proj_res_p(z_p, w_out, h_p, g_fin, BM_CONV_OUT, D_MODEL, False, "conv_out_p")
    a_s, wg, wu = _gate_up_s(h_s, row(g_ffn[0]), ffn_w_gate_up, 0, BN_S_FF, "gate_up0_s")
    a_p = _gate_up_p(h_p, row(g_ffn[0]), wg, wu, BM_P, BN_P, "gate_up0_p")
    h_s, w_dn = _proj_res_s(a_s, ffn_w_down, 0, h_s, g_fin, BK_S, False, "down0_s")
    h_p = _proj_res_p(a_p, w_dn, h_p, g_fin, BM_DOWN, BK_DOWN, False, "down0_p")
    h_p1, ntail_p = _pool_p(h_p, h_s, row(g_mix[1]), pool_w, row(pool_scale[0]), BM_POOL)
    h_s1, n_s = _pool_s(h_s, hp, row(g_mix[1]), pool_w, row(pool_scale[0]))
    a_s, wg, wu = _gate_up_s(h_s1, row(g_ffn[1]), ffn_w_gate_up, 1, BN_S_FF, "gate_up1_s")
    a_p = _gate_up_p(h_p1, row(g_ffn[1]), wg, wu, BM_P, BN_P, "gate_up1_p")
    y_s, w_dn = _proj_res_s(a_s, ffn_w_down, 1, h_s1, g_fin, BK_S, True, "down1_s")
    y_p = _proj_res_p(a_p, w_dn, h_p1, g_fin, BM_DOWN, BK_DOWN, True, "down1_p")

    y_prompt = y_p.reshape(batch, SEQ, d)
    y_sample = y_s[:nb * nt].reshape(nt, nb, d).transpose(1, 0, 2)
    tps = SEQ // BM_P
    new_conv_prompt = utail_p.reshape(batch, tps, TAIL, d)[:, tps - 1, TAIL - CONV_HIST:, :][None]
    tpp = SEQ // BM_POOL
    new_pool_prompt = ntail_p.reshape(batch, tpp, HALO, d)[:, tpp - 1, HALO - POOL_HIST:, :][None]
    new_conv_sample = uts.reshape(CONV_HIST, nb, d).transpose(1, 0, 2)[None]
    new_pool_sample = jnp.concatenate(
        [state_pool[0][:, nt:, :], n_s.reshape(nt, nb, d).transpose(1, 0, 2)], axis=1)[None]
    return (y_prompt, y_sample, new_conv_prompt, new_pool_prompt, new_conv_sample, new_pool_sample)
```

```python
import functools

import jax
import jax.numpy as jnp
from jax import lax
from jax.experimental import pallas as pl
from jax.experimental.pallas import tpu as pltpu

D_MODEL = 2048
D_FF = 5632
N_META = 16
SEQ = 2048
N_SAMPLE_SEQ = 128
SAMPLE_T = 8
CONV_HIST = 2
POOL_WINDOWS = (2, 4, 8, 16)
POOL_GROUP = D_MODEL // len(POOL_WINDOWS)
POOL_HIST = 15
EPS = 1e-6

N_SAMPLE_ROWS = N_SAMPLE_SEQ * SAMPLE_T
S_ROWS = N_SAMPLE_ROWS + 2 * N_META
META_ROW0 = N_SAMPLE_ROWS
TAIL = 8
HALO = 16
MXU_COLS = 256
FRONT = 8

V7X_SCOPED_VMEM_BYTES = 60000 * 1024

BF16 = jnp.bfloat16
F32 = jnp.float32


def _rms(x, g):
    ms = jnp.mean(x * x, axis=-1, keepdims=True)
    return (x * lax.rsqrt(ms + EPS)) * g


def _dot(a, b):
    return jnp.dot(a, b, preferred_element_type=F32)


def _params(n_axes):
    return pltpu.CompilerParams(
        dimension_semantics=("arbitrary",) * n_axes,
        vmem_limit_bytes=V7X_SCOPED_VMEM_BYTES,
    )


def _resident(shape, index_map):
    return pl.BlockSpec(shape, index_map, pipeline_mode=pl.Buffered(1))


def _conv3(w, u, ubuf, off):
    t = u.shape[0]
    return (w[2:3] * u + w[1:2] * ubuf[off - 1:off - 1 + t, :] + w[0:1] * ubuf[off - 2:off - 2 + t, :])


def _mix_p_kernel(tiles_per_seq, h_ref, g_ref, wb_ref, wc_ref, wv_ref, wdw_ref, umeta_ref,
                  z_ref, utail_ref, n_sc, ubuf, carry):
    i = pl.program_id(0)
    j = pl.program_id(1)
    bm = h_ref.shape[0]

    @pl.when(j == 0)
    def _():
        n_sc[...] = _rms(h_ref[...], g_ref[...]).astype(BF16)

    @pl.when(i % tiles_per_seq == 0)
    def _():
        ubuf[0:TAIL, :] = umeta_ref[...]

    @pl.when(i % tiles_per_seq != 0)
    def _():
        ubuf[0:TAIL, :] = carry[j]

    n = n_sc[...]
    for c0 in range(0, z_ref.shape[1], MXU_COLS):
        cols = slice(c0, c0 + MXU_COLS)
        b = _dot(n, wb_ref[:, cols])
        u = _dot(n, wc_ref[:, cols]) * _dot(n, wv_ref[:, cols])
        ubuf[TAIL:TAIL + bm, cols] = u
        z_ref[:, cols] = (b * _conv3(wdw_ref[:, cols], u, ubuf.at[:, cols], TAIL)).astype(BF16)
    tail = ubuf[bm:bm + TAIL, :]
    carry[j] = tail
    utail_ref[...] = tail


def _mix_s_kernel(h_ref, g_ref, wb_ref, wc_ref, wv_ref, wdw_ref, hc_ref,
                  z_ref, uts_ref, umeta_ref, wb_o, wc_o, wv_o, n_sc, ubuf, mbuf):
    j = pl.program_id(0)
    ns = N_SAMPLE_ROWS
    nb = N_SAMPLE_SEQ
    nm = S_ROWS - ns

    @pl.when(j == 0)
    def _():
        n_sc[...] = _rms(h_ref[...], g_ref[...]).astype(BF16)

    wb_o[...] = wb_ref[...].astype(BF16)
    wc_o[...] = wc_ref[...].astype(BF16)
    wv_o[...] = wv_ref[...].astype(BF16)
    n = n_sc[...]
    nh = CONV_HIST * nb
    ubuf[0:nh, :] = hc_ref[...]
    mbuf[0:TAIL, :] = jnp.zeros((TAIL, mbuf.shape[1]), F32)
    for c0 in range(0, z_ref.shape[1], MXU_COLS):
        cols = slice(c0, c0 + MXU_COLS)
        b = _dot(n, wb_o[:, cols])
        u = _dot(n, wc_o[:, cols]) * _dot(n, wv_o[:, cols])
        w = wdw_ref[:, cols]

        us = u[0:ns, :]
        ubuf[nh:nh + ns, cols] = us
        conv_s = (w[2:3] * us + w[1:2] * ubuf[nb:nb + ns, cols] + w[0:1] * ubuf[0:ns, cols])
        z_ref[0:ns, cols] = (b[0:ns, :] * conv_s).astype(BF16)

        um = u[ns:, :]
        mbuf[TAIL:TAIL + nm, cols] = um
        z_ref[ns:, cols] = (b[ns:, :] * _conv3(w, um, mbuf.at[:, cols], TAIL)).astype(BF16)
    uts_ref[...] = ubuf[ns:ns + nh, :]
    umeta_ref[...] = mbuf[N_META:N_META + TAIL, :]


def _mix_conv_p(h, g, wb, wc, wv, w_dw, umeta, bm, bn):
    rows = h.shape[0]
    n_i, n_j = rows // bm, D_MODEL // bn
    wspec = pl.BlockSpec((D_MODEL, bn), lambda i, j: (0, j))
    return pl.pallas_call(
        functools.partial(_mix_p_kernel, SEQ // bm),
        grid=(n_i, n_j),
        in_specs=[
            pl.BlockSpec((bm, D_MODEL), lambda i, j: (i, 0)),
            pl.BlockSpec((1, D_MODEL), lambda i, j: (0, 0)),
            wspec, wspec, wspec,
            pl.BlockSpec((3, bn), lambda i, j: (0, j)),
            pl.BlockSpec((TAIL, bn), lambda i, j: (0, j)),
        ],
        out_specs=[
            pl.BlockSpec((bm, bn), lambda i, j: (i, j)),
            pl.BlockSpec((TAIL, bn), lambda i, j: (i, j)),
        ],
        out_shape=[
            jax.ShapeDtypeStruct((rows, D_MODEL), BF16),
            jax.ShapeDtypeStruct((n_i * TAIL, D_MODEL), F32),
        ],
        scratch_shapes=[
            pltpu.VMEM((bm, D_MODEL), BF16),
            pltpu.VMEM((TAIL + bm, bn), F32),
            pltpu.VMEM((n_j, TAIL, bn), F32),
        ],
        compiler_params=_params(2),
        name="mix_conv_p",
    )(h, g, wb, wc, wv, w_dw, umeta)


def _mix_conv_s(h, g, w_in, w_dw, hc, bn):
    n_j = D_MODEL // bn
    nm = S_ROWS - N_SAMPLE_ROWS
    wspec = lambda part: pl.BlockSpec((None, D_MODEL, bn), lambda j: (0, 0, part * n_j + j))
    wout = pl.BlockSpec((D_MODEL, bn), lambda j: (0, j))
    wshape = jax.ShapeDtypeStruct((D_MODEL, D_MODEL), BF16)
    return pl.pallas_call(
        _mix_s_kernel,
        grid=(n_j,),
        in_specs=[
            _resident((S_ROWS, D_MODEL), lambda j: (0, 0)),
            pl.BlockSpec((1, D_MODEL), lambda j: (0, 0)),
            wspec(0), wspec(1), wspec(2),
            pl.BlockSpec((3, bn), lambda j: (0, j)),
            pl.BlockSpec((CONV_HIST * N_SAMPLE_SEQ, bn), lambda j: (0, j)),
        ],
        out_specs=[
            pl.BlockSpec((S_ROWS, bn), lambda j: (0, j)),
            pl.BlockSpec((CONV_HIST * N_SAMPLE_SEQ, bn), lambda j: (0, j)),
            pl.BlockSpec((TAIL, bn), lambda j: (0, j)),
            wout, wout, wout,
        ],
        out_shape=[
            jax.ShapeDtypeStruct((S_ROWS, D_MODEL), BF16),
            jax.ShapeDtypeStruct((CONV_HIST * N_SAMPLE_SEQ, D_MODEL), F32),
            jax.ShapeDtypeStruct((TAIL, D_MODEL), F32),
            wshape, wshape, wshape,
        ],
        scratch_shapes=[
            pltpu.VMEM((S_ROWS, D_MODEL), BF16),
            pltpu.VMEM((CONV_HIST * N_SAMPLE_SEQ + N_SAMPLE_ROWS, bn), F32),
            pltpu.VMEM((TAIL + nm, bn), F32),
        ],
        compiler_params=_params(1),
        name="mix_conv_s",
    )(h, g, w_in, w_in, w_in, w_dw, hc)


def _proj_res_body(norm, k, n_k, x_ref, w_ref, h_ref, g_ref, o_ref, m_ref):
    if n_k == 1:
        acc = h_ref[...] + _dot(x_ref[...], w_ref[...])
        o_ref[...] = _rms(acc, g_ref[...]) if norm == "final" else acc
        if norm == "emit":
            m_ref[...] = _rms(acc, g_ref[...]).astype(BF16)
        return

    @pl.when(k == 0)
    def _():
        o_ref[...] = h_ref[...] + _dot(x_ref[...], w_ref[...])

    @pl.when(k != 0)
    def _():
        o_ref[...] += _dot(x_ref[...], w_ref[...])

    if norm is not None:
        @pl.when(k == n_k - 1)
        def _():
            y = _rms(o_ref[...], g_ref[...])
            if norm == "final":
                o_ref[...] = y
            else:
                m_ref[...] = y.astype(BF16)


def _proj_res_p_kernel(norm, n_k, x_ref, w_ref, h_ref, g_ref, o_ref, m_ref=None):
    _proj_res_body(norm, pl.program_id(1), n_k, x_ref, w_ref, h_ref, g_ref, o_ref, m_ref)


def _proj_res_s_kernel(norm, n_k, x_ref, w_ref, h_ref, g_ref, o_ref, wo_ref):
    wo_ref[...] = w_ref[...].astype(BF16)
    _proj_res_body(norm, pl.program_id(0), n_k, x_ref, wo_ref, h_ref, g_ref, o_ref, None)


def _proj_res_p(x, w, h, g, bm, bk, norm, name):
    rows, kdim = x.shape
    n_i, n_k = rows // bm, kdim // bk
    wspec = _resident if n_k == 1 else pl.BlockSpec
    row_spec = pl.BlockSpec((bm, D_MODEL), lambda i, k: (i, 0))
    out_specs, out_shape = [row_spec], [jax.ShapeDtypeStruct((rows, D_MODEL), F32)]
    if norm == "emit":
        out_specs.append(row_spec)
        out_shape.append(jax.ShapeDtypeStruct((rows, D_MODEL), BF16))
    return pl.pallas_call(
        functools.partial(_proj_res_p_kernel, norm, n_k),
        grid=(n_i, n_k),
        in_specs=[
            pl.BlockSpec((bm, bk), lambda i, k: (i, k)),
            wspec((bk, D_MODEL), lambda i, k: (k, 0)),
            row_spec,
            pl.BlockSpec((1, D_MODEL), lambda i, k: (0, 0)),
        ],
        out_specs=out_specs,
        out_shape=out_shape,
        compiler_params=_params(2),
        name=name,
    )(x, w, h, g)


def _proj_res_s(x, w_stack, layer, h, g, bk, norm, name):
    assert norm != "emit"
    rows, kdim = x.shape
    n_k = kdim // bk
    return pl.pallas_call(
        functools.partial(_proj_res_s_kernel, norm, n_k),
        grid=(n_k,),
        in_specs=[
            pl.BlockSpec((rows, bk), lambda k: (0, k)),
            pl.BlockSpec((None, bk, D_MODEL), lambda k: (layer, k, 0)),
            _resident((rows, D_MODEL), lambda k: (0, 0)),
            pl.BlockSpec((1, D_MODEL), lambda k: (0, 0)),
        ],
        out_specs=[
            pl.BlockSpec((rows, D_MODEL), lambda k: (0, 0)),
            pl.BlockSpec((bk, D_MODEL), lambda k: (k, 0)),
        ],
        out_shape=[
            jax.ShapeDtypeStruct((rows, D_MODEL), F32),
            jax.ShapeDtypeStruct((kdim, D_MODEL), BF16),
        ],
        compiler_params=_params(1),
        name=name,
    )(x, w_stack, h, g)


def _swiglu_chunks(m, wg_ref, wu_ref, a_ref, wg_src=None, wu_src=None):
    for c0 in range(0, a_ref.shape[1], MXU_COLS):
        cols = slice(c0, c0 + MXU_COLS)
        if wg_src is not None:
            wg_ref[:, cols] = wg_src[:, cols].astype(BF16)
            wu_ref[:, cols] = wu_src[:, cols].astype(BF16)
        gate = _dot(m, wg_ref[:, cols])
        up = _dot(m, wu_ref[:, cols])
        a_ref[:, cols] = (gate * (1.0 / (1.0 + jnp.exp(-gate))) * up).astype(BF16)


def _gate_up_p_kernel(m_ref, wg_ref, wu_ref, a_ref):
    _swiglu_chunks(m_ref[...], wg_ref, wu_ref, a_ref)


def _gate_up_s_kernel(h_ref, g_ref, wg_ref, wu_ref, a_ref, wg_o, wu_o, m_sc):
    @pl.when(pl.program_id(0) == 0)
    def _():
        m_sc[...] = _rms(h_ref[...], g_ref[...]).astype(BF16)

    _swiglu_chunks(m_sc[...], wg_o, wu_o, a_ref, wg_ref, wu_ref)


def _gate_up_p(m, wg, wu, bm, bn, name):
    rows = m.shape[0]
    n_i, n_j = rows // bm, D_FF // bn
    wspec = pl.BlockSpec((D_MODEL, bn), lambda i, j: (0, j))
    return pl.pallas_call(
        _gate_up_p_kernel,
        grid=(n_i, n_j),
        in_specs=[pl.BlockSpec((bm, D_MODEL), lambda i, j: (i, 0)), wspec, wspec],
        out_specs=pl.BlockSpec((bm, bn), lambda i, j: (i, j)),
        out_shape=jax.ShapeDtypeStruct((rows, D_FF), BF16),
        compiler_params=_params(2),
        name=name,
    )(m, wg, wu)


def _gate_up_s(h, g, w_gu, layer, bn, name):
    rows = h.shape[0]
    n_j = D_FF // bn
    wout = pl.BlockSpec((D_MODEL, bn), lambda j: (0, j))
    wshape = jax.ShapeDtypeStruct((D_MODEL, D_FF), BF16)
    return pl.pallas_call(
        _gate_up_s_kernel,
        grid=(n_j,),
        in_specs=[
            _resident((rows, D_MODEL), lambda j: (0, 0)),
            pl.BlockSpec((1, D_MODEL), lambda j: (0, 0)),
            pl.BlockSpec((None, D_MODEL, bn), lambda j: (layer, 0, j)),
            pl.BlockSpec((None, D_MODEL, bn), lambda j: (layer, 0, n_j + j)),
        ],
        out_specs=[pl.BlockSpec((rows, bn), lambda j: (0, j)), wout, wout],
        out_shape=[jax.ShapeDtypeStruct((rows, D_FF), BF16), wshape, wshape],
        scratch_shapes=[pltpu.VMEM((rows, D_MODEL), BF16)],
        compiler_params=_params(1),
        name=name,
    )(h, g, w_gu, w_gu)


def _pool_p_kernel(tiles_per_seq, h_ref, halo_p_ref, halo_s_ref, g_ref, wp_ref, sc_ref, gnext_ref,
                   o_ref, ntail_ref, m_ref, nbuf, pbuf, qbuf):
    i = pl.program_id(0)
    bm = h_ref.shape[0]
    g = g_ref[...]
    x = h_ref[...]
    n = _rms(x, g)
    rows = HALO + bm
    r0 = FRONT + HALO

    for buf in (nbuf, pbuf, qbuf):
        buf[0:FRONT, :] = jnp.zeros((FRONT, buf.shape[1]), F32)

    @pl.when(i % tiles_per_seq == 0)
    def _():
        nbuf[FRONT:r0, :] = _rms(halo_s_ref[...], g)

    @pl.when(i % tiles_per_seq != 0)
    def _():
        nbuf[FRONT:r0, :] = _rms(halo_p_ref[...], g)

    nbuf[r0:r0 + bm, :] = n
    ntail_ref[...] = n[bm - HALO:, :]
    for gi, win in enumerate(POOL_WINDOWS):
        cols = slice(gi * POOL_GROUP, (gi + 1) * POOL_GROUP)
        src, shift, level = nbuf, 1, 0
        while shift < win:
            dst = (pbuf, qbuf)[level % 2]
            dst[FRONT:FRONT + rows, cols] = (src[FRONT:FRONT + rows, cols]
                                             + src[FRONT - shift:FRONT - shift + rows, cols])
            src, shift, level = dst, 2 * shift, level + 1
        ng = n[:, cols]
        p = src[r0:r0 + bm, cols] * (1.0 / win) - ng
        y = _dot(p.astype(BF16), wp_ref[gi].astype(BF16)) * sc_ref[:, cols]
        o_ref[:, cols] = x[:, cols] + y
    m_ref[...] = _rms(o_ref[...], gnext_ref[...]).astype(BF16)


def _pool_p(h_p, h_s, g, wp, scale, g_next, bm):
    rows = h_p.shape[0]
    n_i = rows // bm
    ng = len(POOL_WINDOWS)
    return pl.pallas_call(
        functools.partial(_pool_p_kernel, SEQ // bm),
        grid=(n_i,),
        in_specs=[
            pl.BlockSpec((bm, D_MODEL), lambda i: (i, 0)),
            pl.BlockSpec((HALO, D_MODEL), lambda i: (jnp.maximum(i * (bm // HALO) - 1, 0), 0)),
            pl.BlockSpec((HALO, D_MODEL), lambda i: (META_ROW0 // HALO, 0)),
            pl.BlockSpec((1, D_MODEL), lambda i: (0, 0)),
            pl.BlockSpec((None, ng, POOL_GROUP, POOL_GROUP), lambda i: (0, 0, 0, 0)),
            pl.BlockSpec((1, D_MODEL), lambda i: (0, 0)),
            pl.BlockSpec((1, D_MODEL), lambda i: (0, 0)),
        ],
        out_specs=[
            pl.BlockSpec((bm, D_MODEL), lambda i: (i, 0)),
            pl.BlockSpec((HALO, D_MODEL), lambda i: (i, 0)),
            pl.BlockSpec((bm, D_MODEL), lambda i: (i, 0)),
        ],
        out_shape=[
            jax.ShapeDtypeStruct((rows, D_MODEL), F32),
            jax.ShapeDtypeStruct((n_i * HALO, D_MODEL), F32),
            jax.ShapeDtypeStruct((rows, D_MODEL), BF16),
        ],
        scratch_shapes=[pltpu.VMEM((FRONT + HALO + bm, D_MODEL), F32)] * 3,
        compiler_params=_params(1),
        name="pool_p",
    )(h_p, h_p, h_s, g, wp, scale, g_next)


def _pool_s_kernel(hfull_ref, hcol_ref, hp_ref, g_ref, wp_ref, sc_ref,
                   o_ref, n_ref, inv_sc, nbuf, sum_sc):
    j = pl.program_id(0)
    ns = N_SAMPLE_ROWS
    nb = N_SAMPLE_SEQ
    nh = (POOL_HIST + 1) * nb

    @pl.when(j == 0)
    def _():
        xf = hfull_ref[...]
        inv_sc[...] = lax.rsqrt(jnp.mean(xf * xf, axis=-1, keepdims=True) + EPS)

    x = hcol_ref[...]
    n = ((x * inv_sc[...]) * g_ref[...])[0:ns, :]
    n_ref[...] = n
    nbuf[0:nh, :] = hp_ref[...]
    nbuf[nh:nh + ns, :] = n

    for gi, win in enumerate(POOL_WINDOWS):
        @pl.when(j == gi)
        def _(win=win):
            acc = n
            for k in range(1, win):
                acc = acc + nbuf[nh - k * nb:nh - k * nb + ns, :]
            sum_sc[...] = acc * (1.0 / win)

    p = sum_sc[...] - n
    y = _dot(p.astype(BF16), wp_ref[...].astype(BF16)) * sc_ref[...]
    o_ref[0:ns, :] = x[0:ns, :] + y
    o_ref[ns:, :] = x[ns:, :]


def _pool_s(h_s, hp, g, wp, scale):
    ns = N_SAMPLE_ROWS
    nh = (POOL_HIST + 1) * N_SAMPLE_SEQ
    pg = POOL_GROUP
    return pl.pallas_call(
        _pool_s_kernel,
        grid=(len(POOL_WINDOWS),),
        in_specs=[
            _resident((S_ROWS, D_MODEL), lambda j: (0, 0)),
            pl.BlockSpec((S_ROWS, pg), lambda j: (0, j)),
            pl.BlockSpec((nh, pg), lambda j: (0, j)),
            pl.BlockSpec((1, pg), lambda j: (0, j)),
            pl.BlockSpec((None, None, pg, pg), lambda j: (0, j, 0, 0)),
            pl.BlockSpec((1, pg), lambda j: (0, j)),
        ],
        out_specs=[
            pl.BlockSpec((S_ROWS, pg), lambda j: (0, j)),
            pl.BlockSpec((ns, pg), lambda j: (0, j)),
        ],
        out_shape=[
            jax.ShapeDtypeStruct((S_ROWS, D_MODEL), F32),
            jax.ShapeDtypeStruct((ns, D_MODEL), F32),
        ],
        scratch_shapes=[
            pltpu.VMEM((S_ROWS, 1), F32),
            pltpu.VMEM((nh + ns, pg), F32),
            pltpu.VMEM((ns, pg), F32),
        ],
        compiler_params=_params(1),
        name="pool_s",
    )(h_s, h_s, hp, g, wp, scale)


BM_P = 1024
BM_GATE_UP = 2048
BM_POOL = 512
BM_CONV_OUT = 512
BN_P = 512
BM_DOWN = 512
BK_DOWN = D_FF // 2
BN_S = 256
BN_S_FF = 512
BK_S = 512


def kernel(x_prompt, x_sample, state_conv, state_pool, meta_tokens, norm_mix, norm_ffn, norm_final,
           conv_w_in, conv_w_dw, conv_w_out, pool_w, pool_scale, ffn_w_gate_up, ffn_w_down):
    d = D_MODEL
    nb, nt = N_SAMPLE_SEQ, SAMPLE_T
    batch = x_prompt.shape[0]

    h_p = x_prompt.reshape(batch * SEQ, d)
    h_s = jnp.concatenate([
        x_sample.transpose(1, 0, 2).reshape(nb * nt, d),
        meta_tokens.astype(F32),
        jnp.zeros((S_ROWS - nb * nt - N_META, d), F32),
    ], axis=0)
    hc = state_conv[0].transpose(1, 0, 2).reshape(CONV_HIST * nb, d)
    hp = jnp.concatenate([jnp.zeros((1, nb, d), F32), state_pool[0].transpose(1, 0, 2)],
                         axis=0).reshape((POOL_HIST + 1) * nb, d)
    row = lambda v: v.reshape(1, d)
    g_mix, g_ffn, g_fin = norm_mix, norm_ffn, row(norm_final)

    z_s, uts, umeta, wb, wc, wv = _mix_conv_s(h_s, row(g_mix[0]), conv_w_in, conv_w_dw[0], hc, BN_S)
    z_p, utail_p = _mix_conv_p(h_p, row(g_mix[0]), wb, wc, wv, conv_w_dw[0], umeta, BM_P, BN_P)
    h_s, w_out = _proj_res_s(z_s, conv_w_out, 0, h_s, g_fin, BK_S, None, "conv_out_s")
    h_p, m_p = _proj_res_p(z_p, w_out, h_p, row(g_ffn[0]), BM_CONV_OUT, D_MODEL, "emit", "conv_out_p")
    a_s, wg, wu = _gate_up_s(h_s, row(g_ffn[0]), ffn_w_gate_up, 0, BN_S_FF, "gate_up0_s")
    a_p = _gate_up_p(m_p, wg, wu, BM_GATE_UP, BN_P, "gate_up0_p")
    h_s, w_dn = _proj_res_s(a_s, ffn_w_down, 0, h_s, g_fin, BK_S, None, "down0_s")
    (h_p,) = _proj_res_p(a_p, w_dn, h_p, g_fin, BM_DOWN, BK_DOWN, None, "down0_p")
    h_p1, ntail_p, m_p = _pool_p(h_p, h_s, row(g_mix[1]), pool_w, row(pool_scale[0]), row(g_ffn[1]), BM_POOL)
    h_s1, n_s = _pool_s(h_s, hp, row(g_mix[1]), pool_w, row(pool_scale[0]))
    a_s, wg, wu = _gate_up_s(h_s1, row(g_ffn[1]), ffn_w_gate_up, 1, BN_S_FF, "gate_up1_s")
    a_p = _gate_up_p(m_p, wg, wu, BM_GATE_UP, BN_P, "gate_up1_p")
    y_s, w_dn = _proj_res_s(a_s, ffn_w_down, 1, h_s1, g_fin, BK_S, "final", "down1_s")
    (y_p,) = _proj_res_p(a_p, w_dn, h_p1, g_fin, BM_DOWN, BK_DOWN, "final", "down1_p")

    y_prompt = y_p.reshape(batch, SEQ, d)
    y_sample = y_s[:nb * nt].reshape(nt, nb, d).transpose(1, 0, 2)
    tps = SEQ // BM_P
    new_conv_prompt = utail_p.reshape(batch, tps, TAIL, d)[:, tps - 1, TAIL - CONV_HIST:, :][None]
    tpp = SEQ // BM_POOL
    new_pool_prompt = ntail_p.reshape(batch, tpp, HALO, d)[:, tpp - 1, HALO - POOL_HIST:, :][None]
    new_conv_sample = uts.reshape(CONV_HIST, nb, d).transpose(1, 0, 2)[None]
    new_pool_sample = jnp.concatenate(
        [state_pool[0][:, nt:, :], n_s.reshape(nt, nb, d).transpose(1, 0, 2)], axis=1)[None]
    return (y_prompt, y_sample, new_conv_prompt, new_pool_prompt, new_conv_sample, new_pool_sample)
```

```python
import functools

import jax
import jax.numpy as jnp
from jax import lax
from jax.experimental import pallas as pl
from jax.experimental.pallas import tpu as pltpu

D_MODEL = 2048
D_FF = 5632
N_META = 16
SEQ = 2048
N_SAMPLE_SEQ = 128
SAMPLE_T = 8
CONV_HIST = 2
POOL_WINDOWS = (2, 4, 8, 16)
POOL_GROUP = D_MODEL // len(POOL_WINDOWS)
POOL_HIST = 15
EPS = 1e-6

N_SAMPLE_ROWS = N_SAMPLE_SEQ * SAMPLE_T
S_ROWS = N_SAMPLE_ROWS + 2 * N_META
META_ROW0 = N_SAMPLE_ROWS
TAIL = 8
HALO = 16
MXU_COLS = 256
FRONT = 8

V7X_SCOPED_VMEM_BYTES = 60000 * 1024

BF16 = jnp.bfloat16
F32 = jnp.float32


def _rms(x, g):
    ms = jnp.mean(x * x, axis=-1, keepdims=True)
    return (x * lax.rsqrt(ms + EPS)) * g


def _dot(a, b):
    return jnp.dot(a, b, preferred_element_type=F32)


def _params(n_axes):
    return pltpu.CompilerParams(
        dimension_semantics=("arbitrary",) * n_axes,
        vmem_limit_bytes=V7X_SCOPED_VMEM_BYTES,
    )


def _resident(shape, index_map):
    return pl.BlockSpec(shape, index_map, pipeline_mode=pl.Buffered(1))


def _conv3(w, u, ubuf, off):
    t = u.shape[0]
    return (w[2:3] * u + w[1:2] * ubuf[off - 1:off - 1 + t, :] + w[0:1] * ubuf[off - 2:off - 2 + t, :])


def _mix_p_kernel(tiles_per_seq, h_ref, g_ref, wb_ref, wc_ref, wv_ref, wdw_ref, umeta_ref,
                  z_ref, utail_ref, n_sc, ubuf, carry):
    i = pl.program_id(0)
    j = pl.program_id(1)
    bm = h_ref.shape[0]

    @pl.when(j == 0)
    def _():
        n_sc[...] = _rms(h_ref[...], g_ref[...]).astype(BF16)

    @pl.when(i % tiles_per_seq == 0)
    def _():
        ubuf[0:TAIL, :] = umeta_ref[...]

    @pl.when(i % tiles_per_seq != 0)
    def _():
        ubuf[0:TAIL, :] = carry[j]

    n = n_sc[...]
    for c0 in range(0, z_ref.shape[1], MXU_COLS):
        cols = slice(c0, c0 + MXU_COLS)
        b = _dot(n, wb_ref[:, cols])
        u = _dot(n, wc_ref[:, cols]) * _dot(n, wv_ref[:, cols])
        ubuf[TAIL:TAIL + bm, cols] = u
        z_ref[:, cols] = (b * _conv3(wdw_ref[:, cols], u, ubuf.at[:, cols], TAIL)).astype(BF16)
    tail = ubuf[bm:bm + TAIL, :]
    carry[j] = tail
    utail_ref[...] = tail


def _mix_s_kernel(xs_ref, meta_ref, g_ref, wb_ref, wc_ref, wv_ref, wdw_ref, hc_ref,
                  z_ref, uts_ref, umeta_ref, wb_o, wc_o, wv_o, h0_ref, n_sc, ubuf, mbuf):
    j = pl.program_id(0)
    ns = N_SAMPLE_ROWS
    nb = N_SAMPLE_SEQ
    nm = S_ROWS - ns

    @pl.when(j == 0)
    def _():
        for t in range(SAMPLE_T):
            h0_ref[t * nb:(t + 1) * nb, :] = xs_ref[:, t, :]
        h0_ref[ns:ns + N_META, :] = meta_ref[...]
        h0_ref[ns + N_META:, :] = jnp.zeros((nm - N_META, h0_ref.shape[1]), F32)
        n_sc[...] = _rms(h0_ref[...], g_ref[...]).astype(BF16)

    wb_o[...] = wb_ref[...].astype(BF16)
    wc_o[...] = wc_ref[...].astype(BF16)
    wv_o[...] = wv_ref[...].astype(BF16)
    n = n_sc[...]
    nh = CONV_HIST * nb
    for t in range(CONV_HIST):
        ubuf[t * nb:(t + 1) * nb, :] = hc_ref[:, t, :]
    mbuf[0:TAIL, :] = jnp.zeros((TAIL, mbuf.shape[1]), F32)
    for c0 in range(0, z_ref.shape[1], MXU_COLS):
        cols = slice(c0, c0 + MXU_COLS)
        b = _dot(n, wb_o[:, cols])
        u = _dot(n, wc_o[:, cols]) * _dot(n, wv_o[:, cols])
        w = wdw_ref[:, cols]

        us = u[0:ns, :]
        ubuf[nh:nh + ns, cols] = us
        conv_s = (w[2:3] * us + w[1:2] * ubuf[nb:nb + ns, cols] + w[0:1] * ubuf[0:ns, cols])
        z_ref[0:ns, cols] = (b[0:ns, :] * conv_s).astype(BF16)

        um = u[ns:, :]
        mbuf[TAIL:TAIL + nm, cols] = um
        z_ref[ns:, cols] = (b[ns:, :] * _conv3(w, um, mbuf.at[:, cols], TAIL)).astype(BF16)
    for t in range(CONV_HIST):
        uts_ref[:, t, :] = ubuf[ns + t * nb:ns + (t + 1) * nb, :]
    umeta_ref[...] = mbuf[N_META:N_META + TAIL, :]


def _mix_conv_p(h, g, wb, wc, wv, w_dw, umeta, bm, bn):
    rows = h.shape[0]
    n_i, n_j = rows // bm, D_MODEL // bn
    wspec = pl.BlockSpec((D_MODEL, bn), lambda i, j: (0, j))
    return pl.pallas_call(
        functools.partial(_mix_p_kernel, SEQ // bm),
        grid=(n_i, n_j),
        in_specs=[
            pl.BlockSpec((bm, D_MODEL), lambda i, j: (i, 0)),
            pl.BlockSpec((1, D_MODEL), lambda i, j: (0, 0)),
            wspec, wspec, wspec,
            pl.BlockSpec((3, bn), lambda i, j: (0, j)),
            pl.BlockSpec((TAIL, bn), lambda i, j: (0, j)),
        ],
        out_specs=[
            pl.BlockSpec((bm, bn), lambda i, j: (i, j)),
            pl.BlockSpec((TAIL, bn), lambda i, j: (i, j)),
        ],
        out_shape=[
            jax.ShapeDtypeStruct((rows, D_MODEL), BF16),
            jax.ShapeDtypeStruct((n_i * TAIL, D_MODEL), F32),
        ],
        scratch_shapes=[
            pltpu.VMEM((bm, D_MODEL), BF16),
            pltpu.VMEM((TAIL + bm, bn), F32),
            pltpu.VMEM((n_j, TAIL, bn), F32),
        ],
        compiler_params=_params(2),
        name="mix_conv_p",
    )(h, g, wb, wc, wv, w_dw, umeta)


def _mix_conv_s(x_sample, meta, g, w_in, w_dw, state_conv, bn):
    n_j = D_MODEL // bn
    nb = N_SAMPLE_SEQ
    nm = S_ROWS - N_SAMPLE_ROWS
    wspec = lambda part: pl.BlockSpec((None, D_MODEL, bn), lambda j: (0, 0, part * n_j + j))
    wout = pl.BlockSpec((D_MODEL, bn), lambda j: (0, j))
    wshape = jax.ShapeDtypeStruct((D_MODEL, D_MODEL), BF16)
    return pl.pallas_call(
        _mix_s_kernel,
        grid=(n_j,),
        in_specs=[
            _resident((nb, SAMPLE_T, D_MODEL), lambda j: (0, 0, 0)),
            _resident((N_META, D_MODEL), lambda j: (0, 0)),
            pl.BlockSpec((1, D_MODEL), lambda j: (0, 0)),
            wspec(0), wspec(1), wspec(2),
            pl.BlockSpec((3, bn), lambda j: (0, j)),
            pl.BlockSpec((nb, CONV_HIST, bn), lambda j: (0, 0, j)),
        ],
        out_specs=[
            pl.BlockSpec((S_ROWS, bn), lambda j: (0, j)),
            pl.BlockSpec((nb, CONV_HIST, bn), lambda j: (0, 0, j)),
            pl.BlockSpec((TAIL, bn), lambda j: (0, j)),
            wout, wout, wout,
            pl.BlockSpec((S_ROWS, D_MODEL), lambda j: (0, 0)),
        ],
        out_shape=[
            jax.ShapeDtypeStruct((S_ROWS, D_MODEL), BF16),
            jax.ShapeDtypeStruct((nb, CONV_HIST, D_MODEL), F32),
            jax.ShapeDtypeStruct((TAIL, D_MODEL), F32),
            wshape, wshape, wshape,
            jax.ShapeDtypeStruct((S_ROWS, D_MODEL), F32),
        ],
        scratch_shapes=[
            pltpu.VMEM((S_ROWS, D_MODEL), BF16),
            pltpu.VMEM((CONV_HIST * nb + N_SAMPLE_ROWS, bn), F32),
            pltpu.VMEM((TAIL + nm, bn), F32),
        ],
        compiler_params=_params(1),
        name="mix_conv_s",
    )(x_sample, meta, g, w_in, w_in, w_in, w_dw, state_conv)


def _proj_res_body(norm, k, n_k, x_ref, w_ref, h_ref, g_ref, o_ref, m_ref):
    if n_k == 1:
        acc = h_ref[...] + _dot(x_ref[...], w_ref[...])
        o_ref[...] = _rms(acc, g_ref[...]) if norm == "final" else acc
        if norm == "emit":
            m_ref[...] = _rms(acc, g_ref[...]).astype(BF16)
        return

    @pl.when(k == 0)
    def _():
        o_ref[...] = h_ref[...] + _dot(x_ref[...], w_ref[...])

    @pl.when(k != 0)
    def _():
        o_ref[...] += _dot(x_ref[...], w_ref[...])

    if norm is not None:
        @pl.when(k == n_k - 1)
        def _():
            y = _rms(o_ref[...], g_ref[...])
            if norm == "final":
                o_ref[...] = y
            else:
                m_ref[...] = y.astype(BF16)


def _proj_res_p_kernel(norm, n_k, x_ref, w_ref, h_ref, g_ref, o_ref, m_ref=None):
    _proj_res_body(norm, pl.program_id(1), n_k, x_ref, w_ref, h_ref, g_ref, o_ref, m_ref)


def _proj_res_s_kernel(norm, n_k, x_ref, w_ref, h_ref, g_ref, o_ref, wo_ref, acc=None):
    k = pl.program_id(0)
    wo_ref[...] = w_ref[...].astype(BF16)
    if norm != "final_sample":
        _proj_res_body(norm, k, n_k, x_ref, wo_ref, h_ref, g_ref, o_ref, None)
        return
    _proj_res_body(None, k, n_k, x_ref, wo_ref, h_ref, g_ref, acc, None)

    @pl.when(k == n_k - 1)
    def _():
        y = _rms(acc[0:N_SAMPLE_ROWS, :], g_ref[...])
        for t in range(SAMPLE_T):
            o_ref[:, t, :] = y[t * N_SAMPLE_SEQ:(t + 1) * N_SAMPLE_SEQ, :]


def _proj_res_p(x, w, h, g, bm, bk, norm, name):
    rows, kdim = x.shape
    n_i, n_k = rows // bm, kdim // bk
    wspec = _resident if n_k == 1 else pl.BlockSpec
    row_spec = pl.BlockSpec((bm, D_MODEL), lambda i, k: (i, 0))
    out_specs, out_shape = [row_spec], [jax.ShapeDtypeStruct((rows, D_MODEL), F32)]
    if norm == "emit":
        out_specs.append(row_spec)
        out_shape.append(jax.ShapeDtypeStruct((rows, D_MODEL), BF16))
    return pl.pallas_call(
        functools.partial(_proj_res_p_kernel, norm, n_k),
        grid=(n_i, n_k),
        in_specs=[
            pl.BlockSpec((bm, bk), lambda i, k: (i, k)),
            wspec((bk, D_MODEL), lambda i, k: (k, 0)),
            row_spec,
            pl.BlockSpec((1, D_MODEL), lambda i, k: (0, 0)),
        ],
        out_specs=out_specs,
        out_shape=out_shape,
        compiler_params=_params(2),
        name=name,
    )(x, w, h, g)


def _proj_res_s(x, w_stack, layer, h, g, bk, norm, name):
    assert norm in (None, "final_sample")
    rows, kdim = x.shape
    n_k = kdim // bk
    if norm == "final_sample":
        o_shape = (N_SAMPLE_SEQ, SAMPLE_T, D_MODEL)
        scratch = [pltpu.VMEM((rows, D_MODEL), F32)]
    else:
        o_shape = (rows, D_MODEL)
        scratch = []
    o_index = (0,) * len(o_shape)
    return pl.pallas_call(
        functools.partial(_proj_res_s_kernel, norm, n_k),
        grid=(n_k,),
        in_specs=[
            pl.BlockSpec((rows, bk), lambda k: (0, k)),
            pl.BlockSpec((None, bk, D_MODEL), lambda k: (layer, k, 0)),
            _resident((rows, D_MODEL), lambda k: (0, 0)),
            pl.BlockSpec((1, D_MODEL), lambda k: (0, 0)),
        ],
        out_specs=[
            pl.BlockSpec(o_shape, lambda k: o_index),
            pl.BlockSpec((bk, D_MODEL), lambda k: (k, 0)),
        ],
        out_shape=[
            jax.ShapeDtypeStruct(o_shape, F32),
            jax.ShapeDtypeStruct((kdim, D_MODEL), BF16),
        ],
        scratch_shapes=scratch,
        compiler_params=_params(1),
        name=name,
    )(x, w_stack, h, g)


def _swiglu_chunks(m, wg_ref, wu_ref, a_ref, wg_src=None, wu_src=None):
    for c0 in range(0, a_ref.shape[1], MXU_COLS):
        cols = slice(c0, c0 + MXU_COLS)
        if wg_src is not None:
            wg_ref[:, cols] = wg_src[:, cols].astype(BF16)
            wu_ref[:, cols] = wu_src[:, cols].astype(BF16)
        gate = _dot(m, wg_ref[:, cols])
        up = _dot(m, wu_ref[:, cols])
        a_ref[:, cols] = (gate * (1.0 / (1.0 + jnp.exp(-gate))) * up).astype(BF16)


def _gate_up_p_kernel(m_ref, wg_ref, wu_ref, a_ref):
    _swiglu_chunks(m_ref[...], wg_ref, wu_ref, a_ref)


def _gate_up_s_kernel(h_ref, g_ref, wg_ref, wu_ref, a_ref, wg_o, wu_o, m_sc):
    @pl.when(pl.program_id(0) == 0)
    def _():
        m_sc[...] = _rms(h_ref[...], g_ref[...]).astype(BF16)

    _swiglu_chunks(m_sc[...], wg_o, wu_o, a_ref, wg_ref, wu_ref)


def _gate_up_p(m, wg, wu, bm, bn, name):
    rows = m.shape[0]
    n_i, n_j = rows // bm, D_FF // bn
    wspec = pl.BlockSpec((D_MODEL, bn), lambda i, j: (0, j))
    return pl.pallas_call(
        _gate_up_p_kernel,
        grid=(n_i, n_j),
        in_specs=[pl.BlockSpec((bm, D_MODEL), lambda i, j: (i, 0)), wspec, wspec],
        out_specs=pl.BlockSpec((bm, bn), lambda i, j: (i, j)),
        out_shape=jax.ShapeDtypeStruct((rows, D_FF), BF16),
        compiler_params=_params(2),
        name=name,
    )(m, wg, wu)


def _gate_up_s(h, g, w_gu, layer, bn, name):
    rows = h.shape[0]
    n_j = D_FF // bn
    wout = pl.BlockSpec((D_MODEL, bn), lambda j: (0, j))
    wshape = jax.ShapeDtypeStruct((D_MODEL, D_FF), BF16)
    return pl.pallas_call(
        _gate_up_s_kernel,
        grid=(n_j,),
        in_specs=[
            _resident((rows, D_MODEL), lambda j: (0, 0)),
            pl.BlockSpec((1, D_MODEL), lambda j: (0, 0)),
            pl.BlockSpec((None, D_MODEL, bn), lambda j: (layer, 0, j)),
            pl.BlockSpec((None, D_MODEL, bn), lambda j: (layer, 0, n_j + j)),
        ],
        out_specs=[pl.BlockSpec((rows, bn), lambda j: (0, j)), wout, wout],
        out_shape=[jax.ShapeDtypeStruct((rows, D_FF), BF16), wshape, wshape],
        scratch_shapes=[pltpu.VMEM((rows, D_MODEL), BF16)],
        compiler_params=_params(1),
        name=name,
    )(h, g, w_gu, w_gu)


def _pool_p_kernel(tiles_per_seq, h_ref, halo_p_ref, halo_s_ref, g_ref, wp_ref, sc_ref, gnext_ref,
                   o_ref, ntail_ref, m_ref, nbuf, pbuf, qbuf):
    i = pl.program_id(0)
    bm = h_ref.shape[0]
    g = g_ref[...]
    x = h_ref[...]
    n = _rms(x, g)
    rows = HALO + bm
    r0 = FRONT + HALO

    for buf in (nbuf, pbuf, qbuf):
        buf[0:FRONT, :] = jnp.zeros((FRONT, buf.shape[1]), F32)

    @pl.when(i % tiles_per_seq == 0)
    def _():
        nbuf[FRONT:r0, :] = _rms(halo_s_ref[...], g)

    @pl.when(i % tiles_per_seq != 0)
    def _():
        nbuf[FRONT:r0, :] = _rms(halo_p_ref[...], g)

    nbuf[r0:r0 + bm, :] = n
    ntail_ref[...] = n[bm - HALO:, :]
    for gi, win in enumerate(POOL_WINDOWS):
        cols = slice(gi * POOL_GROUP, (gi + 1) * POOL_GROUP)
        src, shift, level = nbuf, 1, 0
        while shift < win:
            dst = (pbuf, qbuf)[level % 2]
            dst[FRONT:FRONT + rows, cols] = (src[FRONT:FRONT + rows, cols]
                                             + src[FRONT - shift:FRONT - shift + rows, cols])
            src, shift, level = dst, 2 * shift, level + 1
        ng = n[:, cols]
        p = src[r0:r0 + bm, cols] * (1.0 / win) - ng
        y = _dot(p.astype(BF16), wp_ref[gi].astype(BF16)) * sc_ref[:, cols]
        o_ref[:, cols] = x[:, cols] + y
    m_ref[...] = _rms(o_ref[...], gnext_ref[...]).astype(BF16)


def _pool_p(h_p, h_s, g, wp, scale, g_next, bm):
    rows = h_p.shape[0]
    n_i = rows // bm
    ng = len(POOL_WINDOWS)
    return pl.pallas_call(
        functools.partial(_pool_p_kernel, SEQ // bm),
        grid=(n_i,),
        in_specs=[
            pl.BlockSpec((bm, D_MODEL), lambda i: (i, 0)),
            pl.BlockSpec((HALO, D_MODEL), lambda i: (jnp.maximum(i * (bm // HALO) - 1, 0), 0)),
            pl.BlockSpec((HALO, D_MODEL), lambda i: (META_ROW0 // HALO, 0)),
            pl.BlockSpec((1, D_MODEL), lambda i: (0, 0)),
            pl.BlockSpec((None, ng, POOL_GROUP, POOL_GROUP), lambda i: (0, 0, 0, 0)),
            pl.BlockSpec((1, D_MODEL), lambda i: (0, 0)),
            pl.BlockSpec((1, D_MODEL), lambda i: (0, 0)),
        ],
        out_specs=[
            pl.BlockSpec((bm, D_MODEL), lambda i: (i, 0)),
            pl.BlockSpec((HALO, D_MODEL), lambda i: (i, 0)),
            pl.BlockSpec((bm, D_MODEL), lambda i: (i, 0)),
        ],
        out_shape=[
            jax.ShapeDtypeStruct((rows, D_MODEL), F32),
            jax.ShapeDtypeStruct((n_i * HALO, D_MODEL), F32),
            jax.ShapeDtypeStruct((rows, D_MODEL), BF16),
        ],
        scratch_shapes=[pltpu.VMEM((FRONT + HALO + bm, D_MODEL), F32)] * 3,
        compiler_params=_params(1),
        name="pool_p",
    )(h_p, h_p, h_s, g, wp, scale, g_next)


def _pool_s_kernel(hfull_ref, hcol_ref, hp_ref, g_ref, wp_ref, sc_ref, o_ref, tail_ref, inv_sc, nbuf, sum_sc):
    j = pl.program_id(0)
    ns = N_SAMPLE_ROWS
    nb = N_SAMPLE_SEQ
    nh = POOL_HIST * nb

    @pl.when(j == 0)
    def _():
        xf = hfull_ref[...]
        inv_sc[...] = lax.rsqrt(jnp.mean(xf * xf, axis=-1, keepdims=True) + EPS)

    x = hcol_ref[...]
    n = ((x * inv_sc[...]) * g_ref[...])[0:ns, :]
    for t in range(POOL_HIST):
        nbuf[t * nb:(t + 1) * nb, :] = hp_ref[:, t, :]
    nbuf[nh:nh + ns, :] = n
    for t in range(POOL_HIST):
        tail_ref[:, t, :] = nbuf[ns + t * nb:ns + (t + 1) * nb, :]

    for gi, win in enumerate(POOL_WINDOWS):
        @pl.when(j == gi)
        def _(win=win):
            acc = n
            for k in range(1, win):
                acc = acc + nbuf[nh - k * nb:nh - k * nb + ns, :]
            sum_sc[...] = acc * (1.0 / win)

    p = sum_sc[...] - n
    y = _dot(p.astype(BF16), wp_ref[...].astype(BF16)) * sc_ref[...]
    o_ref[0:ns, :] = x[0:ns, :] + y
    o_ref[ns:, :] = x[ns:, :]


def _pool_s(h_s, state_pool, g, wp, scale):
    ns = N_SAMPLE_ROWS
    nb = N_SAMPLE_SEQ
    nh = POOL_HIST * nb
    pg = POOL_GROUP
    n_g = len(POOL_WINDOWS)
    state_spec = pl.BlockSpec((nb, POOL_HIST, pg), lambda j: (0, 0, j))
    return pl.pallas_call(
        _pool_s_kernel,
        grid=(n_g,),
        in_specs=[
            _resident((S_ROWS, D_MODEL), lambda j: (0, 0)),
            pl.BlockSpec((S_ROWS, pg), lambda j: (0, j)),
            state_spec,
            pl.BlockSpec((1, pg), lambda j: (0, j)),
            pl.BlockSpec((None, None, pg, pg), lambda j: (0, j, 0, 0)),
            pl.BlockSpec((1, pg), lambda j: (0, j)),
        ],
        out_specs=[pl.BlockSpec((S_ROWS, pg), lambda j: (0, j)), state_spec],
        out_shape=[
            jax.ShapeDtypeStruct((S_ROWS, D_MODEL), F32),
            jax.ShapeDtypeStruct((nb, POOL_HIST, D_MODEL), F32),
        ],
        scratch_shapes=[
            pltpu.VMEM((S_ROWS, 1), F32),
            pltpu.VMEM((nh + ns, pg), F32),
            pltpu.VMEM((ns, pg), F32),
        ],
        compiler_params=_params(1),
        name="pool_s",
    )(h_s, h_s, state_pool, g, wp, scale)


BM_P = 1024
BM_GATE_UP = 2048
BM_POOL = 512
BM_CONV_OUT = 512
BN_P = 512
BM_DOWN = 512
BK_DOWN = D_FF
BN_S = 256
BN_S_FF = 512
BK_S = 512


def kernel(x_prompt, x_sample, state_conv, state_pool, meta_tokens, norm_mix, norm_ffn, norm_final,
           conv_w_in, conv_w_dw, conv_w_out, pool_w, pool_scale, ffn_w_gate_up, ffn_w_down):
    d = D_MODEL
    nb, nt = N_SAMPLE_SEQ, SAMPLE_T
    batch = x_prompt.shape[0]

    h_p = x_prompt.reshape(batch * SEQ, d)
    row = lambda v: v.reshape(1, d)
    g_mix, g_ffn, g_fin = norm_mix, norm_ffn, row(norm_final)

    z_s, uts, umeta, wb, wc, wv, h_s = _mix_conv_s(
        x_sample, meta_tokens, row(g_mix[0]), conv_w_in, conv_w_dw[0], state_conv[0], BN_S)
    z_p, utail_p = _mix_conv_p(h_p, row(g_mix[0]), wb, wc, wv, conv_w_dw[0], umeta, BM_P, BN_P)
    h_s, w_out = _proj_res_s(z_s, conv_w_out, 0, h_s, g_fin, BK_S, None, "conv_out_s")
    h_p, m_p = _proj_res_p(z_p, w_out, h_p, row(g_ffn[0]), BM_CONV_OUT, D_MODEL, "emit", "conv_out_p")
    a_s, wg, wu = _gate_up_s(h_s, row(g_ffn[0]), ffn_w_gate_up, 0, BN_S_FF, "gate_up0_s")
    a_p = _gate_up_p(m_p, wg, wu, BM_GATE_UP, BN_P, "gate_up0_p")
    h_s, w_dn = _proj_res_s(a_s, ffn_w_down, 0, h_s, g_fin, BK_S, None, "down0_s")
    (h_p,) = _proj_res_p(a_p, w_dn, h_p, g_fin, BM_DOWN, BK_DOWN, None, "down0_p")
    h_p1, ntail_p, m_p = _pool_p(h_p, h_s, row(g_mix[1]), pool_w, row(pool_scale[0]), row(g_ffn[1]), BM_POOL)
    h_s1, pool_tail_s = _pool_s(h_s, state_pool[0], row(g_mix[1]), pool_w, row(pool_scale[0]))
    a_s, wg, wu = _gate_up_s(h_s1, row(g_ffn[1]), ffn_w_gate_up, 1, BN_S_FF, "gate_up1_s")
    a_p = _gate_up_p(m_p, wg, wu, BM_GATE_UP, BN_P, "gate_up1_p")
    y_s, w_dn = _proj_res_s(a_s, ffn_w_down, 1, h_s1, g_fin, BK_S, "final_sample", "down1_s")
    (y_p,) = _proj_res_p(a_p, w_dn, h_p1, g_fin, BM_DOWN, BK_DOWN, "final", "down1_p")

    y_prompt = y_p.reshape(batch, SEQ, d)
    y_sample = y_s
    tps = SEQ // BM_P
    new_conv_prompt = utail_p.reshape(batch, tps, TAIL, d)[:, tps - 1, TAIL - CONV_HIST:, :][None]
    tpp = SEQ // BM_POOL
    new_pool_prompt = ntail_p.reshape(batch, tpp, HALO, d)[:, tpp - 1, HALO - POOL_HIST:, :][None]
    new_conv_sample = uts[None]
    new_pool_sample = pool_tail_s[None]
    return (y_prompt, y_sample, new_conv_prompt, new_pool_prompt, new_conv_sample, new_pool_sample)
```

```python
import functools

import jax
import jax.numpy as jnp
from jax import lax
from jax.experimental import pallas as pl
from jax.experimental.pallas import tpu as pltpu

D_MODEL = 2048
D_FF = 5632
N_META = 16
SEQ = 2048
N_SAMPLE_SEQ = 128
SAMPLE_T = 8
CONV_HIST = 2
POOL_WINDOWS = (2, 4, 8, 16)
POOL_GROUP = D_MODEL // len(POOL_WINDOWS)
POOL_HIST = 15
EPS = 1e-6

N_SAMPLE_ROWS = N_SAMPLE_SEQ * SAMPLE_T
S_ROWS = N_SAMPLE_ROWS + 2 * N_META
META_ROW0 = N_SAMPLE_ROWS
TAIL = 8
HALO = 16
MXU_COLS = 256
FRONT = 8

V7X_SCOPED_VMEM_BYTES = 60000 * 1024

BF16 = jnp.bfloat16
F32 = jnp.float32


def _rms(x, g):
    ms = jnp.mean(x * x, axis=-1, keepdims=True)
    return (x * lax.rsqrt(ms + EPS)) * g


def _dot(a, b):
    return jnp.dot(a, b, preferred_element_type=F32)


def _params(n_axes):
    return pltpu.CompilerParams(
        dimension_semantics=("arbitrary",) * n_axes,
        vmem_limit_bytes=V7X_SCOPED_VMEM_BYTES,
    )


def _resident(shape, index_map):
    return pl.BlockSpec(shape, index_map, pipeline_mode=pl.Buffered(1))


def _conv3(w, u, ubuf, off):
    t = u.shape[0]
    return (w[2:3] * u + w[1:2] * ubuf[off - 1:off - 1 + t, :] + w[0:1] * ubuf[off - 2:off - 2 + t, :])


def _mix_p_kernel(tiles_per_seq, h_ref, g_ref, wb_ref, wc_ref, wv_ref, wdw_ref, umeta_ref,
                  z_ref, utail_ref, n_sc, ubuf, carry):
    i = pl.program_id(0)
    j = pl.program_id(1)
    bm = h_ref.shape[0]

    @pl.when(j == 0)
    def _():
        n_sc[...] = _rms(h_ref[...], g_ref[...]).astype(BF16)

    @pl.when(i % tiles_per_seq == 0)
    def _():
        ubuf[0:TAIL, :] = umeta_ref[...]

    @pl.when(i % tiles_per_seq != 0)
    def _():
        ubuf[0:TAIL, :] = carry[j]

    n = n_sc[...]
    for c0 in range(0, z_ref.shape[1], MXU_COLS):
        cols = slice(c0, c0 + MXU_COLS)
        b = _dot(n, wb_ref[:, cols])
        u = _dot(n, wc_ref[:, cols]) * _dot(n, wv_ref[:, cols])
        ubuf[TAIL:TAIL + bm, cols] = u
        z_ref[:, cols] = (b * _conv3(wdw_ref[:, cols], u, ubuf.at[:, cols], TAIL)).astype(BF16)
    tail = ubuf[bm:bm + TAIL, :]
    carry[j] = tail
    utail_ref[...] = tail


def _mix_s_kernel(xs_ref, meta_ref, g_ref, wb_ref, wc_ref, wv_ref, wdw_ref, hc_ref,
                  z_ref, uts_ref, umeta_ref, wb_o, wc_o, wv_o, h0_ref, n_sc, ubuf, mbuf):
    j = pl.program_id(0)
    ns = N_SAMPLE_ROWS
    nb = N_SAMPLE_SEQ
    nm = S_ROWS - ns

    @pl.when(j == 0)
    def _():
        for t in range(SAMPLE_T):
            h0_ref[t * nb:(t + 1) * nb, :] = xs_ref[:, t, :]
        h0_ref[ns:ns + N_META, :] = meta_ref[...]
        h0_ref[ns + N_META:, :] = jnp.zeros((nm - N_META, h0_ref.shape[1]), F32)
        n_sc[...] = _rms(h0_ref[...], g_ref[...]).astype(BF16)

    wb_o[...] = wb_ref[...].astype(BF16)
    wc_o[...] = wc_ref[...].astype(BF16)
    wv_o[...] = wv_ref[...].astype(BF16)
    n = n_sc[...]
    nh = CONV_HIST * nb
    ubuf[0:nh, :] = hc_ref[...]
    mbuf[0:TAIL, :] = jnp.zeros((TAIL, mbuf.shape[1]), F32)
    for c0 in range(0, z_ref.shape[1], MXU_COLS):
        cols = slice(c0, c0 + MXU_COLS)
        b = _dot(n, wb_o[:, cols])
        u = _dot(n, wc_o[:, cols]) * _dot(n, wv_o[:, cols])
        w = wdw_ref[:, cols]

        us = u[0:ns, :]
        ubuf[nh:nh + ns, cols] = us
        conv_s = (w[2:3] * us + w[1:2] * ubuf[nb:nb + ns, cols] + w[0:1] * ubuf[0:ns, cols])
        z_ref[0:ns, cols] = (b[0:ns, :] * conv_s).astype(BF16)

        um = u[ns:, :]
        mbuf[TAIL:TAIL + nm, cols] = um
        z_ref[ns:, cols] = (b[ns:, :] * _conv3(w, um, mbuf.at[:, cols], TAIL)).astype(BF16)
    uts_ref[...] = ubuf[ns:ns + nh, :]
    umeta_ref[...] = mbuf[N_META:N_META + TAIL, :]


def _mix_conv_p(h, g, wb, wc, wv, w_dw, umeta, bm, bn):
    rows = h.shape[0]
    n_i, n_j = rows // bm, D_MODEL // bn
    wspec = pl.BlockSpec((D_MODEL, bn), lambda i, j: (0, j))
    return pl.pallas_call(
        functools.partial(_mix_p_kernel, SEQ // bm),
        grid=(n_i, n_j),
        in_specs=[
            pl.BlockSpec((bm, D_MODEL), lambda i, j: (i, 0)),
            pl.BlockSpec((1, D_MODEL), lambda i, j: (0, 0)),
            wspec, wspec, wspec,
            pl.BlockSpec((3, bn), lambda i, j: (0, j)),
            pl.BlockSpec((TAIL, bn), lambda i, j: (0, j)),
        ],
        out_specs=[
            pl.BlockSpec((bm, bn), lambda i, j: (i, j)),
            pl.BlockSpec((TAIL, bn), lambda i, j: (i, j)),
        ],
        out_shape=[
            jax.ShapeDtypeStruct((rows, D_MODEL), BF16),
            jax.ShapeDtypeStruct((n_i * TAIL, D_MODEL), F32),
        ],
        scratch_shapes=[
            pltpu.VMEM((bm, D_MODEL), BF16),
            pltpu.VMEM((TAIL + bm, bn), F32),
            pltpu.VMEM((n_j, TAIL, bn), F32),
        ],
        compiler_params=_params(2),
        name="mix_conv_p",
    )(h, g, wb, wc, wv, w_dw, umeta)


def _mix_conv_s(x_sample, meta, g, w_in, w_dw, hc, bn):
    n_j = D_MODEL // bn
    nb = N_SAMPLE_SEQ
    nm = S_ROWS - N_SAMPLE_ROWS
    wspec = lambda part: pl.BlockSpec((None, D_MODEL, bn), lambda j: (0, 0, part * n_j + j))
    wout = pl.BlockSpec((D_MODEL, bn), lambda j: (0, j))
    wshape = jax.ShapeDtypeStruct((D_MODEL, D_MODEL), BF16)
    return pl.pallas_call(
        _mix_s_kernel,
        grid=(n_j,),
        in_specs=[
            _resident((nb, SAMPLE_T, D_MODEL), lambda j: (0, 0, 0)),
            _resident((N_META, D_MODEL), lambda j: (0, 0)),
            pl.BlockSpec((1, D_MODEL), lambda j: (0, 0)),
            wspec(0), wspec(1), wspec(2),
            pl.BlockSpec((3, bn), lambda j: (0, j)),
            pl.BlockSpec((CONV_HIST * nb, bn), lambda j: (0, j)),
        ],
        out_specs=[
            pl.BlockSpec((S_ROWS, bn), lambda j: (0, j)),
            pl.BlockSpec((CONV_HIST * nb, bn), lambda j: (0, j)),
            pl.BlockSpec((TAIL, bn), lambda j: (0, j)),
            wout, wout, wout,
            pl.BlockSpec((S_ROWS, D_MODEL), lambda j: (0, 0)),
        ],
        out_shape=[
            jax.ShapeDtypeStruct((S_ROWS, D_MODEL), BF16),
            jax.ShapeDtypeStruct((CONV_HIST * nb, D_MODEL), F32),
            jax.ShapeDtypeStruct((TAIL, D_MODEL), F32),
            wshape, wshape, wshape,
            jax.ShapeDtypeStruct((S_ROWS, D_MODEL), F32),
        ],
        scratch_shapes=[
            pltpu.VMEM((S_ROWS, D_MODEL), BF16),
            pltpu.VMEM((CONV_HIST * nb + N_SAMPLE_ROWS, bn), F32),
            pltpu.VMEM((TAIL + nm, bn), F32),
        ],
        compiler_params=_params(1),
        name="mix_conv_s",
    )(x_sample, meta, g, w_in, w_in, w_in, w_dw, hc)


def _proj_res_body(norm, k, n_k, x_ref, w_ref, h_ref, g_ref, o_ref, m_ref):
    if n_k == 1:
        acc = h_ref[...] + _dot(x_ref[...], w_ref[...])
        o_ref[...] = _rms(acc, g_ref[...]) if norm == "final" else acc
        if norm == "emit":
            m_ref[...] = _rms(acc, g_ref[...]).astype(BF16)
        return

    @pl.when(k == 0)
    def _():
        o_ref[...] = h_ref[...] + _dot(x_ref[...], w_ref[...])

    @pl.when(k != 0)
    def _():
        o_ref[...] += _dot(x_ref[...], w_ref[...])

    if norm is not None:
        @pl.when(k == n_k - 1)
        def _():
            y = _rms(o_ref[...], g_ref[...])
            if norm == "final":
                o_ref[...] = y
            else:
                m_ref[...] = y.astype(BF16)


def _proj_res_p_kernel(norm, n_k, x_ref, w_ref, h_ref, g_ref, o_ref, m_ref=None):
    _proj_res_body(norm, pl.program_id(1), n_k, x_ref, w_ref, h_ref, g_ref, o_ref, m_ref)


def _proj_res_s_kernel(norm, n_k, x_ref, w_ref, h_ref, g_ref, o_ref, wo_ref, acc=None):
    k = pl.program_id(0)
    wo_ref[...] = w_ref[...].astype(BF16)
    if norm != "final_sample":
        _proj_res_body(norm, k, n_k, x_ref, wo_ref, h_ref, g_ref, o_ref, None)
        return
    _proj_res_body(None, k, n_k, x_ref, wo_ref, h_ref, g_ref, acc, None)

    @pl.when(k == n_k - 1)
    def _():
        y = _rms(acc[0:N_SAMPLE_ROWS, :], g_ref[...])
        for t in range(SAMPLE_T):
            o_ref[:, t, :] = y[t * N_SAMPLE_SEQ:(t + 1) * N_SAMPLE_SEQ, :]


def _proj_res_p(x, w, h, g, bm, bk, norm, name):
    rows, kdim = x.shape
    n_i, n_k = rows // bm, kdim // bk
    wspec = _resident if n_k == 1 else pl.BlockSpec
    row_spec = pl.BlockSpec((bm, D_MODEL), lambda i, k: (i, 0))
    out_specs, out_shape = [row_spec], [jax.ShapeDtypeStruct((rows, D_MODEL), F32)]
    if norm == "emit":
        out_specs.append(row_spec)
        out_shape.append(jax.ShapeDtypeStruct((rows, D_MODEL), BF16))
    return pl.pallas_call(
        functools.partial(_proj_res_p_kernel, norm, n_k),
        grid=(n_i, n_k),
        in_specs=[
            pl.BlockSpec((bm, bk), lambda i, k: (i, k)),
            wspec((bk, D_MODEL), lambda i, k: (k, 0)),
            row_spec,
            pl.BlockSpec((1, D_MODEL), lambda i, k: (0, 0)),
        ],
        out_specs=out_specs,
        out_shape=out_shape,
        compiler_params=_params(2),
        name=name,
    )(x, w, h, g)


def _proj_res_s(x, w_stack, layer, h, g, bk, norm, name):
    assert norm in (None, "final_sample")
    rows, kdim = x.shape
    n_k = kdim // bk
    if norm == "final_sample":
        o_shape = (N_SAMPLE_SEQ, SAMPLE_T, D_MODEL)
        scratch = [pltpu.VMEM((rows, D_MODEL), F32)]
    else:
        o_shape = (rows, D_MODEL)
        scratch = []
    o_index = (0,) * len(o_shape)
    return pl.pallas_call(
        functools.partial(_proj_res_s_kernel, norm, n_k),
        grid=(n_k,),
        in_specs=[
            pl.BlockSpec((rows, bk), lambda k: (0, k)),
            pl.BlockSpec((None, bk, D_MODEL), lambda k: (layer, k, 0)),
            _resident((rows, D_MODEL), lambda k: (0, 0)),
            pl.BlockSpec((1, D_MODEL), lambda k: (0, 0)),
        ],
        out_specs=[
            pl.BlockSpec(o_shape, lambda k: o_index),
            pl.BlockSpec((bk, D_MODEL), lambda k: (k, 0)),
        ],
        out_shape=[
            jax.ShapeDtypeStruct(o_shape, F32),
            jax.ShapeDtypeStruct((kdim, D_MODEL), BF16),
        ],
        scratch_shapes=scratch,
        compiler_params=_params(1),
        name=name,
    )(x, w_stack, h, g)


def _swiglu_chunks(m, wg_ref, wu_ref, a_ref, wg_src=None, wu_src=None):
    for c0 in range(0, a_ref.shape[1], MXU_COLS):
        cols = slice(c0, c0 + MXU_COLS)
        if wg_src is not None:
            wg_ref[:, cols] = wg_src[:, cols].astype(BF16)
            wu_ref[:, cols] = wu_src[:, cols].astype(BF16)
        gate = _dot(m, wg_ref[:, cols])
        up = _dot(m, wu_ref[:, cols])
        a_ref[:, cols] = (gate * (1.0 / (1.0 + jnp.exp(-gate))) * up).astype(BF16)


def _gate_up_p_kernel(m_ref, wg_ref, wu_ref, a_ref):
    _swiglu_chunks(m_ref[...], wg_ref, wu_ref, a_ref)


def _gate_up_s_kernel(h_ref, g_ref, wg_ref, wu_ref, a_ref, wg_o, wu_o, m_sc):
    @pl.when(pl.program_id(0) == 0)
    def _():
        m_sc[...] = _rms(h_ref[...], g_ref[...]).astype(BF16)

    _swiglu_chunks(m_sc[...], wg_o, wu_o, a_ref, wg_ref, wu_ref)


def _gate_up_p(m, wg, wu, bm, bn, name):
    rows = m.shape[0]
    n_i, n_j = rows // bm, D_FF // bn
    wspec = pl.BlockSpec((D_MODEL, bn), lambda i, j: (0, j))
    return pl.pallas_call(
        _gate_up_p_kernel,
        grid=(n_i, n_j),
        in_specs=[pl.BlockSpec((bm, D_MODEL), lambda i, j: (i, 0)), wspec, wspec],
        out_specs=pl.BlockSpec((bm, bn), lambda i, j: (i, j)),
        out_shape=jax.ShapeDtypeStruct((rows, D_FF), BF16),
        compiler_params=_params(2),
        name=name,
    )(m, wg, wu)


def _gate_up_s(h, g, w_gu, layer, bn, name):
    rows = h.shape[0]
    n_j = D_FF // bn
    wout = pl.BlockSpec((D_MODEL, bn), lambda j: (0, j))
    wshape = jax.ShapeDtypeStruct((D_MODEL, D_FF), BF16)
    return pl.pallas_call(
        _gate_up_s_kernel,
        grid=(n_j,),
        in_specs=[
            _resident((rows, D_MODEL), lambda j: (0, 0)),
            pl.BlockSpec((1, D_MODEL), lambda j: (0, 0)),
            pl.BlockSpec((None, D_MODEL, bn), lambda j: (layer, 0, j)),
            pl.BlockSpec((None, D_MODEL, bn), lambda j: (layer, 0, n_j + j)),
        ],
        out_specs=[pl.BlockSpec((rows, bn), lambda j: (0, j)), wout, wout],
        out_shape=[jax.ShapeDtypeStruct((rows, D_FF), BF16), wshape, wshape],
        scratch_shapes=[pltpu.VMEM((rows, D_MODEL), BF16)],
        compiler_params=_params(1),
        name=name,
    )(h, g, w_gu, w_gu)


def _pool_p_kernel(tiles_per_seq, h_ref, halo_p_ref, halo_s_ref, g_ref, wp_ref, sc_ref, gnext_ref,
                   o_ref, ntail_ref, m_ref, nbuf, pbuf, qbuf):
    i = pl.program_id(0)
    bm = h_ref.shape[0]
    g = g_ref[...]
    x = h_ref[...]
    n = _rms(x, g)
    rows = HALO + bm
    r0 = FRONT + HALO

    for buf in (nbuf, pbuf, qbuf):
        buf[0:FRONT, :] = jnp.zeros((FRONT, buf.shape[1]), F32)

    @pl.when(i % tiles_per_seq == 0)
    def _():
        nbuf[FRONT:r0, :] = _rms(halo_s_ref[...], g)

    @pl.when(i % tiles_per_seq != 0)
    def _():
        nbuf[FRONT:r0, :] = _rms(halo_p_ref[...], g)

    nbuf[r0:r0 + bm, :] = n
    ntail_ref[...] = n[bm - HALO:, :]
    for gi, win in enumerate(POOL_WINDOWS):
        cols = slice(gi * POOL_GROUP, (gi + 1) * POOL_GROUP)
        src, shift, level = nbuf, 1, 0
        while shift < win:
            dst = (pbuf, qbuf)[level % 2]
            dst[FRONT:FRONT + rows, cols] = (src[FRONT:FRONT + rows, cols]
                                             + src[FRONT - shift:FRONT - shift + rows, cols])
            src, shift, level = dst, 2 * shift, level + 1
        ng = n[:, cols]
        p = src[r0:r0 + bm, cols] * (1.0 / win) - ng
        y = _dot(p.astype(BF16), wp_ref[gi].astype(BF16)) * sc_ref[:, cols]
        o_ref[:, cols] = x[:, cols] + y
    m_ref[...] = _rms(o_ref[...], gnext_ref[...]).astype(BF16)


def _pool_p(h_p, h_s, g, wp, scale, g_next, bm):
    rows = h_p.shape[0]
    n_i = rows // bm
    ng = len(POOL_WINDOWS)
    return pl.pallas_call(
        functools.partial(_pool_p_kernel, SEQ // bm),
        grid=(n_i,),
        in_specs=[
            pl.BlockSpec((bm, D_MODEL), lambda i: (i, 0)),
            pl.BlockSpec((HALO, D_MODEL), lambda i: (jnp.maximum(i * (bm // HALO) - 1, 0), 0)),
            pl.BlockSpec((HALO, D_MODEL), lambda i: (META_ROW0 // HALO, 0)),
            pl.BlockSpec((1, D_MODEL), lambda i: (0, 0)),
            pl.BlockSpec((None, ng, POOL_GROUP, POOL_GROUP), lambda i: (0, 0, 0, 0)),
            pl.BlockSpec((1, D_MODEL), lambda i: (0, 0)),
            pl.BlockSpec((1, D_MODEL), lambda i: (0, 0)),
        ],
        out_specs=[
            pl.BlockSpec((bm, D_MODEL), lambda i: (i, 0)),
            pl.BlockSpec((HALO, D_MODEL), lambda i: (i, 0)),
            pl.BlockSpec((bm, D_MODEL), lambda i: (i, 0)),
        ],
        out_shape=[
            jax.ShapeDtypeStruct((rows, D_MODEL), F32),
            jax.ShapeDtypeStruct((n_i * HALO, D_MODEL), F32),
            jax.ShapeDtypeStruct((rows, D_MODEL), BF16),
        ],
        scratch_shapes=[pltpu.VMEM((FRONT + HALO + bm, D_MODEL), F32)] * 3,
        compiler_params=_params(1),
        name="pool_p",
    )(h_p, h_p, h_s, g, wp, scale, g_next)


def _pool_s_kernel(hfull_ref, hcol_ref, hp_ref, g_ref, wp_ref, sc_ref, o_ref, tail_ref, inv_sc, nbuf, sum_sc):
    j = pl.program_id(0)
    ns = N_SAMPLE_ROWS
    nb = N_SAMPLE_SEQ
    nh = POOL_HIST * nb

    @pl.when(j == 0)
    def _():
        xf = hfull_ref[...]
        inv_sc[...] = lax.rsqrt(jnp.mean(xf * xf, axis=-1, keepdims=True) + EPS)

    x = hcol_ref[...]
    n = ((x * inv_sc[...]) * g_ref[...])[0:ns, :]
    nbuf[0:nh, :] = hp_ref[...]
    nbuf[nh:nh + ns, :] = n
    tail_ref[...] = nbuf[ns:ns + nh, :]

    for gi, win in enumerate(POOL_WINDOWS):
        @pl.when(j == gi)
        def _(win=win):
            acc = n
            for k in range(1, win):
                acc = acc + nbuf[nh - k * nb:nh - k * nb + ns, :]
            sum_sc[...] = acc * (1.0 / win)

    p = sum_sc[...] - n
    y = _dot(p.astype(BF16), wp_ref[...].astype(BF16)) * sc_ref[...]
    o_ref[0:ns, :] = x[0:ns, :] + y
    o_ref[ns:, :] = x[ns:, :]


def _pool_s(h_s, state_pool, g, wp, scale):
    ns = N_SAMPLE_ROWS
    nb = N_SAMPLE_SEQ
    nh = POOL_HIST * nb
    pg = POOL_GROUP
    n_g = len(POOL_WINDOWS)
    state_spec = pl.BlockSpec((nh, pg), lambda j: (0, j))
    return pl.pallas_call(
        _pool_s_kernel,
        grid=(n_g,),
        in_specs=[
            _resident((S_ROWS, D_MODEL), lambda j: (0, 0)),
            pl.BlockSpec((S_ROWS, pg), lambda j: (0, j)),
            state_spec,
            pl.BlockSpec((1, pg), lambda j: (0, j)),
            pl.BlockSpec((None, None, pg, pg), lambda j: (0, j, 0, 0)),
            pl.BlockSpec((1, pg), lambda j: (0, j)),
        ],
        out_specs=[pl.BlockSpec((S_ROWS, pg), lambda j: (0, j)), state_spec],
        out_shape=[
            jax.ShapeDtypeStruct((S_ROWS, D_MODEL), F32),
            jax.ShapeDtypeStruct((nh, D_MODEL), F32),
        ],
        scratch_shapes=[
            pltpu.VMEM((S_ROWS, 1), F32),
            pltpu.VMEM((nh + ns, pg), F32),
            pltpu.VMEM((ns, pg), F32),
        ],
        compiler_params=_params(1),
        name="pool_s",
    )(h_s, h_s, state_pool, g, wp, scale)


BM_P = 1024
BM_GATE_UP = 2048
BM_POOL = 512
BM_CONV_OUT = 512
BN_P = 512
BM_DOWN = 512
BK_DOWN = D_FF
BN_S = 256
BN_S_FF = 512
BK_S = 512


def kernel(x_prompt, x_sample, state_conv, state_pool, meta_tokens, norm_mix, norm_ffn, norm_final,
           conv_w_in, conv_w_dw, conv_w_out, pool_w, pool_scale, ffn_w_gate_up, ffn_w_down):
    d = D_MODEL
    nb, nt = N_SAMPLE_SEQ, SAMPLE_T
    batch = x_prompt.shape[0]

    h_p = x_prompt.reshape(batch * SEQ, d)
    hc = state_conv[0].transpose(1, 0, 2).reshape(CONV_HIST * nb, d)
    hp = state_pool[0].transpose(1, 0, 2).reshape(POOL_HIST * nb, d)
    row = lambda v: v.reshape(1, d)
    g_mix, g_ffn, g_fin = norm_mix, norm_ffn, row(norm_final)

    z_s, uts, umeta, wb, wc, wv, h_s = _mix_conv_s(
        x_sample, meta_tokens, row(g_mix[0]), conv_w_in, conv_w_dw[0], hc, BN_S)
    z_p, utail_p = _mix_conv_p(h_p, row(g_mix[0]), wb, wc, wv, conv_w_dw[0], umeta, BM_P, BN_P)
    h_s, w_out = _proj_res_s(z_s, conv_w_out, 0, h_s, g_fin, BK_S, None, "conv_out_s")
    h_p, m_p = _proj_res_p(z_p, w_out, h_p, row(g_ffn[0]), BM_CONV_OUT, D_MODEL, "emit", "conv_out_p")
    a_s, wg, wu = _gate_up_s(h_s, row(g_ffn[0]), ffn_w_gate_up, 0, BN_S_FF, "gate_up0_s")
    a_p = _gate_up_p(m_p, wg, wu, BM_GATE_UP, BN_P, "gate_up0_p")
    h_s, w_dn = _proj_res_s(a_s, ffn_w_down, 0, h_s, g_fin, BK_S, None, "down0_s")
    (h_p,) = _proj_res_p(a_p, w_dn, h_p, g_fin, BM_DOWN, BK_DOWN, None, "down0_p")
    h_p1, ntail_p, m_p = _pool_p(h_p, h_s, row(g_mix[1]), pool_w, row(pool_scale[0]), row(g_ffn[1]), BM_POOL)
    h_s1, pool_tail_s = _pool_s(h_s, hp, row(g_mix[1]), pool_w, row(pool_scale[0]))
    a_s, wg, wu = _gate_up_s(h_s1, row(g_ffn[1]), ffn_w_gate_up, 1, BN_S_FF, "gate_up1_s")
    a_p = _gate_up_p(m_p, wg, wu, BM_GATE_UP, BN_P, "gate_up1_p")
    y_s, w_dn = _proj_res_s(a_s, ffn_w_down, 1, h_s1, g_fin, BK_S, "final_sample", "down1_s")
    (y_p,) = _proj_res_p(a_p, w_dn, h_p1, g_fin, BM_DOWN, BK_DOWN, "final", "down1_p")

    y_prompt = y_p.reshape(batch, SEQ, d)
    y_sample = y_s
    tps = SEQ // BM_P
    new_conv_prompt = utail_p.reshape(batch, tps, TAIL, d)[:, tps - 1, TAIL - CONV_HIST:, :][None]
    tpp = SEQ // BM_POOL
    new_pool_prompt = ntail_p.reshape(batch, tpp, HALO, d)[:, tpp - 1, HALO - POOL_HIST:, :][None]
    new_conv_sample = uts.reshape(CONV_HIST, nb, d).transpose(1, 0, 2)[None]
    new_pool_sample = pool_tail_s.reshape(POOL_HIST, nb, d).transpose(1, 0, 2)[None]
    return (y_prompt, y_sample, new_conv_prompt, new_pool_prompt, new_conv_sample, new_pool_sample)
```

```python
import functools

import jax
import jax.numpy as jnp
from jax import lax
from jax.experimental import pallas as pl
from jax.experimental.pallas import tpu as pltpu

D_MODEL = 2048
D_FF = 5632
N_META = 16
SEQ = 2048
N_SAMPLE_SEQ = 128
SAMPLE_T = 8
CONV_HIST = 2
POOL_WINDOWS = (2, 4, 8, 16)
POOL_GROUP = D_MODEL // len(POOL_WINDOWS)
POOL_HIST = 15
EPS = 1e-6

N_SAMPLE_ROWS = N_SAMPLE_SEQ * SAMPLE_T
S_ROWS = N_SAMPLE_ROWS + 2 * N_META
META_ROW0 = N_SAMPLE_ROWS
TAIL = 8
HALO = 16
MXU_COLS = 256
FRONT = 8

V7X_SCOPED_VMEM_BYTES = 60000 * 1024

BF16 = jnp.bfloat16
F32 = jnp.float32


def _rms(x, g):
    ms = jnp.mean(x * x, axis=-1, keepdims=True)
    return (x * lax.rsqrt(ms + EPS)) * g


def _dot(a, b):
    return jnp.dot(a, b, preferred_element_type=F32)


def _params(n_axes):
    return pltpu.CompilerParams(
        dimension_semantics=("arbitrary",) * n_axes,
        vmem_limit_bytes=V7X_SCOPED_VMEM_BYTES,
    )


def _resident(shape, index_map):
    return pl.BlockSpec(shape, index_map, pipeline_mode=pl.Buffered(1))


def _conv3(w, u, ubuf, off):
    t = u.shape[0]
    return (w[2:3] * u + w[1:2] * ubuf[off - 1:off - 1 + t, :] + w[0:1] * ubuf[off - 2:off - 2 + t, :])


def _mix_p_kernel(tiles_per_seq, h_ref, g_ref, wb_ref, wc_ref, wv_ref, wdw_ref, umeta_ref,
                  z_ref, utail_ref, n_sc, ubuf, carry):
    i = pl.program_id(0)
    j = pl.program_id(1)
    bm = h_ref.shape[0]

    @pl.when(j == 0)
    def _():
        n_sc[...] = _rms(h_ref[...], g_ref[...]).astype(BF16)

    @pl.when(i % tiles_per_seq == 0)
    def _():
        ubuf[0:TAIL, :] = umeta_ref[...]

    @pl.when(i % tiles_per_seq != 0)
    def _():
        ubuf[0:TAIL, :] = carry[j]

    n = n_sc[...]
    for c0 in range(0, z_ref.shape[1], MXU_COLS):
        cols = slice(c0, c0 + MXU_COLS)
        b = _dot(n, wb_ref[:, cols])
        u = _dot(n, wc_ref[:, cols]) * _dot(n, wv_ref[:, cols])
        ubuf[TAIL:TAIL + bm, cols] = u
        z_ref[:, cols] = (b * _conv3(wdw_ref[:, cols], u, ubuf.at[:, cols], TAIL)).astype(BF16)
    tail = ubuf[bm:bm + TAIL, :]
    carry[j] = tail
    utail_ref[...] = tail


def _mix_s_kernel(xs_ref, meta_ref, g_ref, wb_ref, wc_ref, wv_ref, wdw_ref, hc_ref,
                  z_ref, uts_ref, umeta_ref, wb_o, wc_o, wv_o, h0_ref, n_sc, ubuf, mbuf):
    j = pl.program_id(0)
    ns = N_SAMPLE_ROWS
    nb = N_SAMPLE_SEQ
    nm = S_ROWS - ns

    @pl.when(j == 0)
    def _():
        for t in range(SAMPLE_T):
            h0_ref[t * nb:(t + 1) * nb, :] = xs_ref[:, t, :]
        h0_ref[ns:ns + N_META, :] = meta_ref[...]
        h0_ref[ns + N_META:, :] = jnp.zeros((nm - N_META, h0_ref.shape[1]), F32)
        n_sc[...] = _rms(h0_ref[...], g_ref[...]).astype(BF16)

    wb_o[...] = wb_ref[...].astype(BF16)
    wc_o[...] = wc_ref[...].astype(BF16)
    wv_o[...] = wv_ref[...].astype(BF16)
    n = n_sc[...]
    nh = CONV_HIST * nb
    ubuf[0:nh, :] = hc_ref[...]
    mbuf[0:TAIL, :] = jnp.zeros((TAIL, mbuf.shape[1]), F32)
    for c0 in range(0, z_ref.shape[1], MXU_COLS):
        cols = slice(c0, c0 + MXU_COLS)
        b = _dot(n, wb_o[:, cols])
        u = _dot(n, wc_o[:, cols]) * _dot(n, wv_o[:, cols])
        w = wdw_ref[:, cols]

        us = u[0:ns, :]
        ubuf[nh:nh + ns, cols] = us
        conv_s = (w[2:3] * us + w[1:2] * ubuf[nb:nb + ns, cols] + w[0:1] * ubuf[0:ns, cols])
        z_ref[0:ns, cols] = (b[0:ns, :] * conv_s).astype(BF16)

        um = u[ns:, :]
        mbuf[TAIL:TAIL + nm, cols] = um
        z_ref[ns:, cols] = (b[ns:, :] * _conv3(w, um, mbuf.at[:, cols], TAIL)).astype(BF16)
    uts_ref[...] = ubuf[ns:ns + nh, :]
    umeta_ref[...] = mbuf[N_META:N_META + TAIL, :]


def _mix_conv_p(h, g, wb, wc, wv, w_dw, umeta, bm, bn):
    rows = h.shape[0]
    n_i, n_j = rows // bm, D_MODEL // bn
    wspec = pl.BlockSpec((D_MODEL, bn), lambda i, j: (0, j))
    return pl.pallas_call(
        functools.partial(_mix_p_kernel, SEQ // bm),
        grid=(n_i, n_j),
        in_specs=[
            pl.BlockSpec((bm, D_MODEL), lambda i, j: (i, 0)),
            pl.BlockSpec((1, D_MODEL), lambda i, j: (0, 0)),
            wspec, wspec, wspec,
            pl.BlockSpec((3, bn), lambda i, j: (0, j)),
            pl.BlockSpec((TAIL, bn), lambda i, j: (0, j)),
        ],
        out_specs=[
            pl.BlockSpec((bm, bn), lambda i, j: (i, j)),
            pl.BlockSpec((TAIL, bn), lambda i, j: (i, j)),
        ],
        out_shape=[
            jax.ShapeDtypeStruct((rows, D_MODEL), BF16),
            jax.ShapeDtypeStruct((n_i * TAIL, D_MODEL), F32),
        ],
        scratch_shapes=[
            pltpu.VMEM((bm, D_MODEL), BF16),
            pltpu.VMEM((TAIL + bm, bn), F32),
            pltpu.VMEM((n_j, TAIL, bn), F32),
        ],
        compiler_params=_params(2),
        name="mix_conv_p",
    )(h, g, wb, wc, wv, w_dw, umeta)


def _mix_conv_s(x_sample, meta, g, w_in, w_dw, hc, bn):
    n_j = D_MODEL // bn
    nb = N_SAMPLE_SEQ
    nm = S_ROWS - N_SAMPLE_ROWS
    wspec = lambda part: pl.BlockSpec((None, D_MODEL, bn), lambda j: (0, 0, part * n_j + j))
    wout = pl.BlockSpec((D_MODEL, bn), lambda j: (0, j))
    wshape = jax.ShapeDtypeStruct((D_MODEL, D_MODEL), BF16)
    return pl.pallas_call(
        _mix_s_kernel,
        grid=(n_j,),
        in_specs=[
            _resident((nb, SAMPLE_T, D_MODEL), lambda j: (0, 0, 0)),
            _resident((N_META, D_MODEL), lambda j: (0, 0)),
            pl.BlockSpec((1, D_MODEL), lambda j: (0, 0)),
            wspec(0), wspec(1), wspec(2),
            pl.BlockSpec((3, bn), lambda j: (0, j)),
            pl.BlockSpec((CONV_HIST * nb, bn), lambda j: (0, j)),
        ],
        out_specs=[
            pl.BlockSpec((S_ROWS, bn), lambda j: (0, j)),
            pl.BlockSpec((CONV_HIST * nb, bn), lambda j: (0, j)),
            pl.BlockSpec((TAIL, bn), lambda j: (0, j)),
            wout, wout, wout,
            pl.BlockSpec((S_ROWS, D_MODEL), lambda j: (0, 0)),
        ],
        out_shape=[
            jax.ShapeDtypeStruct((S_ROWS, D_MODEL), BF16),
            jax.ShapeDtypeStruct((CONV_HIST * nb, D_MODEL), F32),
            jax.ShapeDtypeStruct((TAIL, D_MODEL), F32),
            wshape, wshape, wshape,
            jax.ShapeDtypeStruct((S_ROWS, D_MODEL), F32),
        ],
        scratch_shapes=[
            pltpu.VMEM((S_ROWS, D_MODEL), BF16),
            pltpu.VMEM((CONV_HIST * nb + N_SAMPLE_ROWS, bn), F32),
            pltpu.VMEM((TAIL + nm, bn), F32),
        ],
        compiler_params=_params(1),
        name="mix_conv_s",
    )(x_sample, meta, g, w_in, w_in, w_in, w_dw, hc)


def _proj_res_body(norm, k, n_k, x_ref, w_ref, h_ref, g_ref, o_ref, m_ref):
    if n_k == 1:
        acc = h_ref[...] + _dot(x_ref[...], w_ref[...])
        o_ref[...] = _rms(acc, g_ref[...]) if norm == "final" else acc
        if norm == "emit":
            m_ref[...] = _rms(acc, g_ref[...]).astype(BF16)
        return

    @pl.when(k == 0)
    def _():
        o_ref[...] = h_ref[...] + _dot(x_ref[...], w_ref[...])

    @pl.when(k != 0)
    def _():
        o_ref[...] += _dot(x_ref[...], w_ref[...])

    if norm is not None:
        @pl.when(k == n_k - 1)
        def _():
            y = _rms(o_ref[...], g_ref[...])
            if norm == "final":
                o_ref[...] = y
            else:
                m_ref[...] = y.astype(BF16)


def _proj_res_p_kernel(norm, n_k, x_ref, w_ref, h_ref, g_ref, o_ref, m_ref=None):
    _proj_res_body(norm, pl.program_id(1), n_k, x_ref, w_ref, h_ref, g_ref, o_ref, m_ref)


def _proj_res_s_kernel(norm, n_k, x_ref, w_ref, h_ref, g_ref, o_ref, wo_ref, acc=None):
    k = pl.program_id(0)
    wo_ref[...] = w_ref[...].astype(BF16)
    if norm != "final_sample":
        _proj_res_body(norm, k, n_k, x_ref, wo_ref, h_ref, g_ref, o_ref, None)
        return
    _proj_res_body(None, k, n_k, x_ref, wo_ref, h_ref, g_ref, acc, None)

    @pl.when(k == n_k - 1)
    def _():
        y = _rms(acc[0:N_SAMPLE_ROWS, :], g_ref[...])
        for t in range(SAMPLE_T):
            o_ref[:, t, :] = y[t * N_SAMPLE_SEQ:(t + 1) * N_SAMPLE_SEQ, :]


def _proj_res_p(x, w, h, g, bm, bk, norm, name):
    rows, kdim = x.shape
    n_i, n_k = rows // bm, kdim // bk
    wspec = _resident if n_k == 1 else pl.BlockSpec
    row_spec = pl.BlockSpec((bm, D_MODEL), lambda i, k: (i, 0))
    out_specs, out_shape = [row_spec], [jax.ShapeDtypeStruct((rows, D_MODEL), F32)]
    if norm == "emit":
        out_specs.append(row_spec)
        out_shape.append(jax.ShapeDtypeStruct((rows, D_MODEL), BF16))
    return pl.pallas_call(
        functools.partial(_proj_res_p_kernel, norm, n_k),
        grid=(n_i, n_k),
        in_specs=[
            pl.BlockSpec((bm, bk), lambda i, k: (i, k)),
            wspec((bk, D_MODEL), lambda i, k: (k, 0)),
            row_spec,
            pl.BlockSpec((1, D_MODEL), lambda i, k: (0, 0)),
        ],
        out_specs=out_specs,
        out_shape=out_shape,
        compiler_params=_params(2),
        name=name,
    )(x, w, h, g)


def _proj_res_s(x, w_stack, layer, h, g, bk, norm, name):
    assert norm in (None, "final_sample")
    rows, kdim = x.shape
    n_k = kdim // bk
    if norm == "final_sample":
        o_shape = (N_SAMPLE_SEQ, SAMPLE_T, D_MODEL)
        scratch = [pltpu.VMEM((rows, D_MODEL), F32)]
    else:
        o_shape = (rows, D_MODEL)
        scratch = []
    o_index = (0,) * len(o_shape)
    return pl.pallas_call(
        functools.partial(_proj_res_s_kernel, norm, n_k),
        grid=(n_k,),
        in_specs=[
            pl.BlockSpec((rows, bk), lambda k: (0, k)),
            pl.BlockSpec((None, bk, D_MODEL), lambda k: (layer, k, 0)),
            _resident((rows, D_MODEL), lambda k: (0, 0)),
            pl.BlockSpec((1, D_MODEL), lambda k: (0, 0)),
        ],
        out_specs=[
            pl.BlockSpec(o_shape, lambda k: o_index),
            pl.BlockSpec((bk, D_MODEL), lambda k: (k, 0)),
        ],
        out_shape=[
            jax.ShapeDtypeStruct(o_shape, F32),
            jax.ShapeDtypeStruct((kdim, D_MODEL), BF16),
        ],
        scratch_shapes=scratch,
        compiler_params=_params(1),
        name=name,
    )(x, w_stack, h, g)


def _swiglu_chunks(m, wg_ref, wu_ref, a_ref, wg_src=None, wu_src=None):
    for c0 in range(0, a_ref.shape[1], MXU_COLS):
        cols = slice(c0, c0 + MXU_COLS)
        if wg_src is not None:
            wg_ref[:, cols] = wg_src[:, cols].astype(BF16)
            wu_ref[:, cols] = wu_src[:, cols].astype(BF16)
        gate = _dot(m, wg_ref[:, cols])
        up = _dot(m, wu_ref[:, cols])
        a_ref[:, cols] = (gate * (1.0 / (1.0 + jnp.exp(-gate))) * up).astype(BF16)


def _gate_up_p_kernel(m_ref, wg_ref, wu_ref, a_ref):
    _swiglu_chunks(m_ref[...], wg_ref, wu_ref, a_ref)


def _gate_up_s_kernel(h_ref, g_ref, wg_ref, wu_ref, a_ref, wg_o, wu_o, m_sc):
    @pl.when(pl.program_id(0) == 0)
    def _():
        m_sc[...] = _rms(h_ref[...], g_ref[...]).astype(BF16)

    _swiglu_chunks(m_sc[...], wg_o, wu_o, a_ref, wg_ref, wu_ref)


def _gate_up_p(m, wg, wu, bm, bn, name):
    rows = m.shape[0]
    n_i, n_j = rows // bm, D_FF // bn
    wspec = pl.BlockSpec((D_MODEL, bn), lambda i, j: (0, j))
    return pl.pallas_call(
        _gate_up_p_kernel,
        grid=(n_i, n_j),
        in_specs=[pl.BlockSpec((bm, D_MODEL), lambda i, j: (i, 0)), wspec, wspec],
        out_specs=pl.BlockSpec((bm, bn), lambda i, j: (i, j)),
        out_shape=jax.ShapeDtypeStruct((rows, D_FF), BF16),
        compiler_params=_params(2),
        name=name,
    )(m, wg, wu)


def _gate_up_s(h, g, w_gu, layer, bn, name):
    rows = h.shape[0]
    n_j = D_FF // bn
    wout = pl.BlockSpec((D_MODEL, bn), lambda j: (0, j))
    wshape = jax.ShapeDtypeStruct((D_MODEL, D_FF), BF16)
    return pl.pallas_call(
        _gate_up_s_kernel,
        grid=(n_j,),
        in_specs=[
            _resident((rows, D_MODEL), lambda j: (0, 0)),
            pl.BlockSpec((1, D_MODEL), lambda j: (0, 0)),
            pl.BlockSpec((None, D_MODEL, bn), lambda j: (layer, 0, j)),
            pl.BlockSpec((None, D_MODEL, bn), lambda j: (layer, 0, n_j + j)),
        ],
        out_specs=[pl.BlockSpec((rows, bn), lambda j: (0, j)), wout, wout],
        out_shape=[jax.ShapeDtypeStruct((rows, D_FF), BF16), wshape, wshape],
        scratch_shapes=[pltpu.VMEM((rows, D_MODEL), BF16)],
        compiler_params=_params(1),
        name=name,
    )(h, g, w_gu, w_gu)


def _pool_p_kernel(tiles_per_seq, h_ref, halo_p_ref, halo_s_ref, g_ref, wp_ref, sc_ref, gnext_ref,
                   o_ref, ntail_ref, m_ref, nbuf, pbuf, qbuf):
    i = pl.program_id(0)
    bm = h_ref.shape[0]
    g = g_ref[...]
    x = h_ref[...]
    n = _rms(x, g)
    rows = HALO + bm
    r0 = FRONT + HALO

    for buf in (nbuf, pbuf, qbuf):
        buf[0:FRONT, :] = jnp.zeros((FRONT, buf.shape[1]), F32)

    @pl.when(i % tiles_per_seq == 0)
    def _():
        nbuf[FRONT:r0, :] = _rms(halo_s_ref[...], g)

    @pl.when(i % tiles_per_seq != 0)
    def _():
        nbuf[FRONT:r0, :] = _rms(halo_p_ref[...], g)

    nbuf[r0:r0 + bm, :] = n
    ntail_ref[...] = n[bm - HALO:, :]
    for gi, win in enumerate(POOL_WINDOWS):
        cols = slice(gi * POOL_GROUP, (gi + 1) * POOL_GROUP)
        src, shift, level = nbuf, 1, 0
        while shift < win:
            dst = (pbuf, qbuf)[level % 2]
            dst[FRONT:FRONT + rows, cols] = (src[FRONT:FRONT + rows, cols]
                                             + src[FRONT - shift:FRONT - shift + rows, cols])
            src, shift, level = dst, 2 * shift, level + 1
        ng = n[:, cols]
        p = src[r0:r0 + bm, cols] * (1.0 / win) - ng
        y = _dot(p.astype(BF16), wp_ref[gi].astype(BF16)) * sc_ref[:, cols]
        o_ref[:, cols] = x[:, cols] + y
    m_ref[...] = _rms(o_ref[...], gnext_ref[...]).astype(BF16)


def _pool_p(h_p, h_s, g, wp, scale, g_next, bm):
    rows = h_p.shape[0]
    n_i = rows // bm
    ng = len(POOL_WINDOWS)
    return pl.pallas_call(
        functools.partial(_pool_p_kernel, SEQ // bm),
        grid=(n_i,),
        in_specs=[
            pl.BlockSpec((bm, D_MODEL), lambda i: (i, 0)),
            pl.BlockSpec((HALO, D_MODEL), lambda i: (jnp.maximum(i * (bm // HALO) - 1, 0), 0)),
            pl.BlockSpec((HALO, D_MODEL), lambda i: (META_ROW0 // HALO, 0)),
            pl.BlockSpec((1, D_MODEL), lambda i: (0, 0)),
            pl.BlockSpec((None, ng, POOL_GROUP, POOL_GROUP), lambda i: (0, 0, 0, 0)),
            pl.BlockSpec((1, D_MODEL), lambda i: (0, 0)),
            pl.BlockSpec((1, D_MODEL), lambda i: (0, 0)),
        ],
        out_specs=[
            pl.BlockSpec((bm, D_MODEL), lambda i: (i, 0)),
            pl.BlockSpec((HALO, D_MODEL), lambda i: (i, 0)),
            pl.BlockSpec((bm, D_MODEL), lambda i: (i, 0)),
        ],
        out_shape=[
            jax.ShapeDtypeStruct((rows, D_MODEL), F32),
            jax.ShapeDtypeStruct((n_i * HALO, D_MODEL), F32),
            jax.ShapeDtypeStruct((rows, D_MODEL), BF16),
        ],
        scratch_shapes=[pltpu.VMEM((FRONT + HALO + bm, D_MODEL), F32)] * 3,
        compiler_params=_params(1),
        name="pool_p",
    )(h_p, h_p, h_s, g, wp, scale, g_next)


def _pool_s_kernel(hfull_ref, hcol_ref, hp_ref, g_ref, wp_ref, sc_ref, o_ref, tail_ref, inv_sc, nbuf, sum_sc):
    j = pl.program_id(0)
    ns = N_SAMPLE_ROWS
    nb = N_SAMPLE_SEQ
    nh = POOL_HIST * nb

    @pl.when(j == 0)
    def _():
        xf = hfull_ref[...]
        inv_sc[...] = lax.rsqrt(jnp.mean(xf * xf, axis=-1, keepdims=True) + EPS)

    x = hcol_ref[...]
    n = ((x * inv_sc[...]) * g_ref[...])[0:ns, :]
    nbuf[0:nh, :] = hp_ref[...]
    nbuf[nh:nh + ns, :] = n
    tail_ref[...] = nbuf[ns:ns + nh, :]

    for gi, win in enumerate(POOL_WINDOWS):
        @pl.when(j == gi)
        def _(win=win):
            acc = n
            for k in range(1, win):
                acc = acc + nbuf[nh - k * nb:nh - k * nb + ns, :]
            sum_sc[...] = acc * (1.0 / win)

    p = sum_sc[...] - n
    y = _dot(p.astype(BF16), wp_ref[...].astype(BF16)) * sc_ref[...]
    o_ref[0:ns, :] = x[0:ns, :] + y
    o_ref[ns:, :] = x[ns:, :]


def _pool_s(h_s, state_pool, g, wp, scale):
    ns = N_SAMPLE_ROWS
    nb = N_SAMPLE_SEQ
    nh = POOL_HIST * nb
    pg = POOL_GROUP
    n_g = len(POOL_WINDOWS)
    state_spec = pl.BlockSpec((nh, pg), lambda j: (0, j))
    return pl.pallas_call(
        _pool_s_kernel,
        grid=(n_g,),
        in_specs=[
            _resident((S_ROWS, D_MODEL), lambda j: (0, 0)),
            pl.BlockSpec((S_ROWS, pg), lambda j: (0, j)),
            state_spec,
            pl.BlockSpec((1, pg), lambda j: (0, j)),
            pl.BlockSpec((None, None, pg, pg), lambda j: (0, j, 0, 0)),
            pl.BlockSpec((1, pg), lambda j: (0, j)),
        ],
        out_specs=[pl.BlockSpec((S_ROWS, pg), lambda j: (0, j)), state_spec],
        out_shape=[
            jax.ShapeDtypeStruct((S_ROWS, D_MODEL), F32),
            jax.ShapeDtypeStruct((nh, D_MODEL), F32),
        ],
        scratch_shapes=[
            pltpu.VMEM((S_ROWS, 1), F32),
            pltpu.VMEM((nh + ns, pg), F32),
            pltpu.VMEM((ns, pg), F32),
        ],
        compiler_params=_params(1),
        name="pool_s",
    )(h_s, h_s, state_pool, g, wp, scale)


BM_P = 1024
BM_GATE_UP = 2048
BM_POOL = 512
BM_CONV_OUT = 512
BN_P = 512
BN_MIX = 1024
BM_DOWN = 512
BK_DOWN = D_FF
BN_S = 256
BN_S_FF = 512
BK_S = 512


def kernel(x_prompt, x_sample, state_conv, state_pool, meta_tokens, norm_mix, norm_ffn, norm_final,
           conv_w_in, conv_w_dw, conv_w_out, pool_w, pool_scale, ffn_w_gate_up, ffn_w_down):
    d = D_MODEL
    nb, nt = N_SAMPLE_SEQ, SAMPLE_T
    batch = x_prompt.shape[0]

    h_p = x_prompt.reshape(batch * SEQ, d)
    hc = state_conv[0].transpose(1, 0, 2).reshape(CONV_HIST * nb, d)
    hp = state_pool[0].transpose(1, 0, 2).reshape(POOL_HIST * nb, d)
    row = lambda v: v.reshape(1, d)
    g_mix, g_ffn, g_fin = norm_mix, norm_ffn, row(norm_final)

    z_s, uts, umeta, wb, wc, wv, h_s = _mix_conv_s(
        x_sample, meta_tokens, row(g_mix[0]), conv_w_in, conv_w_dw[0], hc, BN_S)
    z_p, utail_p = _mix_conv_p(h_p, row(g_mix[0]), wb, wc, wv, conv_w_dw[0], umeta, BM_P, BN_MIX)
    h_s, w_out = _proj_res_s(z_s, conv_w_out, 0, h_s, g_fin, BK_S, None, "conv_out_s")
    h_p, m_p = _proj_res_p(z_p, w_out, h_p, row(g_ffn[0]), BM_CONV_OUT, D_MODEL, "emit", "conv_out_p")
    a_s, wg, wu = _gate_up_s(h_s, row(g_ffn[0]), ffn_w_gate_up, 0, BN_S_FF, "gate_up0_s")
    a_p = _gate_up_p(m_p, wg, wu, BM_GATE_UP, BN_P, "gate_up0_p")
    h_s, w_dn = _proj_res_s(a_s, ffn_w_down, 0, h_s, g_fin, BK_S, None, "down0_s")
    (h_p,) = _proj_res_p(a_p, w_dn, h_p, g_fin, BM_DOWN, BK_DOWN, None, "down0_p")
    h_p1, ntail_p, m_p = _pool_p(h_p, h_s, row(g_mix[1]), pool_w, row(pool_scale[0]), row(g_ffn[1]), BM_POOL)
    h_s1, pool_tail_s = _pool_s(h_s, hp, row(g_mix[1]), pool_w, row(pool_scale[0]))
    a_s, wg, wu = _gate_up_s(h_s1, row(g_ffn[1]), ffn_w_gate_up, 1, BN_S_FF, "gate_up1_s")
    a_p = _gate_up_p(m_p, wg, wu, BM_GATE_UP, BN_P, "gate_up1_p")
    y_s, w_dn = _proj_res_s(a_s, ffn_w_down, 1, h_s1, g_fin, BK_S, "final_sample", "down1_s")
    (y_p,) = _proj_res_p(a_p, w_dn, h_p1, g_fin, BM_DOWN, BK_DOWN, "final", "down1_p")

    y_prompt = y_p.reshape(batch, SEQ, d)
    y_sample = y_s
    tps = SEQ // BM_P
    new_conv_prompt = utail_p.reshape(batch, tps, TAIL, d)[:, tps - 1, TAIL - CONV_HIST:, :][None]
    tpp = SEQ // BM_POOL
    new_pool_prompt = ntail_p.reshape(batch, tpp, HALO, d)[:, tpp - 1, HALO - POOL_HIST:, :][None]
    new_conv_sample = uts.reshape(CONV_HIST, nb, d).transpose(1, 0, 2)[None]
    new_pool_sample = pool_tail_s.reshape(POOL_HIST, nb, d).transpose(1, 0, 2)[None]
    return (y_prompt, y_sample, new_conv_prompt, new_pool_prompt, new_conv_sample, new_pool_sample)
```

```python
import functools

import jax
import jax.numpy as jnp
from jax import lax
from jax.experimental import pallas as pl
from jax.experimental.pallas import tpu as pltpu

D_MODEL = 2048
D_FF = 5632
N_META = 16
SEQ = 2048
N_SAMPLE_SEQ = 128
SAMPLE_T = 8
CONV_HIST = 2
POOL_WINDOWS = (2, 4, 8, 16)
POOL_GROUP = D_MODEL // len(POOL_WINDOWS)
POOL_HIST = 15
EPS = 1e-6

N_SAMPLE_ROWS = N_SAMPLE_SEQ * SAMPLE_T
S_ROWS = N_SAMPLE_ROWS + 2 * N_META
META_ROW0 = N_SAMPLE_ROWS
TAIL = 8
HALO = 16
MXU_COLS = 256
FRONT = 8

V7X_SCOPED_VMEM_BYTES = 60000 * 1024

BF16 = jnp.bfloat16
F32 = jnp.float32


def _rms(x, g):
    ms = jnp.mean(x * x, axis=-1, keepdims=True)
    return (x * lax.rsqrt(ms + EPS)) * g


def _dot(a, b):
    return jnp.dot(a, b, preferred_element_type=F32)


def _params(n_axes):
    return pltpu.CompilerParams(
        dimension_semantics=("arbitrary",) * n_axes,
        vmem_limit_bytes=V7X_SCOPED_VMEM_BYTES,
    )


def _resident(shape, index_map):
    return pl.BlockSpec(shape, index_map, pipeline_mode=pl.Buffered(1))


def _conv3(w, u, ubuf, off):
    t = u.shape[0]
    return (w[2:3] * u + w[1:2] * ubuf[off - 1:off - 1 + t, :] + w[0:1] * ubuf[off - 2:off - 2 + t, :])


def _mix_p_kernel(tiles_per_seq, h_ref, g_ref, wb_ref, wc_ref, wv_ref, wdw_ref, umeta_ref,
                  z_ref, utail_ref, n_sc, ubuf, carry):
    i = pl.program_id(0)
    j = pl.program_id(1)
    bm = h_ref.shape[0]

    @pl.when(j == 0)
    def _():
        n_sc[...] = _rms(h_ref[...], g_ref[...]).astype(BF16)

    @pl.when(i % tiles_per_seq == 0)
    def _():
        ubuf[0:TAIL, :] = umeta_ref[...]

    @pl.when(i % tiles_per_seq != 0)
    def _():
        ubuf[0:TAIL, :] = carry[j]

    n = n_sc[...]
    for c0 in range(0, z_ref.shape[1], MXU_COLS):
        cols = slice(c0, c0 + MXU_COLS)
        b = _dot(n, wb_ref[:, cols])
        u = _dot(n, wc_ref[:, cols]) * _dot(n, wv_ref[:, cols])
        ubuf[TAIL:TAIL + bm, cols] = u
        z_ref[:, cols] = (b * _conv3(wdw_ref[:, cols], u, ubuf.at[:, cols], TAIL)).astype(BF16)
    tail = ubuf[bm:bm + TAIL, :]
    carry[j] = tail
    utail_ref[...] = tail


def _mix_s_kernel(xs_ref, meta_ref, g_ref, wb_ref, wc_ref, wv_ref, wdw_ref, hc_ref,
                  z_ref, uts_ref, umeta_ref, wb_o, wc_o, wv_o, h0_ref, n_sc, ubuf, mbuf):
    j = pl.program_id(0)
    ns = N_SAMPLE_ROWS
    nb = N_SAMPLE_SEQ
    nm = S_ROWS - ns

    @pl.when(j == 0)
    def _():
        for t in range(SAMPLE_T):
            h0_ref[t * nb:(t + 1) * nb, :] = xs_ref[:, t, :]
        h0_ref[ns:ns + N_META, :] = meta_ref[...]
        h0_ref[ns + N_META:, :] = jnp.zeros((nm - N_META, h0_ref.shape[1]), F32)
        n_sc[...] = _rms(h0_ref[...], g_ref[...]).astype(BF16)

    wb_o[...] = wb_ref[...].astype(BF16)
    wc_o[...] = wc_ref[...].astype(BF16)
    wv_o[...] = wv_ref[...].astype(BF16)
    n = n_sc[...]
    nh = CONV_HIST * nb
    ubuf[0:nh, :] = hc_ref[...]
    mbuf[0:TAIL, :] = jnp.zeros((TAIL, mbuf.shape[1]), F32)
    for c0 in range(0, z_ref.shape[1], MXU_COLS):
        cols = slice(c0, c0 + MXU_COLS)
        b = _dot(n, wb_o[:, cols])
        u = _dot(n, wc_o[:, cols]) * _dot(n, wv_o[:, cols])
        w = wdw_ref[:, cols]

        us = u[0:ns, :]
        ubuf[nh:nh + ns, cols] = us
        conv_s = (w[2:3] * us + w[1:2] * ubuf[nb:nb + ns, cols] + w[0:1] * ubuf[0:ns, cols])
        z_ref[0:ns, cols] = (b[0:ns, :] * conv_s).astype(BF16)

        um = u[ns:, :]
        mbuf[TAIL:TAIL + nm, cols] = um
        z_ref[ns:, cols] = (b[ns:, :] * _conv3(w, um, mbuf.at[:, cols], TAIL)).astype(BF16)
    uts_ref[...] = ubuf[ns:ns + nh, :]
    umeta_ref[...] = mbuf[N_META:N_META + TAIL, :]


def _mix_conv_p(h, g, wb, wc, wv, w_dw, umeta, bm, bn):
    rows = h.shape[0]
    n_i, n_j = rows // bm, D_MODEL // bn
    wspec = pl.BlockSpec((D_MODEL, bn), lambda i, j: (0, j))
    return pl.pallas_call(
        functools.partial(_mix_p_kernel, SEQ // bm),
        grid=(n_i, n_j),
        in_specs=[
            pl.BlockSpec((bm, D_MODEL), lambda i, j: (i, 0)),
            pl.BlockSpec((1, D_MODEL), lambda i, j: (0, 0)),
            wspec, wspec, wspec,
            pl.BlockSpec((3, bn), lambda i, j: (0, j)),
            pl.BlockSpec((TAIL, bn), lambda i, j: (0, j)),
        ],
        out_specs=[
            pl.BlockSpec((bm, bn), lambda i, j: (i, j)),
            pl.BlockSpec((TAIL, bn), lambda i, j: (i, j)),
        ],
        out_shape=[
            jax.ShapeDtypeStruct((rows, D_MODEL), BF16),
            jax.ShapeDtypeStruct((n_i * TAIL, D_MODEL), F32),
        ],
        scratch_shapes=[
            pltpu.VMEM((bm, D_MODEL), BF16),
            pltpu.VMEM((TAIL + bm, bn), F32),
            pltpu.VMEM((n_j, TAIL, bn), F32),
        ],
        compiler_params=_params(2),
        name="mix_conv_p",
    )(h, g, wb, wc, wv, w_dw, umeta)


def _mix_conv_s(x_sample, meta, g, w_in, w_dw, hc, bn):
    n_j = D_MODEL // bn
    nb = N_SAMPLE_SEQ
    nm = S_ROWS - N_SAMPLE_ROWS
    wspec = lambda part: pl.BlockSpec((None, D_MODEL, bn), lambda j: (0, 0, part * n_j + j))
    wout = pl.BlockSpec((D_MODEL, bn), lambda j: (0, j))
    wshape = jax.ShapeDtypeStruct((D_MODEL, D_MODEL), BF16)
    return pl.pallas_call(
        _mix_s_kernel,
        grid=(n_j,),
        in_specs=[
            _resident((nb, SAMPLE_T, D_MODEL), lambda j: (0, 0, 0)),
            _resident((N_META, D_MODEL), lambda j: (0, 0)),
            pl.BlockSpec((1, D_MODEL), lambda j: (0, 0)),
            wspec(0), wspec(1), wspec(2),
            pl.BlockSpec((3, bn), lambda j: (0, j)),
            pl.BlockSpec((CONV_HIST * nb, bn), lambda j: (0, j)),
        ],
        out_specs=[
            pl.BlockSpec((S_ROWS, bn), lambda j: (0, j)),
            pl.BlockSpec((CONV_HIST * nb, bn), lambda j: (0, j)),
            pl.BlockSpec((TAIL, bn), lambda j: (0, j)),
            wout, wout, wout,
            pl.BlockSpec((S_ROWS, D_MODEL), lambda j: (0, 0)),
        ],
        out_shape=[
            jax.ShapeDtypeStruct((S_ROWS, D_MODEL), BF16),
            jax.ShapeDtypeStruct((CONV_HIST * nb, D_MODEL), F32),
            jax.ShapeDtypeStruct((TAIL, D_MODEL), F32),
            wshape, wshape, wshape,
            jax.ShapeDtypeStruct((S_ROWS, D_MODEL), F32),
        ],
        scratch_shapes=[
            pltpu.VMEM((S_ROWS, D_MODEL), BF16),
            pltpu.VMEM((CONV_HIST * nb + N_SAMPLE_ROWS, bn), F32),
            pltpu.VMEM((TAIL + nm, bn), F32),
        ],
        compiler_params=_params(1),
        name="mix_conv_s",
    )(x_sample, meta, g, w_in, w_in, w_in, w_dw, hc)


def _proj_res_body(norm, k, n_k, x_ref, w_ref, h_ref, g_ref, o_ref, m_ref):
    if n_k == 1:
        acc = h_ref[...] + _dot(x_ref[...], w_ref[...])
        o_ref[...] = _rms(acc, g_ref[...]) if norm == "final" else acc
        if norm == "emit":
            m_ref[...] = _rms(acc, g_ref[...]).astype(BF16)
        return

    @pl.when(k == 0)
    def _():
        o_ref[...] = h_ref[...] + _dot(x_ref[...], w_ref[...])

    @pl.when(k != 0)
    def _():
        o_ref[...] += _dot(x_ref[...], w_ref[...])

    if norm is not None:
        @pl.when(k == n_k - 1)
        def _():
            y = _rms(o_ref[...], g_ref[...])
            if norm == "final":
                o_ref[...] = y
            else:
                m_ref[...] = y.astype(BF16)


def _proj_res_p_kernel(norm, n_k, x_ref, w_ref, h_ref, g_ref, o_ref, m_ref=None):
    _proj_res_body(norm, pl.program_id(1), n_k, x_ref, w_ref, h_ref, g_ref, o_ref, m_ref)


def _proj_res_s_kernel(norm, n_k, x_ref, w_ref, h_ref, g_ref, o_ref, wo_ref, acc=None):
    k = pl.program_id(0)
    wo_ref[...] = w_ref[...].astype(BF16)
    if norm != "final_sample":
        _proj_res_body(norm, k, n_k, x_ref, wo_ref, h_ref, g_ref, o_ref, None)
        return
    _proj_res_body(None, k, n_k, x_ref, wo_ref, h_ref, g_ref, acc, None)

    @pl.when(k == n_k - 1)
    def _():
        y = _rms(acc[0:N_SAMPLE_ROWS, :], g_ref[...])
        for t in range(SAMPLE_T):
            o_ref[:, t, :] = y[t * N_SAMPLE_SEQ:(t + 1) * N_SAMPLE_SEQ, :]


def _proj_res_p(x, w, h, g, bm, bk, norm, name):
    rows, kdim = x.shape
    n_i, n_k = rows // bm, kdim // bk
    wspec = _resident if n_k == 1 else pl.BlockSpec
    row_spec = pl.BlockSpec((bm, D_MODEL), lambda i, k: (i, 0))
    out_specs, out_shape = [row_spec], [jax.ShapeDtypeStruct((rows, D_MODEL), F32)]
    if norm == "emit":
        out_specs.append(row_spec)
        out_shape.append(jax.ShapeDtypeStruct((rows, D_MODEL), BF16))
    return pl.pallas_call(
        functools.partial(_proj_res_p_kernel, norm, n_k),
        grid=(n_i, n_k),
        in_specs=[
            pl.BlockSpec((bm, bk), lambda i, k: (i, k)),
            wspec((bk, D_MODEL), lambda i, k: (k, 0)),
            row_spec,
            pl.BlockSpec((1, D_MODEL), lambda i, k: (0, 0)),
        ],
        out_specs=out_specs,
        out_shape=out_shape,
        compiler_params=_params(2),
        name=name,
    )(x, w, h, g)


def _proj_res_s(x, w_stack, layer, h, g, bk, norm, name):
    assert norm in (None, "final_sample")
    rows, kdim = x.shape
    n_k = kdim // bk
    if norm == "final_sample":
        o_shape = (N_SAMPLE_SEQ, SAMPLE_T, D_MODEL)
        scratch = [pltpu.VMEM((rows, D_MODEL), F32)]
    else:
        o_shape = (rows, D_MODEL)
        scratch = []
    o_index = (0,) * len(o_shape)
    return pl.pallas_call(
        functools.partial(_proj_res_s_kernel, norm, n_k),
        grid=(n_k,),
        in_specs=[
            pl.BlockSpec((rows, bk), lambda k: (0, k)),
            pl.BlockSpec((None, bk, D_MODEL), lambda k: (layer, k, 0)),
            _resident((rows, D_MODEL), lambda k: (0, 0)),
            pl.BlockSpec((1, D_MODEL), lambda k: (0, 0)),
        ],
        out_specs=[
            pl.BlockSpec(o_shape, lambda k: o_index),
            pl.BlockSpec((bk, D_MODEL), lambda k: (k, 0)),
        ],
        out_shape=[
            jax.ShapeDtypeStruct(o_shape, F32),
            jax.ShapeDtypeStruct((kdim, D_MODEL), BF16),
        ],
        scratch_shapes=scratch,
        compiler_params=_params(1),
        name=name,
    )(x, w_stack, h, g)


def _swiglu_chunks(m, wg_ref, wu_ref, a_ref, wg_src=None, wu_src=None, n_cols=None):
    for c0 in range(0, a_ref.shape[1] if n_cols is None else n_cols, MXU_COLS):
        cols = slice(c0, c0 + MXU_COLS)
        if wg_src is not None:
            wg_ref[:, cols] = wg_src[:, cols].astype(BF16)
            wu_ref[:, cols] = wu_src[:, cols].astype(BF16)
        gate = _dot(m, wg_ref[:, cols])
        up = _dot(m, wu_ref[:, cols])
        a_ref[:, cols] = (gate * (1.0 / (1.0 + jnp.exp(-gate))) * up).astype(BF16)


def _gate_up_p_kernel(n_full, tail_cols, m_ref, wg_ref, wu_ref, a_ref):
    if tail_cols == 0:
        _swiglu_chunks(m_ref[...], wg_ref, wu_ref, a_ref)
        return
    j = pl.program_id(1)

    @pl.when(j < n_full)
    def _():
        _swiglu_chunks(m_ref[...], wg_ref, wu_ref, a_ref)

    @pl.when(j == n_full)
    def _():
        _swiglu_chunks(m_ref[...], wg_ref, wu_ref, a_ref, n_cols=tail_cols)


def _gate_up_s_kernel(h_ref, g_ref, wg_ref, wu_ref, a_ref, wg_o, wu_o, m_sc):
    @pl.when(pl.program_id(0) == 0)
    def _():
        m_sc[...] = _rms(h_ref[...], g_ref[...]).astype(BF16)

    _swiglu_chunks(m_sc[...], wg_o, wu_o, a_ref, wg_ref, wu_ref)


def _gate_up_p(m, wg, wu, bm, bn, name):
    rows = m.shape[0]
    n_i = rows // bm
    n_full, tail_cols = divmod(D_FF, bn)
    assert tail_cols % MXU_COLS == 0
    n_j = n_full + (tail_cols > 0)
    wspec = pl.BlockSpec((D_MODEL, bn), lambda i, j: (0, j))
    return pl.pallas_call(
        functools.partial(_gate_up_p_kernel, n_full, tail_cols),
        grid=(n_i, n_j),
        in_specs=[pl.BlockSpec((bm, D_MODEL), lambda i, j: (i, 0)), wspec, wspec],
        out_specs=pl.BlockSpec((bm, bn), lambda i, j: (i, j)),
        out_shape=jax.ShapeDtypeStruct((rows, D_FF), BF16),
        compiler_params=_params(2),
        name=name,
    )(m, wg, wu)


def _gate_up_s(h, g, w_gu, layer, bn, name):
    rows = h.shape[0]
    n_j = D_FF // bn
    wout = pl.BlockSpec((D_MODEL, bn), lambda j: (0, j))
    wshape = jax.ShapeDtypeStruct((D_MODEL, D_FF), BF16)
    return pl.pallas_call(
        _gate_up_s_kernel,
        grid=(n_j,),
        in_specs=[
            _resident((rows, D_MODEL), lambda j: (0, 0)),
            pl.BlockSpec((1, D_MODEL), lambda j: (0, 0)),
            pl.BlockSpec((None, D_MODEL, bn), lambda j: (layer, 0, j)),
            pl.BlockSpec((None, D_MODEL, bn), lambda j: (layer, 0, n_j + j)),
        ],
        out_specs=[pl.BlockSpec((rows, bn), lambda j: (0, j)), wout, wout],
        out_shape=[jax.ShapeDtypeStruct((rows, D_FF), BF16), wshape, wshape],
        scratch_shapes=[pltpu.VMEM((rows, D_MODEL), BF16)],
        compiler_params=_params(1),
        name=name,
    )(h, g, w_gu, w_gu)


def _pool_p_kernel(tiles_per_seq, h_ref, halo_p_ref, halo_s_ref, g_ref, wp_ref, sc_ref, gnext_ref,
                   o_ref, ntail_ref, m_ref, nbuf, pbuf, qbuf):
    i = pl.program_id(0)
    bm = h_ref.shape[0]
    g = g_ref[...]
    x = h_ref[...]
    n = _rms(x, g)
    rows = HALO + bm
    r0 = FRONT + HALO

    for buf in (nbuf, pbuf, qbuf):
        buf[0:FRONT, :] = jnp.zeros((FRONT, buf.shape[1]), F32)

    @pl.when(i % tiles_per_seq == 0)
    def _():
        nbuf[FRONT:r0, :] = _rms(halo_s_ref[...], g)

    @pl.when(i % tiles_per_seq != 0)
    def _():
        nbuf[FRONT:r0, :] = _rms(halo_p_ref[...], g)

    nbuf[r0:r0 + bm, :] = n
    ntail_ref[...] = n[bm - HALO:, :]
    for gi, win in enumerate(POOL_WINDOWS):
        cols = slice(gi * POOL_GROUP, (gi + 1) * POOL_GROUP)
        src, shift, level = nbuf, 1, 0
        while shift < win:
            dst = (pbuf, qbuf)[level % 2]
            dst[FRONT:FRONT + rows, cols] = (src[FRONT:FRONT + rows, cols]
                                             + src[FRONT - shift:FRONT - shift + rows, cols])
            src, shift, level = dst, 2 * shift, level + 1
        ng = n[:, cols]
        p = src[r0:r0 + bm, cols] * (1.0 / win) - ng
        y = _dot(p.astype(BF16), wp_ref[gi].astype(BF16)) * sc_ref[:, cols]
        o_ref[:, cols] = x[:, cols] + y
    m_ref[...] = _rms(o_ref[...], gnext_ref[...]).astype(BF16)


def _pool_p(h_p, h_s, g, wp, scale, g_next, bm):
    rows = h_p.shape[0]
    n_i = rows // bm
    ng = len(POOL_WINDOWS)
    return pl.pallas_call(
        functools.partial(_pool_p_kernel, SEQ // bm),
        grid=(n_i,),
        in_specs=[
            pl.BlockSpec((bm, D_MODEL), lambda i: (i, 0)),
            pl.BlockSpec((HALO, D_MODEL), lambda i: (jnp.maximum(i * (bm // HALO) - 1, 0), 0)),
            pl.BlockSpec((HALO, D_MODEL), lambda i: (META_ROW0 // HALO, 0)),
            pl.BlockSpec((1, D_MODEL), lambda i: (0, 0)),
            pl.BlockSpec((None, ng, POOL_GROUP, POOL_GROUP), lambda i: (0, 0, 0, 0)),
            pl.BlockSpec((1, D_MODEL), lambda i: (0, 0)),
            pl.BlockSpec((1, D_MODEL), lambda i: (0, 0)),
        ],
        out_specs=[
            pl.BlockSpec((bm, D_MODEL), lambda i: (i, 0)),
            pl.BlockSpec((HALO, D_MODEL), lambda i: (i, 0)),
            pl.BlockSpec((bm, D_MODEL), lambda i: (i, 0)),
        ],
        out_shape=[
            jax.ShapeDtypeStruct((rows, D_MODEL), F32),
            jax.ShapeDtypeStruct((n_i * HALO, D_MODEL), F32),
            jax.ShapeDtypeStruct((rows, D_MODEL), BF16),
        ],
        scratch_shapes=[pltpu.VMEM((FRONT + HALO + bm, D_MODEL), F32)] * 3,
        compiler_params=_params(1),
        name="pool_p",
    )(h_p, h_p, h_s, g, wp, scale, g_next)


def _pool_s_kernel(hfull_ref, hcol_ref, hp_ref, g_ref, wp_ref, sc_ref, o_ref, tail_ref, inv_sc, nbuf, sum_sc):
    j = pl.program_id(0)
    ns = N_SAMPLE_ROWS
    nb = N_SAMPLE_SEQ
    nh = POOL_HIST * nb

    @pl.when(j == 0)
    def _():
        xf = hfull_ref[...]
        inv_sc[...] = lax.rsqrt(jnp.mean(xf * xf, axis=-1, keepdims=True) + EPS)

    x = hcol_ref[...]
    n = ((x * inv_sc[...]) * g_ref[...])[0:ns, :]
    nbuf[0:nh, :] = hp_ref[...]
    nbuf[nh:nh + ns, :] = n
    tail_ref[...] = nbuf[ns:ns + nh, :]

    for gi, win in enumerate(POOL_WINDOWS):
        @pl.when(j == gi)
        def _(win=win):
            acc = n
            for k in range(1, win):
                acc = acc + nbuf[nh - k * nb:nh - k * nb + ns, :]
            sum_sc[...] = acc * (1.0 / win)

    p = sum_sc[...] - n
    y = _dot(p.astype(BF16), wp_ref[...].astype(BF16)) * sc_ref[...]
    o_ref[0:ns, :] = x[0:ns, :] + y
    o_ref[ns:, :] = x[ns:, :]


def _pool_s(h_s, state_pool, g, wp, scale):
    ns = N_SAMPLE_ROWS
    nb = N_SAMPLE_SEQ
    nh = POOL_HIST * nb
    pg = POOL_GROUP
    n_g = len(POOL_WINDOWS)
    state_spec = pl.BlockSpec((nh, pg), lambda j: (0, j))
    return pl.pallas_call(
        _pool_s_kernel,
        grid=(n_g,),
        in_specs=[
            _resident((S_ROWS, D_MODEL), lambda j: (0, 0)),
            pl.BlockSpec((S_ROWS, pg), lambda j: (0, j)),
            state_spec,
            pl.BlockSpec((1, pg), lambda j: (0, j)),
            pl.BlockSpec((None, None, pg, pg), lambda j: (0, j, 0, 0)),
            pl.BlockSpec((1, pg), lambda j: (0, j)),
        ],
        out_specs=[pl.BlockSpec((S_ROWS, pg), lambda j: (0, j)), state_spec],
        out_shape=[
            jax.ShapeDtypeStruct((S_ROWS, D_MODEL), F32),
            jax.ShapeDtypeStruct((nh, D_MODEL), F32),
        ],
        scratch_shapes=[
            pltpu.VMEM((S_ROWS, 1), F32),
            pltpu.VMEM((nh + ns, pg), F32),
            pltpu.VMEM((ns, pg), F32),
        ],
        compiler_params=_params(1),
        name="pool_s",
    )(h_s, h_s, state_pool, g, wp, scale)


BM_P = 1024
BM_GATE_UP = 2048
BM_POOL = 512
BM_CONV_OUT = 512
BN_P = 1024
BN_MIX = 512
BM_DOWN = 512
BK_DOWN = D_FF
BN_S = 256
BN_S_FF = 512
BK_S = 512


def kernel(x_prompt, x_sample, state_conv, state_pool, meta_tokens, norm_mix, norm_ffn, norm_final,
           conv_w_in, conv_w_dw, conv_w_out, pool_w, pool_scale, ffn_w_gate_up, ffn_w_down):
    d = D_MODEL
    nb, nt = N_SAMPLE_SEQ, SAMPLE_T
    batch = x_prompt.shape[0]

    h_p = x_prompt.reshape(batch * SEQ, d)
    hc = state_conv[0].transpose(1, 0, 2).reshape(CONV_HIST * nb, d)
    hp = state_pool[0].transpose(1, 0, 2).reshape(POOL_HIST * nb, d)
    row = lambda v: v.reshape(1, d)
    g_mix, g_ffn, g_fin = norm_mix, norm_ffn, row(norm_final)

    z_s, uts, umeta, wb, wc, wv, h_s = _mix_conv_s(
        x_sample, meta_tokens, row(g_mix[0]), conv_w_in, conv_w_dw[0], hc, BN_S)
    z_p, utail_p = _mix_conv_p(h_p, row(g_mix[0]), wb, wc, wv, conv_w_dw[0], umeta, BM_P, BN_MIX)
    h_s, w_out = _proj_res_s(z_s, conv_w_out, 0, h_s, g_fin, BK_S, None, "conv_out_s")
    h_p, m_p = _proj_res_p(z_p, w_out, h_p, row(g_ffn[0]), BM_CONV_OUT, D_MODEL, "emit", "conv_out_p")
    a_s, wg, wu = _gate_up_s(h_s, row(g_ffn[0]), ffn_w_gate_up, 0, BN_S_FF, "gate_up0_s")
    a_p = _gate_up_p(m_p, wg, wu, BM_GATE_UP, BN_P, "gate_up0_p")
    h_s, w_dn = _proj_res_s(a_s, ffn_w_down, 0, h_s, g_fin, BK_S, None, "down0_s")
    (h_p,) = _proj_res_p(a_p, w_dn, h_p, g_fin, BM_DOWN, BK_DOWN, None, "down0_p")
    h_p1, ntail_p, m_p = _pool_p(h_p, h_s, row(g_mix[1]), pool_w, row(pool_scale[0]), row(g_ffn[1]), BM_POOL)
    h_s1, pool_tail_s = _pool_s(h_s, hp, row(g_mix[1]), pool_w, row(pool_scale[0]))
    a_s, wg, wu = _gate_up_s(h_s1, row(g_ffn[1]), ffn_w_gate_up, 1, BN_S_FF, "gate_up1_s")
    a_p = _gate_up_p(m_p, wg, wu, BM_GATE_UP, BN_P, "gate_up1_p")
    y_s, w_dn = _proj_res_s(a_s, ffn_w_down, 1, h_s1, g_fin, BK_S, "final_sample", "down1_s")
    (y_p,) = _proj_res_p(a_p, w_dn, h_p1, g_fin, BM_DOWN, BK_DOWN, "final", "down1_p")

    y_prompt = y_p.reshape(batch, SEQ, d)
    y_sample = y_s
    tps = SEQ // BM_P
    new_conv_prompt = utail_p.reshape(batch, tps, TAIL, d)[:, tps - 1, TAIL - CONV_HIST:, :][None]
    tpp = SEQ // BM_POOL
    new_pool_prompt = ntail_p.reshape(batch, tpp, HALO, d)[:, tpp - 1, HALO - POOL_HIST:, :][None]
    new_conv_sample = uts.reshape(CONV_HIST, nb, d).transpose(1, 0, 2)[None]
    new_pool_sample = pool_tail_s.reshape(POOL_HIST, nb, d).transpose(1, 0, 2)[None]
    return (y_prompt, y_sample, new_conv_prompt, new_pool_prompt, new_conv_sample, new_pool_sample)
```

```python
import functools
from typing import NamedTuple

import jax
import jax.numpy as jnp
from jax import lax
from jax.experimental import pallas as pl
from jax.experimental.pallas import tpu as pltpu

D_MODEL = 2048
D_FF = 5632
N_META = 16
SEQ = 2048
N_SAMPLE_SEQ = 128
SAMPLE_T = 8
CONV_HIST = 2
POOL_WINDOWS = (2, 4, 8, 16)
POOL_GROUP = D_MODEL // len(POOL_WINDOWS)
POOL_HIST = 15
EPS = 1e-6

N_SAMPLE_ROWS = N_SAMPLE_SEQ * SAMPLE_T
S_ROWS = N_SAMPLE_ROWS + 2 * N_META
META_ROW0 = N_SAMPLE_ROWS
TAIL = 8
HALO = 16
MXU_COLS = 256
FRONT = 8

V7X_SCOPED_VMEM_BYTES = 60000 * 1024

BF16 = jnp.bfloat16
F32 = jnp.float32


def _rms(x, g):
    ms = jnp.mean(x * x, axis=-1, keepdims=True)
    return (x * lax.rsqrt(ms + EPS)) * g


def _dot(a, b):
    return jnp.dot(a, b, preferred_element_type=F32)


def _params(n_axes):
    return pltpu.CompilerParams(
        dimension_semantics=("arbitrary",) * n_axes,
        vmem_limit_bytes=V7X_SCOPED_VMEM_BYTES,
    )


def _resident(shape, index_map):
    return pl.BlockSpec(shape, index_map, pipeline_mode=pl.Buffered(1))


class _CastJob(NamedTuple):
    src: jax.Array
    in_spec: pl.BlockSpec
    out_spec: pl.BlockSpec
    out_shape: jax.ShapeDtypeStruct


def _cast_job(w_stack, layer, n_steps, axis, step_of):
    _, r, c = w_stack.shape
    if axis == 0:
        blk = (r // n_steps, c)
        in_idx = lambda *ids: (layer, step_of(*ids), 0)
        out_idx = lambda *ids: (step_of(*ids), 0)
    else:
        blk = (r, c // n_steps)
        in_idx = lambda *ids: (layer, 0, step_of(*ids))
        out_idx = lambda *ids: (0, step_of(*ids))
    assert blk[0] * (n_steps if axis == 0 else 1) == r and blk[1] * (n_steps if axis == 1 else 1) == c
    return _CastJob(w_stack, pl.BlockSpec((None,) + blk, in_idx), pl.BlockSpec(blk, out_idx),
                    jax.ShapeDtypeStruct((r, c), BF16))


def _run_casts(src_refs, dst_refs):
    for src, dst in zip(src_refs, dst_refs):
        dst[...] = src[...].astype(BF16)


def _conv3(w, u, ubuf, off):
    t = u.shape[0]
    return (w[2:3] * u + w[1:2] * ubuf[off - 1:off - 1 + t, :] + w[0:1] * ubuf[off - 2:off - 2 + t, :])


def _mix_p_kernel(tiles_per_seq, n_cast, *refs):
    h_ref, g_ref, wb_ref, wc_ref, wv_ref, wdw_ref, umeta_ref = refs[:7]
    cast_src, refs = refs[7:7 + n_cast], refs[7 + n_cast:]
    z_ref, utail_ref = refs[:2]
    cast_dst, (n_sc, ubuf, carry) = refs[2:2 + n_cast], refs[2 + n_cast:]
    i = pl.program_id(0)
    j = pl.program_id(1)
    bm = h_ref.shape[0]
    _run_casts(cast_src, cast_dst)

    @pl.when(j == 0)
    def _():
        n_sc[...] = _rms(h_ref[...], g_ref[...]).astype(BF16)

    @pl.when(i % tiles_per_seq == 0)
    def _():
        ubuf[0:TAIL, :] = umeta_ref[...]

    @pl.when(i % tiles_per_seq != 0)
    def _():
        ubuf[0:TAIL, :] = carry[j]

    n = n_sc[...]
    for c0 in range(0, z_ref.shape[1], MXU_COLS):
        cols = slice(c0, c0 + MXU_COLS)
        b = _dot(n, wb_ref[:, cols])
        u = _dot(n, wc_ref[:, cols]) * _dot(n, wv_ref[:, cols])
        ubuf[TAIL:TAIL + bm, cols] = u
        z_ref[:, cols] = (b * _conv3(wdw_ref[:, cols], u, ubuf.at[:, cols], TAIL)).astype(BF16)
    tail = ubuf[bm:bm + TAIL, :]
    carry[j] = tail
    utail_ref[...] = tail


def _mix_s_kernel(xs_ref, meta_ref, g_ref, wb_ref, wc_ref, wv_ref, wdw_ref, hc_ref,
                  z_ref, uts_ref, umeta_ref, wb_o, wc_o, wv_o, h0_ref, n_sc, ubuf, mbuf):
    j = pl.program_id(0)
    ns = N_SAMPLE_ROWS
    nb = N_SAMPLE_SEQ
    nm = S_ROWS - ns

    @pl.when(j == 0)
    def _():
        for t in range(SAMPLE_T):
            h0_ref[t * nb:(t + 1) * nb, :] = xs_ref[:, t, :]
        h0_ref[ns:ns + N_META, :] = meta_ref[...]
        h0_ref[ns + N_META:, :] = jnp.zeros((nm - N_META, h0_ref.shape[1]), F32)
        n_sc[...] = _rms(h0_ref[...], g_ref[...]).astype(BF16)

    wb_o[...] = wb_ref[...].astype(BF16)
    wc_o[...] = wc_ref[...].astype(BF16)
    wv_o[...] = wv_ref[...].astype(BF16)
    n = n_sc[...]
    nh = CONV_HIST * nb
    ubuf[0:nh, :] = hc_ref[...]
    mbuf[0:TAIL, :] = jnp.zeros((TAIL, mbuf.shape[1]), F32)
    for c0 in range(0, z_ref.shape[1], MXU_COLS):
        cols = slice(c0, c0 + MXU_COLS)
        b = _dot(n, wb_o[:, cols])
        u = _dot(n, wc_o[:, cols]) * _dot(n, wv_o[:, cols])
        w = wdw_ref[:, cols]

        us = u[0:ns, :]
        ubuf[nh:nh + ns, cols] = us
        conv_s = (w[2:3] * us + w[1:2] * ubuf[nb:nb + ns, cols] + w[0:1] * ubuf[0:ns, cols])
        z_ref[0:ns, cols] = (b[0:ns, :] * conv_s).astype(BF16)

        um = u[ns:, :]
        mbuf[TAIL:TAIL + nm, cols] = um
        z_ref[ns:, cols] = (b[ns:, :] * _conv3(w, um, mbuf.at[:, cols], TAIL)).astype(BF16)
    uts_ref[...] = ubuf[ns:ns + nh, :]
    umeta_ref[...] = mbuf[N_META:N_META + TAIL, :]


def _mix_conv_p(h, g, wb, wc, wv, w_dw, umeta, bm, bn, casts):
    rows = h.shape[0]
    n_i, n_j = rows // bm, D_MODEL // bn
    wspec = pl.BlockSpec((D_MODEL, bn), lambda i, j: (0, j))
    return pl.pallas_call(
        functools.partial(_mix_p_kernel, SEQ // bm, len(casts)),
        grid=(n_i, n_j),
        in_specs=[
            pl.BlockSpec((bm, D_MODEL), lambda i, j: (i, 0)),
            pl.BlockSpec((1, D_MODEL), lambda i, j: (0, 0)),
            wspec, wspec, wspec,
            pl.BlockSpec((3, bn), lambda i, j: (0, j)),
            pl.BlockSpec((TAIL, bn), lambda i, j: (0, j)),
        ] + [c.in_spec for c in casts],
        out_specs=[
            pl.BlockSpec((bm, bn), lambda i, j: (i, j)),
            pl.BlockSpec((TAIL, bn), lambda i, j: (i, j)),
        ] + [c.out_spec for c in casts],
        out_shape=[
            jax.ShapeDtypeStruct((rows, D_MODEL), BF16),
            jax.ShapeDtypeStruct((n_i * TAIL, D_MODEL), F32),
        ] + [c.out_shape for c in casts],
        scratch_shapes=[
            pltpu.VMEM((bm, D_MODEL), BF16),
            pltpu.VMEM((TAIL + bm, bn), F32),
            pltpu.VMEM((n_j, TAIL, bn), F32),
        ],
        compiler_params=_params(2),
        name="mix_conv_p",
    )(h, g, wb, wc, wv, w_dw, umeta, *[c.src for c in casts])


def _mix_conv_s(x_sample, meta, g, w_in, w_dw, hc, bn):
    n_j = D_MODEL // bn
    nb = N_SAMPLE_SEQ
    nm = S_ROWS - N_SAMPLE_ROWS
    wspec = lambda part: pl.BlockSpec((None, D_MODEL, bn), lambda j: (0, 0, part * n_j + j))
    wout = pl.BlockSpec((D_MODEL, bn), lambda j: (0, j))
    wshape = jax.ShapeDtypeStruct((D_MODEL, D_MODEL), BF16)
    return pl.pallas_call(
        _mix_s_kernel,
        grid=(n_j,),
        in_specs=[
            _resident((nb, SAMPLE_T, D_MODEL), lambda j: (0, 0, 0)),
            _resident((N_META, D_MODEL), lambda j: (0, 0)),
            pl.BlockSpec((1, D_MODEL), lambda j: (0, 0)),
            wspec(0), wspec(1), wspec(2),
            pl.BlockSpec((3, bn), lambda j: (0, j)),
            pl.BlockSpec((CONV_HIST * nb, bn), lambda j: (0, j)),
        ],
        out_specs=[
            pl.BlockSpec((S_ROWS, bn), lambda j: (0, j)),
            pl.BlockSpec((CONV_HIST * nb, bn), lambda j: (0, j)),
            pl.BlockSpec((TAIL, bn), lambda j: (0, j)),
            wout, wout, wout,
            pl.BlockSpec((S_ROWS, D_MODEL), lambda j: (0, 0)),
        ],
        out_shape=[
            jax.ShapeDtypeStruct((S_ROWS, D_MODEL), BF16),
            jax.ShapeDtypeStruct((CONV_HIST * nb, D_MODEL), F32),
            jax.ShapeDtypeStruct((TAIL, D_MODEL), F32),
            wshape, wshape, wshape,
            jax.ShapeDtypeStruct((S_ROWS, D_MODEL), F32),
        ],
        scratch_shapes=[
            pltpu.VMEM((S_ROWS, D_MODEL), BF16),
            pltpu.VMEM((CONV_HIST * nb + N_SAMPLE_ROWS, bn), F32),
            pltpu.VMEM((TAIL + nm, bn), F32),
        ],
        compiler_params=_params(1),
        name="mix_conv_s",
    )(x_sample, meta, g, w_in, w_in, w_in, w_dw, hc)


def _proj_res_body(norm, k, n_k, x_ref, w_ref, h_ref, g_ref, o_ref, m_ref):
    if n_k == 1:
        acc = h_ref[...] + _dot(x_ref[...], w_ref[...])
        o_ref[...] = _rms(acc, g_ref[...]) if norm == "final" else acc
        if norm == "emit":
            m_ref[...] = _rms(acc, g_ref[...]).astype(BF16)
        return

    @pl.when(k == 0)
    def _():
        o_ref[...] = h_ref[...] + _dot(x_ref[...], w_ref[...])

    @pl.when(k != 0)
    def _():
        o_ref[...] += _dot(x_ref[...], w_ref[...])

    if norm is not None:
        @pl.when(k == n_k - 1)
        def _():
            y = _rms(o_ref[...], g_ref[...])
            if norm == "final":
                o_ref[...] = y
            else:
                m_ref[...] = y.astype(BF16)


def _proj_res_p_kernel(norm, n_k, x_ref, w_ref, h_ref, g_ref, o_ref, m_ref=None):
    _proj_res_body(norm, pl.program_id(1), n_k, x_ref, w_ref, h_ref, g_ref, o_ref, m_ref)


def _proj_res_p(x, w, h, g, bm, bk, norm, name):
    rows, kdim = x.shape
    n_i, n_k = rows // bm, kdim // bk
    wspec = _resident if n_k == 1 else pl.BlockSpec
    row_spec = pl.BlockSpec((bm, D_MODEL), lambda i, k: (i, 0))
    out_specs, out_shape = [row_spec], [jax.ShapeDtypeStruct((rows, D_MODEL), F32)]
    if norm == "emit":
        out_specs.append(row_spec)
        out_shape.append(jax.ShapeDtypeStruct((rows, D_MODEL), BF16))
    return pl.pallas_call(
        functools.partial(_proj_res_p_kernel, norm, n_k),
        grid=(n_i, n_k),
        in_specs=[
            pl.BlockSpec((bm, bk), lambda i, k: (i, k)),
            wspec((bk, D_MODEL), lambda i, k: (k, 0)),
            row_spec,
            pl.BlockSpec((1, D_MODEL), lambda i, k: (0, 0)),
        ],
        out_specs=out_specs,
        out_shape=out_shape,
        compiler_params=_params(2),
        name=name,
    )(x, w, h, g)


def _swiglu_chunks(m, wg_ref, wu_ref, a_ref):
    for c0 in range(0, a_ref.shape[1], MXU_COLS):
        cols = slice(c0, c0 + MXU_COLS)
        gate = _dot(m, wg_ref[:, cols])
        up = _dot(m, wu_ref[:, cols])
        a_ref[:, cols] = (gate * (1.0 / (1.0 + jnp.exp(-gate))) * up).astype(BF16)


def _gate_up_p_kernel(n_cast, m_ref, wg_ref, wu_ref, *refs):
    _run_casts(refs[:n_cast], refs[n_cast + 1:])
    _swiglu_chunks(m_ref[...], wg_ref, wu_ref, refs[n_cast])


def _ffn_s_kernel(sample_out, n_j, h_ref, g_ref, gfin_ref, wg_ref, wu_ref, wd_ref, o_ref, m_sc, a_sc, acc=None):
    j = pl.program_id(0)

    @pl.when(j == 0)
    def _():
        m_sc[...] = _rms(h_ref[...], g_ref[...]).astype(BF16)

    _swiglu_chunks(m_sc[...], wg_ref, wu_ref, a_sc)
    if not sample_out:
        _proj_res_body(None, j, n_j, a_sc, wd_ref, h_ref, gfin_ref, o_ref, None)
        return
    _proj_res_body(None, j, n_j, a_sc, wd_ref, h_ref, gfin_ref, acc, None)

    @pl.when(j == n_j - 1)
    def _():
        y = _rms(acc[0:N_SAMPLE_ROWS, :], gfin_ref[...])
        for t in range(SAMPLE_T):
            o_ref[:, t, :] = y[t * N_SAMPLE_SEQ:(t + 1) * N_SAMPLE_SEQ, :]


def _ffn_s(h, g, g_fin, w_gu, w_dn, bn, sample_out, name):
    rows = h.shape[0]
    n_j = D_FF // bn
    row_spec = pl.BlockSpec((1, D_MODEL), lambda j: (0, 0))
    o_shape = (N_SAMPLE_SEQ, SAMPLE_T, D_MODEL) if sample_out else (rows, D_MODEL)
    o_index = (0,) * len(o_shape)
    scratch = [pltpu.VMEM((rows, D_MODEL), BF16), pltpu.VMEM((rows, bn), BF16)]
    if sample_out:
        scratch.append(pltpu.VMEM((rows, D_MODEL), F32))
    return pl.pallas_call(
        functools.partial(_ffn_s_kernel, sample_out, n_j),
        grid=(n_j,),
        in_specs=[
            _resident((rows, D_MODEL), lambda j: (0, 0)),
            row_spec, row_spec,
            pl.BlockSpec((D_MODEL, bn), lambda j: (0, j)),
            pl.BlockSpec((D_MODEL, bn), lambda j: (0, n_j + j)),
            pl.BlockSpec((bn, D_MODEL), lambda j: (j, 0)),
        ],
        out_specs=pl.BlockSpec(o_shape, lambda j: o_index),
        out_shape=jax.ShapeDtypeStruct(o_shape, F32),
        scratch_shapes=scratch,
        compiler_params=_params(1),
        name=name,
    )(h, g, g_fin, w_gu, w_gu, w_dn)


def _gate_up_p(m, w_gu, bm, bn, casts, name):
    rows = m.shape[0]
    n_i, n_j = rows // bm, D_FF // bn
    return pl.pallas_call(
        functools.partial(_gate_up_p_kernel, len(casts)),
        grid=(n_i, n_j),
        in_specs=[
            pl.BlockSpec((bm, D_MODEL), lambda i, j: (i, 0)),
            pl.BlockSpec((D_MODEL, bn), lambda i, j: (0, j)),
            pl.BlockSpec((D_MODEL, bn), lambda i, j: (0, n_j + j)),
        ] + [c.in_spec for c in casts],
        out_specs=[pl.BlockSpec((bm, bn), lambda i, j: (i, j))] + [c.out_spec for c in casts],
        out_shape=[jax.ShapeDtypeStruct((rows, D_FF), BF16)] + [c.out_shape for c in casts],
        compiler_params=_params(2),
        name=name,
    )(m, w_gu, w_gu, *[c.src for c in casts])


def _pool_p_kernel(tiles_per_seq, h_ref, halo_p_ref, halo_s_ref, g_ref, wp_ref, sc_ref, gnext_ref,
                   o_ref, ntail_ref, m_ref, nbuf, pbuf, qbuf):
    i = pl.program_id(0)
    bm = h_ref.shape[0]
    g = g_ref[...]
    x = h_ref[...]
    n = _rms(x, g)
    rows = HALO + bm
    r0 = FRONT + HALO

    for buf in (nbuf, pbuf, qbuf):
        buf[0:FRONT, :] = jnp.zeros((FRONT, buf.shape[1]), F32)

    @pl.when(i % tiles_per_seq == 0)
    def _():
        nbuf[FRONT:r0, :] = _rms(halo_s_ref[...], g)

    @pl.when(i % tiles_per_seq != 0)
    def _():
        nbuf[FRONT:r0, :] = _rms(halo_p_ref[...], g)

    nbuf[r0:r0 + bm, :] = n
    ntail_ref[...] = n[bm - HALO:, :]
    for gi, win in enumerate(POOL_WINDOWS):
        cols = slice(gi * POOL_GROUP, (gi + 1) * POOL_GROUP)
        src, shift, level = nbuf, 1, 0
        while shift < win:
            dst = (pbuf, qbuf)[level % 2]
            dst[FRONT:FRONT + rows, cols] = (src[FRONT:FRONT + rows, cols]
                                             + src[FRONT - shift:FRONT - shift + rows, cols])
            src, shift, level = dst, 2 * shift, level + 1
        ng = n[:, cols]
        p = src[r0:r0 + bm, cols] * (1.0 / win) - ng
        y = _dot(p.astype(BF16), wp_ref[gi].astype(BF16)) * sc_ref[:, cols]
        o_ref[:, cols] = x[:, cols] + y
    m_ref[...] = _rms(o_ref[...], gnext_ref[...]).astype(BF16)


def _pool_p(h_p, h_s, g, wp, scale, g_next, bm):
    rows = h_p.shape[0]
    n_i = rows // bm
    ng = len(POOL_WINDOWS)
    return pl.pallas_call(
        functools.partial(_pool_p_kernel, SEQ // bm),
        grid=(n_i,),
        in_specs=[
            pl.BlockSpec((bm, D_MODEL), lambda i: (i, 0)),
            pl.BlockSpec((HALO, D_MODEL), lambda i: (jnp.maximum(i * (bm // HALO) - 1, 0), 0)),
            pl.BlockSpec((HALO, D_MODEL), lambda i: (META_ROW0 // HALO, 0)),
            pl.BlockSpec((1, D_MODEL), lambda i: (0, 0)),
            pl.BlockSpec((None, ng, POOL_GROUP, POOL_GROUP), lambda i: (0, 0, 0, 0)),
            pl.BlockSpec((1, D_MODEL), lambda i: (0, 0)),
            pl.BlockSpec((1, D_MODEL), lambda i: (0, 0)),
        ],
        out_specs=[
            pl.BlockSpec((bm, D_MODEL), lambda i: (i, 0)),
            pl.BlockSpec((HALO, D_MODEL), lambda i: (i, 0)),
            pl.BlockSpec((bm, D_MODEL), lambda i: (i, 0)),
        ],
        out_shape=[
            jax.ShapeDtypeStruct((rows, D_MODEL), F32),
            jax.ShapeDtypeStruct((n_i * HALO, D_MODEL), F32),
            jax.ShapeDtypeStruct((rows, D_MODEL), BF16),
        ],
        scratch_shapes=[pltpu.VMEM((FRONT + HALO + bm, D_MODEL), F32)] * 3,
        compiler_params=_params(1),
        name="pool_p",
    )(h_p, h_p, h_s, g, wp, scale, g_next)


def _pool_s_kernel(hfull_ref, hcol_ref, hp_ref, g_ref, wp_ref, sc_ref, o_ref, tail_ref, inv_sc, nbuf, sum_sc):
    j = pl.program_id(0)
    ns = N_SAMPLE_ROWS
    nb = N_SAMPLE_SEQ
    nh = POOL_HIST * nb

    @pl.when(j == 0)
    def _():
        xf = hfull_ref[...]
        inv_sc[...] = lax.rsqrt(jnp.mean(xf * xf, axis=-1, keepdims=True) + EPS)

    x = hcol_ref[...]
    n = ((x * inv_sc[...]) * g_ref[...])[0:ns, :]
    nbuf[0:nh, :] = hp_ref[...]
    nbuf[nh:nh + ns, :] = n
    tail_ref[...] = nbuf[ns:ns + nh, :]

    for gi, win in enumerate(POOL_WINDOWS):
        @pl.when(j == gi)
        def _(win=win):
            acc = n
            for k in range(1, win):
                acc = acc + nbuf[nh - k * nb:nh - k * nb + ns, :]
            sum_sc[...] = acc * (1.0 / win)

    p = sum_sc[...] - n
    y = _dot(p.astype(BF16), wp_ref[...].astype(BF16)) * sc_ref[...]
    o_ref[0:ns, :] = x[0:ns, :] + y
    o_ref[ns:, :] = x[ns:, :]


def _pool_s(h_s, state_pool, g, wp, scale):
    ns = N_SAMPLE_ROWS
    nb = N_SAMPLE_SEQ
    nh = POOL_HIST * nb
    pg = POOL_GROUP
    n_g = len(POOL_WINDOWS)
    state_spec = pl.BlockSpec((nh, pg), lambda j: (0, j))
    return pl.pallas_call(
        _pool_s_kernel,
        grid=(n_g,),
        in_specs=[
            _resident((S_ROWS, D_MODEL), lambda j: (0, 0)),
            pl.BlockSpec((S_ROWS, pg), lambda j: (0, j)),
            state_spec,
            pl.BlockSpec((1, pg), lambda j: (0, j)),
            pl.BlockSpec((None, None, pg, pg), lambda j: (0, j, 0, 0)),
            pl.BlockSpec((1, pg), lambda j: (0, j)),
        ],
        out_specs=[pl.BlockSpec((S_ROWS, pg), lambda j: (0, j)), state_spec],
        out_shape=[
            jax.ShapeDtypeStruct((S_ROWS, D_MODEL), F32),
            jax.ShapeDtypeStruct((nh, D_MODEL), F32),
        ],
        scratch_shapes=[
            pltpu.VMEM((S_ROWS, 1), F32),
            pltpu.VMEM((nh + ns, pg), F32),
            pltpu.VMEM((ns, pg), F32),
        ],
        compiler_params=_params(1),
        name="pool_s",
    )(h_s, h_s, state_pool, g, wp, scale)


BM_P = 1024
BN_MIX = 512
BM_CONV_OUT = 512
BM_GATE_UP = 2048
BN_P = 512
BM_DOWN = 512
BK_DOWN = D_FF
BM_POOL = 512
BN_S = 256
BN_S_FF = 512
BK_S = 512

assert SEQ % BM_P == 0 and SEQ % BM_POOL == 0 and BM_POOL % HALO == 0 and META_ROW0 % HALO == 0
assert D_MODEL % BN_MIX == 0 and D_MODEL % BN_S == 0 and BN_MIX % MXU_COLS == 0 and BN_S % MXU_COLS == 0
assert D_FF % BN_P == 0 and D_FF % BN_S_FF == 0 and D_FF % BK_S == 0 and D_MODEL % BK_S == 0


def kernel(x_prompt, x_sample, state_conv, state_pool, meta_tokens, norm_mix, norm_ffn, norm_final,
           conv_w_in, conv_w_dw, conv_w_out, pool_w, pool_scale, ffn_w_gate_up, ffn_w_down):
    d = D_MODEL
    nb, nt = N_SAMPLE_SEQ, SAMPLE_T
    batch = x_prompt.shape[0]
    assert x_prompt.shape == (batch, SEQ, d) and x_sample.shape == (nb, nt, d)
    assert state_conv.shape == (1, nb, CONV_HIST, d) and state_pool.shape == (1, nb, POOL_HIST, d)
    assert meta_tokens.shape == (N_META, d) and ffn_w_gate_up.shape == (2, d, 2 * D_FF)
    assert (batch * SEQ) % BM_GATE_UP == 0 and (batch * SEQ) % BM_DOWN == 0 and (batch * SEQ) % BM_CONV_OUT == 0

    h_p = x_prompt.reshape(batch * SEQ, d)
    hc = state_conv[0].transpose(1, 0, 2).reshape(CONV_HIST * nb, d)
    hp = state_pool[0].transpose(1, 0, 2).reshape(POOL_HIST * nb, d)
    row = lambda v: v.reshape(1, d)
    g_mix, g_ffn, g_fin = norm_mix, norm_ffn, row(norm_final)

    rows_p = batch * SEQ
    n_j_mix, n_j_ff = d // BN_MIX, D_FF // BN_P
    n_mix, n_ff = (rows_p // BM_P) * n_j_mix, (rows_p // BM_GATE_UP) * n_j_ff
    step_mix = lambda i, j: i * n_j_mix + j
    step_ff = lambda i, j: i * n_j_ff + j
    casts_mix = [_cast_job(conv_w_out, 0, n_mix, 0, step_mix), _cast_job(ffn_w_gate_up, 0, n_mix, 0, step_mix)]
    casts_ff0 = [_cast_job(ffn_w_down, 0, n_ff, 0, step_ff), _cast_job(ffn_w_gate_up, 1, n_ff, 1, step_ff)]
    casts_ff1 = [_cast_job(ffn_w_down, 1, n_ff, 0, step_ff)]

    z_s, uts, umeta, wb, wc, wv, h_s = _mix_conv_s(
        x_sample, meta_tokens, row(g_mix[0]), conv_w_in, conv_w_dw[0], hc, BN_S)
    z_p, utail_p, w_out, w_gu0 = _mix_conv_p(
        h_p, row(g_mix[0]), wb, wc, wv, conv_w_dw[0], umeta, BM_P, BN_MIX, casts_mix)
    (h_s,) = _proj_res_p(z_s, w_out, h_s, g_fin, S_ROWS, BK_S, None, "conv_out_s")
    h_p, m_p = _proj_res_p(z_p, w_out, h_p, row(g_ffn[0]), BM_CONV_OUT, D_MODEL, "emit", "conv_out_p")
    a_p, w_dn0, w_gu1 = _gate_up_p(m_p, w_gu0, BM_GATE_UP, BN_P, casts_ff0, "gate_up0_p")
    h_s = _ffn_s(h_s, row(g_ffn[0]), g_fin, w_gu0, w_dn0, BN_S_FF, False, "ffn0_s")
    (h_p,) = _proj_res_p(a_p, w_dn0, h_p, g_fin, BM_DOWN, BK_DOWN, None, "down0_p")
    h_p1, ntail_p, m_p = _pool_p(h_p, h_s, row(g_mix[1]), pool_w, row(pool_scale[0]), row(g_ffn[1]), BM_POOL)
    h_s1, pool_tail_s = _pool_s(h_s, hp, row(g_mix[1]), pool_w, row(pool_scale[0]))
    a_p, w_dn1 = _gate_up_p(m_p, w_gu1, BM_GATE_UP, BN_P, casts_ff1, "gate_up1_p")
    y_s = _ffn_s(h_s1, row(g_ffn[1]), g_fin, w_gu1, w_dn1, BN_S_FF, True, "ffn1_s")
    (y_p,) = _proj_res_p(a_p, w_dn1, h_p1, g_fin, BM_DOWN, BK_DOWN, "final", "down1_p")

    y_prompt = y_p.reshape(batch, SEQ, d)
    y_sample = y_s
    tps = SEQ // BM_P
    new_conv_prompt = utail_p.reshape(batch, tps, TAIL, d)[:, tps - 1, TAIL - CONV_HIST:, :][None]
    tpp = SEQ // BM_POOL
    new_pool_prompt = ntail_p.reshape(batch, tpp, HALO, d)[:, tpp - 1, HALO - POOL_HIST:, :][None]
    new_conv_sample = uts.reshape(CONV_HIST, nb, d).transpose(1, 0, 2)[None]
    new_pool_sample = pool_tail_s.reshape(POOL_HIST, nb, d).transpose(1, 0, 2)[None]
    return (y_prompt, y_sample, new_conv_prompt, new_pool_prompt, new_conv_sample, new_pool_sample)
```

```python
import functools
from typing import NamedTuple

import jax
import jax.numpy as jnp
from jax import lax
from jax.experimental import pallas as pl
from jax.experimental.pallas import tpu as pltpu

D_MODEL = 2048
D_FF = 5632
N_META = 16
SEQ = 2048
N_SAMPLE_SEQ = 128
SAMPLE_T = 8
CONV_HIST = 2
POOL_WINDOWS = (2, 4, 8, 16)
POOL_GROUP = D_MODEL // len(POOL_WINDOWS)
POOL_HIST = 15
EPS = 1e-6

N_SAMPLE_ROWS = N_SAMPLE_SEQ * SAMPLE_T
S_ROWS = N_SAMPLE_ROWS + 2 * N_META
META_ROW0 = N_SAMPLE_ROWS
TAIL = 8
HALO = 16
MXU_COLS = 256
FRONT = 8

V7X_SCOPED_VMEM_BYTES = 60000 * 1024

BF16 = jnp.bfloat16
F32 = jnp.float32


def _rms(x, g):
    ms = jnp.mean(x * x, axis=-1, keepdims=True)
    return (x * lax.rsqrt(ms + EPS)) * g


def _dot(a, b):
    return jnp.dot(a, b, preferred_element_type=F32)


def _params(n_axes):
    return pltpu.CompilerParams(
        dimension_semantics=("arbitrary",) * n_axes,
        vmem_limit_bytes=V7X_SCOPED_VMEM_BYTES,
    )


def _resident(shape, index_map):
    return pl.BlockSpec(shape, index_map, pipeline_mode=pl.Buffered(1))


class _CastJob(NamedTuple):
    src: jax.Array
    in_spec: pl.BlockSpec
    out_spec: pl.BlockSpec
    out_shape: jax.ShapeDtypeStruct


def _cast_job(w_stack, layer, n_steps, axis, step_of):
    _, r, c = w_stack.shape
    if axis == 0:
        blk = (r // n_steps, c)
        in_idx = lambda *ids: (layer, step_of(*ids), 0)
        out_idx = lambda *ids: (step_of(*ids), 0)
    else:
        blk = (r, c // n_steps)
        in_idx = lambda *ids: (layer, 0, step_of(*ids))
        out_idx = lambda *ids: (0, step_of(*ids))
    assert blk[0] * (n_steps if axis == 0 else 1) == r and blk[1] * (n_steps if axis == 1 else 1) == c
    return _CastJob(w_stack, pl.BlockSpec((None,) + blk, in_idx), pl.BlockSpec(blk, out_idx),
                    jax.ShapeDtypeStruct((r, c), BF16))


def _run_casts(src_refs, dst_refs):
    for src, dst in zip(src_refs, dst_refs):
        dst[...] = src[...].astype(BF16)


def _conv3(w, u, ubuf, off):
    t = u.shape[0]
    return (w[2:3] * u + w[1:2] * ubuf[off - 1:off - 1 + t, :] + w[0:1] * ubuf[off - 2:off - 2 + t, :])


def _mix_p_kernel(tiles_per_seq, n_cast, *refs):
    h_ref, g_ref, wb_ref, wc_ref, wv_ref, wdw_ref, umeta_ref = refs[:7]
    cast_src, refs = refs[7:7 + n_cast], refs[7 + n_cast:]
    z_ref, utail_ref = refs[:2]
    cast_dst, (n_sc, ubuf, carry) = refs[2:2 + n_cast], refs[2 + n_cast:]
    i = pl.program_id(0)
    j = pl.program_id(1)
    bm = h_ref.shape[0]
    _run_casts(cast_src, cast_dst)

    @pl.when(j == 0)
    def _():
        n_sc[...] = _rms(h_ref[...], g_ref[...]).astype(BF16)

    @pl.when(i % tiles_per_seq == 0)
    def _():
        ubuf[0:TAIL, :] = umeta_ref[...]

    @pl.when(i % tiles_per_seq != 0)
    def _():
        ubuf[0:TAIL, :] = carry[j]

    n = n_sc[...]
    for c0 in range(0, z_ref.shape[1], MXU_COLS):
        cols = slice(c0, c0 + MXU_COLS)
        b = _dot(n, wb_ref[:, cols])
        u = _dot(n, wc_ref[:, cols]) * _dot(n, wv_ref[:, cols])
        ubuf[TAIL:TAIL + bm, cols] = u
        z_ref[:, cols] = (b * _conv3(wdw_ref[:, cols], u, ubuf.at[:, cols], TAIL)).astype(BF16)
    tail = ubuf[bm:bm + TAIL, :]
    carry[j] = tail
    utail_ref[...] = tail


def _mix_s_kernel(xs_ref, meta_ref, g_ref, wb_ref, wc_ref, wv_ref, wdw_ref, hc_ref,
                  z_ref, uts_ref, umeta_ref, wb_o, wc_o, wv_o, h0_ref, n_sc, ubuf, mbuf):
    j = pl.program_id(0)
    ns = N_SAMPLE_ROWS
    nb = N_SAMPLE_SEQ
    nm = S_ROWS - ns

    @pl.when(j == 0)
    def _():
        for t in range(SAMPLE_T):
            h0_ref[t * nb:(t + 1) * nb, :] = xs_ref[:, t, :]
        h0_ref[ns:ns + N_META, :] = meta_ref[...]
        h0_ref[ns + N_META:, :] = jnp.zeros((nm - N_META, h0_ref.shape[1]), F32)
        n_sc[...] = _rms(h0_ref[...], g_ref[...]).astype(BF16)

    wb_o[...] = wb_ref[...].astype(BF16)
    wc_o[...] = wc_ref[...].astype(BF16)
    wv_o[...] = wv_ref[...].astype(BF16)
    n = n_sc[...]
    nh = CONV_HIST * nb
    ubuf[0:nh, :] = hc_ref[...]
    mbuf[0:TAIL, :] = jnp.zeros((TAIL, mbuf.shape[1]), F32)
    for c0 in range(0, z_ref.shape[1], MXU_COLS):
        cols = slice(c0, c0 + MXU_COLS)
        b = _dot(n, wb_o[:, cols])
        u = _dot(n, wc_o[:, cols]) * _dot(n, wv_o[:, cols])
        w = wdw_ref[:, cols]

        us = u[0:ns, :]
        ubuf[nh:nh + ns, cols] = us
        conv_s = (w[2:3] * us + w[1:2] * ubuf[nb:nb + ns, cols] + w[0:1] * ubuf[0:ns, cols])
        z_ref[0:ns, cols] = (b[0:ns, :] * conv_s).astype(BF16)

        um = u[ns:, :]
        mbuf[TAIL:TAIL + nm, cols] = um
        z_ref[ns:, cols] = (b[ns:, :] * _conv3(w, um, mbuf.at[:, cols], TAIL)).astype(BF16)
    uts_ref[...] = ubuf[ns:ns + nh, :]
    umeta_ref[...] = mbuf[N_META:N_META + TAIL, :]


def _mix_conv_p(h, g, wb, wc, wv, w_dw, umeta, bm, bn, casts):
    rows = h.shape[0]
    n_i, n_j = rows // bm, D_MODEL // bn
    wspec = pl.BlockSpec((D_MODEL, bn), lambda i, j: (0, j))
    return pl.pallas_call(
        functools.partial(_mix_p_kernel, SEQ // bm, len(casts)),
        grid=(n_i, n_j),
        in_specs=[
            pl.BlockSpec((bm, D_MODEL), lambda i, j: (i, 0)),
            pl.BlockSpec((1, D_MODEL), lambda i, j: (0, 0)),
            wspec, wspec, wspec,
            pl.BlockSpec((3, bn), lambda i, j: (0, j)),
            pl.BlockSpec((TAIL, bn), lambda i, j: (0, j)),
        ] + [c.in_spec for c in casts],
        out_specs=[
            pl.BlockSpec((bm, bn), lambda i, j: (i, j)),
            pl.BlockSpec((TAIL, bn), lambda i, j: (i, j)),
        ] + [c.out_spec for c in casts],
        out_shape=[
            jax.ShapeDtypeStruct((rows, D_MODEL), BF16),
            jax.ShapeDtypeStruct((n_i * TAIL, D_MODEL), F32),
        ] + [c.out_shape for c in casts],
        scratch_shapes=[
            pltpu.VMEM((bm, D_MODEL), BF16),
            pltpu.VMEM((TAIL + bm, bn), F32),
            pltpu.VMEM((n_j, TAIL, bn), F32),
        ],
        compiler_params=_params(2),
        name="mix_conv_p",
    )(h, g, wb, wc, wv, w_dw, umeta, *[c.src for c in casts])


def _mix_conv_s(x_sample, meta, g, w_in, w_dw, hc, bn):
    n_j = D_MODEL // bn
    nb = N_SAMPLE_SEQ
    nm = S_ROWS - N_SAMPLE_ROWS
    wspec = lambda part: pl.BlockSpec((None, D_MODEL, bn), lambda j: (0, 0, part * n_j + j))
    wout = pl.BlockSpec((D_MODEL, bn), lambda j: (0, j))
    wshape = jax.ShapeDtypeStruct((D_MODEL, D_MODEL), BF16)
    return pl.pallas_call(
        _mix_s_kernel,
        grid=(n_j,),
        in_specs=[
            _resident((nb, SAMPLE_T, D_MODEL), lambda j: (0, 0, 0)),
            _resident((N_META, D_MODEL), lambda j: (0, 0)),
            pl.BlockSpec((1, D_MODEL), lambda j: (0, 0)),
            wspec(0), wspec(1), wspec(2),
            pl.BlockSpec((3, bn), lambda j: (0, j)),
            pl.BlockSpec((CONV_HIST * nb, bn), lambda j: (0, j)),
        ],
        out_specs=[
            pl.BlockSpec((S_ROWS, bn), lambda j: (0, j)),
            pl.BlockSpec((CONV_HIST * nb, bn), lambda j: (0, j)),
            pl.BlockSpec((TAIL, bn), lambda j: (0, j)),
            wout, wout, wout,
            pl.BlockSpec((S_ROWS, D_MODEL), lambda j: (0, 0)),
        ],
        out_shape=[
            jax.ShapeDtypeStruct((S_ROWS, D_MODEL), BF16),
            jax.ShapeDtypeStruct((CONV_HIST * nb, D_MODEL), F32),
            jax.ShapeDtypeStruct((TAIL, D_MODEL), F32),
            wshape, wshape, wshape,
            jax.ShapeDtypeStruct((S_ROWS, D_MODEL), F32),
        ],
        scratch_shapes=[
            pltpu.VMEM((S_ROWS, D_MODEL), BF16),
            pltpu.VMEM((CONV_HIST * nb + N_SAMPLE_ROWS, bn), F32),
            pltpu.VMEM((TAIL + nm, bn), F32),
        ],
        compiler_params=_params(1),
        name="mix_conv_s",
    )(x_sample, meta, g, w_in, w_in, w_in, w_dw, hc)


def _proj_res_body(norm, k, n_k, x_ref, w_ref, h_ref, g_ref, o_ref, m_ref):
    if n_k == 1:
        acc = h_ref[...] + _dot(x_ref[...], w_ref[...])
        o_ref[...] = _rms(acc, g_ref[...]) if norm == "final" else acc
        if norm == "emit":
            m_ref[...] = _rms(acc, g_ref[...]).astype(BF16)
        return

    @pl.when(k == 0)
    def _():
        o_ref[...] = h_ref[...] + _dot(x_ref[...], w_ref[...])

    @pl.when(k != 0)
    def _():
        o_ref[...] += _dot(x_ref[...], w_ref[...])

    if norm is not None:
        @pl.when(k == n_k - 1)
        def _():
            y = _rms(o_ref[...], g_ref[...])
            if norm == "final":
                o_ref[...] = y
            else:
                m_ref[...] = y.astype(BF16)


def _proj_res_p_kernel(norm, n_k, x_ref, w_ref, h_ref, g_ref, o_ref, m_ref=None):
    _proj_res_body(norm, pl.program_id(1), n_k, x_ref, w_ref, h_ref, g_ref, o_ref, m_ref)


def _proj_res_p(x, w, h, g, bm, bk, norm, name):
    rows, kdim = x.shape
    n_i, n_k = rows // bm, kdim // bk
    wspec = _resident if n_k == 1 else pl.BlockSpec
    row_spec = pl.BlockSpec((bm, D_MODEL), lambda i, k: (i, 0))
    out_specs, out_shape = [row_spec], [jax.ShapeDtypeStruct((rows, D_MODEL), F32)]
    if norm == "emit":
        out_specs.append(row_spec)
        out_shape.append(jax.ShapeDtypeStruct((rows, D_MODEL), BF16))
    return pl.pallas_call(
        functools.partial(_proj_res_p_kernel, norm, n_k),
        grid=(n_i, n_k),
        in_specs=[
            pl.BlockSpec((bm, bk), lambda i, k: (i, k)),
            wspec((bk, D_MODEL), lambda i, k: (k, 0)),
            row_spec,
            pl.BlockSpec((1, D_MODEL), lambda i, k: (0, 0)),
        ],
        out_specs=out_specs,
        out_shape=out_shape,
        compiler_params=_params(2),
        name=name,
    )(x, w, h, g)


def _swiglu_chunks(m, wg_ref, wu_ref, a_ref):
    for c0 in range(0, a_ref.shape[1], MXU_COLS):
        cols = slice(c0, c0 + MXU_COLS)
        gate = _dot(m, wg_ref[:, cols])
        up = _dot(m, wu_ref[:, cols])
        a_ref[:, cols] = (gate * (1.0 / (1.0 + jnp.exp(-gate))) * up).astype(BF16)


def _gate_up_p_kernel(n_cast, m_ref, wg_ref, wu_ref, *refs):
    _run_casts(refs[:n_cast], refs[n_cast + 1:])
    _swiglu_chunks(m_ref[...], wg_ref, wu_ref, refs[n_cast])


def _ffn_s_kernel(sample_out, n_pre, n_j, *refs):
    h_ref, refs = refs[0], refs[1:]
    if n_pre:
        (x_ref, wpre_ref), refs = refs[:2], refs[2:]
    g_ref, gfin_ref, wg_ref, wu_ref, wd_ref, o_ref, m_sc, a_sc = refs[:8]
    rest = list(refs[8:])
    hsrc = rest.pop(0) if n_pre else h_ref
    acc = rest.pop(0) if sample_out else o_ref
    s = pl.program_id(0)
    j = s - n_pre

    if n_pre:
        @pl.when(s < n_pre)
        def _():
            _proj_res_body(None, s, n_pre, x_ref, wpre_ref, h_ref, None, hsrc, None)

    @pl.when(s >= n_pre)
    def _():
        @pl.when(j == 0)
        def _():
            m_sc[...] = _rms(hsrc[...], g_ref[...]).astype(BF16)

        _swiglu_chunks(m_sc[...], wg_ref, wu_ref, a_sc)
        _proj_res_body(None, j, n_j, a_sc, wd_ref, hsrc, None, acc, None)

        if sample_out:
            @pl.when(j == n_j - 1)
            def _():
                y = _rms(acc[0:N_SAMPLE_ROWS, :], gfin_ref[...])
                for t in range(SAMPLE_T):
                    o_ref[:, t, :] = y[t * N_SAMPLE_SEQ:(t + 1) * N_SAMPLE_SEQ, :]


def _ffn_s(h, g, g_fin, w_gu, w_dn, bn, sample_out, name, pre=None):
    rows = h.shape[0]
    n_j = D_FF // bn
    n_pre = 0 if pre is None else pre[0].shape[1] // pre[2]
    ffn_step = lambda s: jnp.maximum(s - n_pre, 0)
    row_spec = pl.BlockSpec((1, D_MODEL), lambda s: (0, 0))
    o_shape = (N_SAMPLE_SEQ, SAMPLE_T, D_MODEL) if sample_out else (rows, D_MODEL)
    o_index = (0,) * len(o_shape)
    in_specs, args = [_resident((rows, D_MODEL), lambda s: (0, 0))], [h]
    scratch = [pltpu.VMEM((rows, D_MODEL), BF16), pltpu.VMEM((rows, bn), BF16)]
    if n_pre:
        x, w_pre, bk = pre
        pre_step = lambda s: jnp.minimum(s, n_pre - 1)
        in_specs += [pl.BlockSpec((rows, bk), lambda s: (0, pre_step(s))),
                     pl.BlockSpec((bk, D_MODEL), lambda s: (pre_step(s), 0))]
        args += [x, w_pre]
        scratch.append(pltpu.VMEM((rows, D_MODEL), F32))
    if sample_out:
        scratch.append(pltpu.VMEM((rows, D_MODEL), F32))
    in_specs += [
        row_spec, row_spec,
        pl.BlockSpec((D_MODEL, bn), lambda s: (0, ffn_step(s))),
        pl.BlockSpec((D_MODEL, bn), lambda s: (0, n_j + ffn_step(s))),
        pl.BlockSpec((bn, D_MODEL), lambda s: (ffn_step(s), 0)),
    ]
    return pl.pallas_call(
        functools.partial(_ffn_s_kernel, sample_out, n_pre, n_j),
        grid=(n_pre + n_j,),
        in_specs=in_specs,
        out_specs=pl.BlockSpec(o_shape, lambda s: o_index),
        out_shape=jax.ShapeDtypeStruct(o_shape, F32),
        scratch_shapes=scratch,
        compiler_params=_params(1),
        name=name,
    )(*args, g, g_fin, w_gu, w_gu, w_dn)


def _gate_up_p(m, w_gu, bm, bn, casts, name):
    rows = m.shape[0]
    n_i, n_j = rows // bm, D_FF // bn
    return pl.pallas_call(
        functools.partial(_gate_up_p_kernel, len(casts)),
        grid=(n_i, n_j),
        in_specs=[
            pl.BlockSpec((bm, D_MODEL), lambda i, j: (i, 0)),
            pl.BlockSpec((D_MODEL, bn), lambda i, j: (0, j)),
            pl.BlockSpec((D_MODEL, bn), lambda i, j: (0, n_j + j)),
        ] + [c.in_spec for c in casts],
        out_specs=[pl.BlockSpec((bm, bn), lambda i, j: (i, j))] + [c.out_spec for c in casts],
        out_shape=[jax.ShapeDtypeStruct((rows, D_FF), BF16)] + [c.out_shape for c in casts],
        compiler_params=_params(2),
        name=name,
    )(m, w_gu, w_gu, *[c.src for c in casts])


def _pool_p_kernel(tiles_per_seq, h_ref, halo_p_ref, halo_s_ref, g_ref, wp_ref, sc_ref, gnext_ref,
                   o_ref, ntail_ref, m_ref, nbuf, pbuf, qbuf):
    i = pl.program_id(0)
    bm = h_ref.shape[0]
    g = g_ref[...]
    x = h_ref[...]
    n = _rms(x, g)
    rows = HALO + bm
    r0 = FRONT + HALO

    for buf in (nbuf, pbuf, qbuf):
        buf[0:FRONT, :] = jnp.zeros((FRONT, buf.shape[1]), F32)

    @pl.when(i % tiles_per_seq == 0)
    def _():
        nbuf[FRONT:r0, :] = _rms(halo_s_ref[...], g)

    @pl.when(i % tiles_per_seq != 0)
    def _():
        nbuf[FRONT:r0, :] = _rms(halo_p_ref[...], g)

    nbuf[r0:r0 + bm, :] = n
    ntail_ref[...] = n[bm - HALO:, :]
    for gi, win in enumerate(POOL_WINDOWS):
        cols = slice(gi * POOL_GROUP, (gi + 1) * POOL_GROUP)
        src, shift, level = nbuf, 1, 0
        while shift < win:
            dst = (pbuf, qbuf)[level % 2]
            dst[FRONT:FRONT + rows, cols] = (src[FRONT:FRONT + rows, cols]
                                             + src[FRONT - shift:FRONT - shift + rows, cols])
            src, shift, level = dst, 2 * shift, level + 1
        ng = n[:, cols]
        p = src[r0:r0 + bm, cols] * (1.0 / win) - ng
        y = _dot(p.astype(BF16), wp_ref[gi].astype(BF16)) * sc_ref[:, cols]
        o_ref[:, cols] = x[:, cols] + y
    m_ref[...] = _rms(o_ref[...], gnext_ref[...]).astype(BF16)


def _pool_p(h_p, h_s, g, wp, scale, g_next, bm):
    rows = h_p.shape[0]
    n_i = rows // bm
    ng = len(POOL_WINDOWS)
    return pl.pallas_call(
        functools.partial(_pool_p_kernel, SEQ // bm),
        grid=(n_i,),
        in_specs=[
            pl.BlockSpec((bm, D_MODEL), lambda i: (i, 0)),
            pl.BlockSpec((HALO, D_MODEL), lambda i: (jnp.maximum(i * (bm // HALO) - 1, 0), 0)),
            pl.BlockSpec((HALO, D_MODEL), lambda i: (META_ROW0 // HALO, 0)),
            pl.BlockSpec((1, D_MODEL), lambda i: (0, 0)),
            pl.BlockSpec((None, ng, POOL_GROUP, POOL_GROUP), lambda i: (0, 0, 0, 0)),
            pl.BlockSpec((1, D_MODEL), lambda i: (0, 0)),
            pl.BlockSpec((1, D_MODEL), lambda i: (0, 0)),
        ],
        out_specs=[
            pl.BlockSpec((bm, D_MODEL), lambda i: (i, 0)),
            pl.BlockSpec((HALO, D_MODEL), lambda i: (i, 0)),
            pl.BlockSpec((bm, D_MODEL), lambda i: (i, 0)),
        ],
        out_shape=[
            jax.ShapeDtypeStruct((rows, D_MODEL), F32),
            jax.ShapeDtypeStruct((n_i * HALO, D_MODEL), F32),
            jax.ShapeDtypeStruct((rows, D_MODEL), BF16),
        ],
        scratch_shapes=[pltpu.VMEM((FRONT + HALO + bm, D_MODEL), F32)] * 3,
        compiler_params=_params(1),
        name="pool_p",
    )(h_p, h_p, h_s, g, wp, scale, g_next)


def _pool_s_kernel(hfull_ref, hcol_ref, hp_ref, g_ref, wp_ref, sc_ref, o_ref, tail_ref, inv_sc, nbuf, sum_sc):
    j = pl.program_id(0)
    ns = N_SAMPLE_ROWS
    nb = N_SAMPLE_SEQ
    nh = POOL_HIST * nb

    @pl.when(j == 0)
    def _():
        xf = hfull_ref[...]
        inv_sc[...] = lax.rsqrt(jnp.mean(xf * xf, axis=-1, keepdims=True) + EPS)

    x = hcol_ref[...]
    n = ((x * inv_sc[...]) * g_ref[...])[0:ns, :]
    nbuf[0:nh, :] = hp_ref[...]
    nbuf[nh:nh + ns, :] = n
    tail_ref[...] = nbuf[ns:ns + nh, :]

    for gi, win in enumerate(POOL_WINDOWS):
        @pl.when(j == gi)
        def _(win=win):
            acc = n
            for k in range(1, win):
                acc = acc + nbuf[nh - k * nb:nh - k * nb + ns, :]
            sum_sc[...] = acc * (1.0 / win)

    p = sum_sc[...] - n
    y = _dot(p.astype(BF16), wp_ref[...].astype(BF16)) * sc_ref[...]
    o_ref[0:ns, :] = x[0:ns, :] + y
    o_ref[ns:, :] = x[ns:, :]


def _pool_s(h_s, state_pool, g, wp, scale):
    ns = N_SAMPLE_ROWS
    nb = N_SAMPLE_SEQ
    nh = POOL_HIST * nb
    pg = POOL_GROUP
    n_g = len(POOL_WINDOWS)
    state_spec = pl.BlockSpec((nh, pg), lambda j: (0, j))
    return pl.pallas_call(
        _pool_s_kernel,
        grid=(n_g,),
        in_specs=[
            _resident((S_ROWS, D_MODEL), lambda j: (0, 0)),
            pl.BlockSpec((S_ROWS, pg), lambda j: (0, j)),
            state_spec,
            pl.BlockSpec((1, pg), lambda j: (0, j)),
            pl.BlockSpec((None, None, pg, pg), lambda j: (0, j, 0, 0)),
            pl.BlockSpec((1, pg), lambda j: (0, j)),
        ],
        out_specs=[pl.BlockSpec((S_ROWS, pg), lambda j: (0, j)), state_spec],
        out_shape=[
            jax.ShapeDtypeStruct((S_ROWS, D_MODEL), F32),
            jax.ShapeDtypeStruct((nh, D_MODEL), F32),
        ],
        scratch_shapes=[
            pltpu.VMEM((S_ROWS, 1), F32),
            pltpu.VMEM((nh + ns, pg), F32),
            pltpu.VMEM((ns, pg), F32),
        ],
        compiler_params=_params(1),
        name="pool_s",
    )(h_s, h_s, state_pool, g, wp, scale)


BM_P = 1024
BN_MIX = 512
BM_CONV_OUT = 512
BM_GATE_UP = 2048
BN_P = 512
BM_DOWN = 512
BK_DOWN = D_FF
BM_POOL = 512
BN_S = 256
BN_S_FF = 512
BK_S = 512

assert SEQ % BM_P == 0 and SEQ % BM_POOL == 0 and BM_POOL % HALO == 0 and META_ROW0 % HALO == 0
assert D_MODEL % BN_MIX == 0 and D_MODEL % BN_S == 0 and BN_MIX % MXU_COLS == 0 and BN_S % MXU_COLS == 0
assert D_FF % BN_P == 0 and D_FF % BN_S_FF == 0 and D_FF % BK_S == 0 and D_MODEL % BK_S == 0


def kernel(x_prompt, x_sample, state_conv, state_pool, meta_tokens, norm_mix, norm_ffn, norm_final,
           conv_w_in, conv_w_dw, conv_w_out, pool_w, pool_scale, ffn_w_gate_up, ffn_w_down):
    d = D_MODEL
    nb, nt = N_SAMPLE_SEQ, SAMPLE_T
    batch = x_prompt.shape[0]
    assert x_prompt.shape == (batch, SEQ, d) and x_sample.shape == (nb, nt, d)
    assert state_conv.shape == (1, nb, CONV_HIST, d) and state_pool.shape == (1, nb, POOL_HIST, d)
    assert meta_tokens.shape == (N_META, d) and ffn_w_gate_up.shape == (2, d, 2 * D_FF)
    assert (batch * SEQ) % BM_GATE_UP == 0 and (batch * SEQ) % BM_DOWN == 0 and (batch * SEQ) % BM_CONV_OUT == 0

    h_p = x_prompt.reshape(batch * SEQ, d)
    hc = state_conv[0].transpose(1, 0, 2).reshape(CONV_HIST * nb, d)
    hp = state_pool[0].transpose(1, 0, 2).reshape(POOL_HIST * nb, d)
    row = lambda v: v.reshape(1, d)
    g_mix, g_ffn, g_fin = norm_mix, norm_ffn, row(norm_final)

    rows_p = batch * SEQ
    n_j_mix, n_j_ff = d // BN_MIX, D_FF // BN_P
    n_mix, n_ff = (rows_p // BM_P) * n_j_mix, (rows_p // BM_GATE_UP) * n_j_ff
    step_mix = lambda i, j: i * n_j_mix + j
    step_ff = lambda i, j: i * n_j_ff + j
    casts_mix = [_cast_job(conv_w_out, 0, n_mix, 0, step_mix), _cast_job(ffn_w_gate_up, 0, n_mix, 0, step_mix)]
    casts_ff0 = [_cast_job(ffn_w_down, 0, n_ff, 0, step_ff), _cast_job(ffn_w_gate_up, 1, n_ff, 1, step_ff)]
    casts_ff1 = [_cast_job(ffn_w_down, 1, n_ff, 0, step_ff)]

    z_s, uts, umeta, wb, wc, wv, h_s = _mix_conv_s(
        x_sample, meta_tokens, row(g_mix[0]), conv_w_in, conv_w_dw[0], hc, BN_S)
    z_p, utail_p, w_out, w_gu0 = _mix_conv_p(
        h_p, row(g_mix[0]), wb, wc, wv, conv_w_dw[0], umeta, BM_P, BN_MIX, casts_mix)
    h_p, m_p = _proj_res_p(z_p, w_out, h_p, row(g_ffn[0]), BM_CONV_OUT, D_MODEL, "emit", "conv_out_p")
    a_p, w_dn0, w_gu1 = _gate_up_p(m_p, w_gu0, BM_GATE_UP, BN_P, casts_ff0, "gate_up0_p")
    h_s = _ffn_s(h_s, row(g_ffn[0]), g_fin, w_gu0, w_dn0, BN_S_FF, False, "conv_out_ffn0_s", pre=(z_s, w_out, BK_S))
    (h_p,) = _proj_res_p(a_p, w_dn0, h_p, g_fin, BM_DOWN, BK_DOWN, None, "down0_p")
    h_p1, ntail_p, m_p = _pool_p(h_p, h_s, row(g_mix[1]), pool_w, row(pool_scale[0]), row(g_ffn[1]), BM_POOL)
    h_s1, pool_tail_s = _pool_s(h_s, hp, row(g_mix[1]), pool_w, row(pool_scale[0]))
    a_p, w_dn1 = _gate_up_p(m_p, w_gu1, BM_GATE_UP, BN_P, casts_ff1, "gate_up1_p")
    y_s = _ffn_s(h_s1, row(g_ffn[1]), g_fin, w_gu1, w_dn1, BN_S_FF, True, "ffn1_s")
    (y_p,) = _proj_res_p(a_p, w_dn1, h_p1, g_fin, BM_DOWN, BK_DOWN, "final", "down1_p")

    y_prompt = y_p.reshape(batch, SEQ, d)
    y_sample = y_s
    tps = SEQ // BM_P
    new_conv_prompt = utail_p.reshape(batch, tps, TAIL, d)[:, tps - 1, TAIL - CONV_HIST:, :][None]
    tpp = SEQ // BM_POOL
    new_pool_prompt = ntail_p.reshape(batch, tpp, HALO, d)[:, tpp - 1, HALO - POOL_HIST:, :][None]
    new_conv_sample = uts.reshape(CONV_HIST, nb, d).transpose(1, 0, 2)[None]
    new_pool_sample = pool_tail_s.reshape(POOL_HIST, nb, d).transpose(1, 0, 2)[None]
    return (y_prompt, y_sample, new_conv_prompt, new_pool_prompt, new_conv_sample, new_pool_sample)
```

```python
import functools
from typing import NamedTuple

import jax
import jax.numpy as jnp
from jax import lax
from jax.experimental import pallas as pl
from jax.experimental.pallas import tpu as pltpu

D_MODEL = 2048
D_FF = 5632
N_META = 16
SEQ = 2048
N_SAMPLE_SEQ = 128
SAMPLE_T = 8
CONV_HIST = 2
POOL_WINDOWS = (2, 4, 8, 16)
POOL_GROUP = D_MODEL // len(POOL_WINDOWS)
POOL_HIST = 15
EPS = 1e-6

N_SAMPLE_ROWS = N_SAMPLE_SEQ * SAMPLE_T
S_ROWS = N_SAMPLE_ROWS + N_META
META_ROW0 = N_SAMPLE_ROWS
TAIL = 8
HALO = 16
MXU_COLS = 256
FRONT = 8

V7X_SCOPED_VMEM_BYTES = 60000 * 1024

BF16 = jnp.bfloat16
F32 = jnp.float32


def _rms(x, g):
    ms = jnp.mean(x * x, axis=-1, keepdims=True)
    return (x * lax.rsqrt(ms + EPS)) * g


def _dot(a, b):
    return jnp.dot(a, b, preferred_element_type=F32)


def _params(n_axes):
    return pltpu.CompilerParams(
        dimension_semantics=("arbitrary",) * n_axes,
        vmem_limit_bytes=V7X_SCOPED_VMEM_BYTES,
    )


def _resident(shape, index_map):
    return pl.BlockSpec(shape, index_map, pipeline_mode=pl.Buffered(1))


class _CastJob(NamedTuple):
    src: jax.Array
    in_spec: pl.BlockSpec
    out_spec: pl.BlockSpec
    out_shape: jax.ShapeDtypeStruct


def _cast_job(w_stack, layer, n_steps, axis, step_of):
    _, r, c = w_stack.shape
    if axis == 0:
        blk = (r // n_steps, c)
        in_idx = lambda *ids: (layer, step_of(*ids), 0)
        out_idx = lambda *ids: (step_of(*ids), 0)
    else:
        blk = (r, c // n_steps)
        in_idx = lambda *ids: (layer, 0, step_of(*ids))
        out_idx = lambda *ids: (0, step_of(*ids))
    assert blk[0] * (n_steps if axis == 0 else 1) == r and blk[1] * (n_steps if axis == 1 else 1) == c
    return _CastJob(w_stack, pl.BlockSpec((None,) + blk, in_idx), pl.BlockSpec(blk, out_idx),
                    jax.ShapeDtypeStruct((r, c), BF16))


def _run_casts(src_refs, dst_refs):
    for src, dst in zip(src_refs, dst_refs):
        dst[...] = src[...].astype(BF16)


def _conv3(w, u, ubuf, off):
    t = u.shape[0]
    return (w[2:3] * u + w[1:2] * ubuf[off - 1:off - 1 + t, :] + w[0:1] * ubuf[off - 2:off - 2 + t, :])


def _mix_p_kernel(tiles_per_seq, n_cast, *refs):
    h_ref, g_ref, wb_ref, wc_ref, wv_ref, wdw_ref, umeta_ref = refs[:7]
    cast_src, refs = refs[7:7 + n_cast], refs[7 + n_cast:]
    z_ref, utail_ref = refs[:2]
    cast_dst, (n_sc, ubuf, carry) = refs[2:2 + n_cast], refs[2 + n_cast:]
    i = pl.program_id(0)
    j = pl.program_id(1)
    bm = h_ref.shape[0]
    _run_casts(cast_src, cast_dst)

    @pl.when(j == 0)
    def _():
        n_sc[...] = _rms(h_ref[...], g_ref[...]).astype(BF16)

    @pl.when(i % tiles_per_seq == 0)
    def _():
        ubuf[0:TAIL, :] = umeta_ref[...]

    @pl.when(i % tiles_per_seq != 0)
    def _():
        ubuf[0:TAIL, :] = carry[j]

    n = n_sc[...]
    for c0 in range(0, z_ref.shape[1], MXU_COLS):
        cols = slice(c0, c0 + MXU_COLS)
        b = _dot(n, wb_ref[:, cols])
        u = _dot(n, wc_ref[:, cols]) * _dot(n, wv_ref[:, cols])
        ubuf[TAIL:TAIL + bm, cols] = u
        z_ref[:, cols] = (b * _conv3(wdw_ref[:, cols], u, ubuf.at[:, cols], TAIL)).astype(BF16)
    tail = ubuf[bm:bm + TAIL, :]
    carry[j] = tail
    utail_ref[...] = tail


def _mix_s_kernel(xs_ref, meta_ref, g_ref, wb_ref, wc_ref, wv_ref, wdw_ref, hc_ref,
                  z_ref, uts_ref, umeta_ref, wb_o, wc_o, wv_o, h0_ref, n_sc, ubuf, mbuf):
    j = pl.program_id(0)
    ns = N_SAMPLE_ROWS
    nb = N_SAMPLE_SEQ
    nm = S_ROWS - ns

    @pl.when(j == 0)
    def _():
        for t in range(SAMPLE_T):
            h0_ref[t * nb:(t + 1) * nb, :] = xs_ref[:, t, :]
        h0_ref[ns:, :] = meta_ref[...]
        n_sc[...] = _rms(h0_ref[...], g_ref[...]).astype(BF16)

    wb_o[...] = wb_ref[...].astype(BF16)
    wc_o[...] = wc_ref[...].astype(BF16)
    wv_o[...] = wv_ref[...].astype(BF16)
    n = n_sc[...]
    nh = CONV_HIST * nb
    ubuf[0:nh, :] = hc_ref[...]
    mbuf[0:TAIL, :] = jnp.zeros((TAIL, mbuf.shape[1]), F32)
    for c0 in range(0, z_ref.shape[1], MXU_COLS):
        cols = slice(c0, c0 + MXU_COLS)
        b = _dot(n, wb_o[:, cols])
        u = _dot(n, wc_o[:, cols]) * _dot(n, wv_o[:, cols])
        w = wdw_ref[:, cols]

        us = u[0:ns, :]
        ubuf[nh:nh + ns, cols] = us
        conv_s = (w[2:3] * us + w[1:2] * ubuf[nb:nb + ns, cols] + w[0:1] * ubuf[0:ns, cols])
        z_ref[0:ns, cols] = (b[0:ns, :] * conv_s).astype(BF16)

        um = u[ns:, :]
        mbuf[TAIL:TAIL + nm, cols] = um
        z_ref[ns:, cols] = (b[ns:, :] * _conv3(w, um, mbuf.at[:, cols], TAIL)).astype(BF16)
    uts_ref[...] = ubuf[ns:ns + nh, :]
    umeta_ref[...] = mbuf[N_META:N_META + TAIL, :]


def _mix_conv_p(h, g, wb, wc, wv, w_dw, umeta, bm, bn, casts):
    rows = h.shape[0]
    n_i, n_j = rows // bm, D_MODEL // bn
    wspec = pl.BlockSpec((D_MODEL, bn), lambda i, j: (0, j))
    return pl.pallas_call(
        functools.partial(_mix_p_kernel, SEQ // bm, len(casts)),
        grid=(n_i, n_j),
        in_specs=[
            pl.BlockSpec((bm, D_MODEL), lambda i, j: (i, 0)),
            pl.BlockSpec((1, D_MODEL), lambda i, j: (0, 0)),
            wspec, wspec, wspec,
            pl.BlockSpec((3, bn), lambda i, j: (0, j)),
            pl.BlockSpec((TAIL, bn), lambda i, j: (0, j)),
        ] + [c.in_spec for c in casts],
        out_specs=[
            pl.BlockSpec((bm, bn), lambda i, j: (i, j)),
            pl.BlockSpec((TAIL, bn), lambda i, j: (i, j)),
        ] + [c.out_spec for c in casts],
        out_shape=[
            jax.ShapeDtypeStruct((rows, D_MODEL), BF16),
            jax.ShapeDtypeStruct((n_i * TAIL, D_MODEL), F32),
        ] + [c.out_shape for c in casts],
        scratch_shapes=[
            pltpu.VMEM((bm, D_MODEL), BF16),
            pltpu.VMEM((TAIL + bm, bn), F32),
            pltpu.VMEM((n_j, TAIL, bn), F32),
        ],
        compiler_params=_params(2),
        name="mix_conv_p",
    )(h, g, wb, wc, wv, w_dw, umeta, *[c.src for c in casts])


def _mix_conv_s(x_sample, meta, g, w_in, w_dw, hc, bn):
    n_j = D_MODEL // bn
    nb = N_SAMPLE_SEQ
    nm = S_ROWS - N_SAMPLE_ROWS
    wspec = lambda part: pl.BlockSpec((None, D_MODEL, bn), lambda j: (0, 0, part * n_j + j))
    wout = pl.BlockSpec((D_MODEL, bn), lambda j: (0, j))
    wshape = jax.ShapeDtypeStruct((D_MODEL, D_MODEL), BF16)
    return pl.pallas_call(
        _mix_s_kernel,
        grid=(n_j,),
        in_specs=[
            _resident((nb, SAMPLE_T, D_MODEL), lambda j: (0, 0, 0)),
            _resident((N_META, D_MODEL), lambda j: (0, 0)),
            pl.BlockSpec((1, D_MODEL), lambda j: (0, 0)),
            wspec(0), wspec(1), wspec(2),
            pl.BlockSpec((3, bn), lambda j: (0, j)),
            pl.BlockSpec((CONV_HIST * nb, bn), lambda j: (0, j)),
        ],
        out_specs=[
            pl.BlockSpec((S_ROWS, bn), lambda j: (0, j)),
            pl.BlockSpec((CONV_HIST * nb, bn), lambda j: (0, j)),
            pl.BlockSpec((TAIL, bn), lambda j: (0, j)),
            wout, wout, wout,
            pl.BlockSpec((S_ROWS, D_MODEL), lambda j: (0, 0)),
        ],
        out_shape=[
            jax.ShapeDtypeStruct((S_ROWS, D_MODEL), BF16),
            jax.ShapeDtypeStruct((CONV_HIST * nb, D_MODEL), F32),
            jax.ShapeDtypeStruct((TAIL, D_MODEL), F32),
            wshape, wshape, wshape,
            jax.ShapeDtypeStruct((S_ROWS, D_MODEL), F32),
        ],
        scratch_shapes=[
            pltpu.VMEM((S_ROWS, D_MODEL), BF16),
            pltpu.VMEM((CONV_HIST * nb + N_SAMPLE_ROWS, bn), F32),
            pltpu.VMEM((TAIL + nm, bn), F32),
        ],
        compiler_params=_params(1),
        name="mix_conv_s",
    )(x_sample, meta, g, w_in, w_in, w_in, w_dw, hc)


def _proj_res_body(norm, k, n_k, x_ref, w_ref, h_ref, g_ref, o_ref, m_ref):
    if n_k == 1:
        acc = h_ref[...] + _dot(x_ref[...], w_ref[...])
        o_ref[...] = _rms(acc, g_ref[...]) if norm == "final" else acc
        if norm == "emit":
            m_ref[...] = _rms(acc, g_ref[...]).astype(BF16)
        return

    @pl.when(k == 0)
    def _():
        o_ref[...] = h_ref[...] + _dot(x_ref[...], w_ref[...])

    @pl.when(k != 0)
    def _():
        o_ref[...] += _dot(x_ref[...], w_ref[...])

    if norm is not None:
        @pl.when(k == n_k - 1)
        def _():
            y = _rms(o_ref[...], g_ref[...])
            if norm == "final":
                o_ref[...] = y
            else:
                m_ref[...] = y.astype(BF16)


def _proj_res_p_kernel(norm, n_k, x_ref, w_ref, h_ref, g_ref, o_ref, m_ref=None):
    _proj_res_body(norm, pl.program_id(1), n_k, x_ref, w_ref, h_ref, g_ref, o_ref, m_ref)


def _proj_res_p(x, w, h, g, bm, bk, norm, name):
    rows, kdim = x.shape
    n_i, n_k = rows // bm, kdim // bk
    wspec = _resident if n_k == 1 else pl.BlockSpec
    row_spec = pl.BlockSpec((bm, D_MODEL), lambda i, k: (i, 0))
    out_specs, out_shape = [row_spec], [jax.ShapeDtypeStruct((rows, D_MODEL), F32)]
    if norm == "emit":
        out_specs.append(row_spec)
        out_shape.append(jax.ShapeDtypeStruct((rows, D_MODEL), BF16))
    return pl.pallas_call(
        functools.partial(_proj_res_p_kernel, norm, n_k),
        grid=(n_i, n_k),
        in_specs=[
            pl.BlockSpec((bm, bk), lambda i, k: (i, k)),
            wspec((bk, D_MODEL), lambda i, k: (k, 0)),
            row_spec,
            pl.BlockSpec((1, D_MODEL), lambda i, k: (0, 0)),
        ],
        out_specs=out_specs,
        out_shape=out_shape,
        compiler_params=_params(2),
        name=name,
    )(x, w, h, g)


def _swiglu_chunks(m, wg_ref, wu_ref, a_ref):
    for c0 in range(0, a_ref.shape[1], MXU_COLS):
        cols = slice(c0, c0 + MXU_COLS)
        gate = _dot(m, wg_ref[:, cols])
        up = _dot(m, wu_ref[:, cols])
        a_ref[:, cols] = (gate * (1.0 / (1.0 + jnp.exp(-gate))) * up).astype(BF16)


def _gate_up_p_kernel(n_cast, m_ref, wg_ref, wu_ref, *refs):
    _run_casts(refs[:n_cast], refs[n_cast + 1:])
    _swiglu_chunks(m_ref[...], wg_ref, wu_ref, refs[n_cast])


def _ffn_s_kernel(sample_out, n_pre, n_j, *refs):
    h_ref, refs = refs[0], refs[1:]
    if n_pre:
        (x_ref, wpre_ref), refs = refs[:2], refs[2:]
    g_ref, gfin_ref, wg_ref, wu_ref, wd_ref, o_ref, m_sc, a_sc = refs[:8]
    rest = list(refs[8:])
    hsrc = rest.pop(0) if n_pre else h_ref
    acc = rest.pop(0) if sample_out else o_ref
    s = pl.program_id(0)
    j = s - n_pre

    if n_pre:
        @pl.when(s < n_pre)
        def _():
            _proj_res_body(None, s, n_pre, x_ref, wpre_ref, h_ref, None, hsrc, None)

    @pl.when(s >= n_pre)
    def _():
        @pl.when(j == 0)
        def _():
            m_sc[...] = _rms(hsrc[...], g_ref[...]).astype(BF16)

        _swiglu_chunks(m_sc[...], wg_ref, wu_ref, a_sc)
        _proj_res_body(None, j, n_j, a_sc, wd_ref, hsrc, None, acc, None)

        if sample_out:
            @pl.when(j == n_j - 1)
            def _():
                y = _rms(acc[0:N_SAMPLE_ROWS, :], gfin_ref[...])
                for t in range(SAMPLE_T):
                    o_ref[:, t, :] = y[t * N_SAMPLE_SEQ:(t + 1) * N_SAMPLE_SEQ, :]


def _ffn_s(h, g, g_fin, w_gu, w_dn, bn, sample_out, name, pre=None):
    rows = h.shape[0]
    n_j = D_FF // bn
    n_pre = 0 if pre is None else pre[0].shape[1] // pre[2]
    ffn_step = lambda s: jnp.maximum(s - n_pre, 0)
    row_spec = pl.BlockSpec((1, D_MODEL), lambda s: (0, 0))
    o_shape = (N_SAMPLE_SEQ, SAMPLE_T, D_MODEL) if sample_out else (rows, D_MODEL)
    o_index = (0,) * len(o_shape)
    in_specs, args = [_resident((rows, D_MODEL), lambda s: (0, 0))], [h]
    scratch = [pltpu.VMEM((rows, D_MODEL), BF16), pltpu.VMEM((rows, bn), BF16)]
    if n_pre:
        x, w_pre, bk = pre
        pre_step = lambda s: jnp.minimum(s, n_pre - 1)
        in_specs += [pl.BlockSpec((rows, bk), lambda s: (0, pre_step(s))),
                     pl.BlockSpec((bk, D_MODEL), lambda s: (pre_step(s), 0))]
        args += [x, w_pre]
        scratch.append(pltpu.VMEM((rows, D_MODEL), F32))
    if sample_out:
        scratch.append(pltpu.VMEM((rows, D_MODEL), F32))
    in_specs += [
        row_spec, row_spec,
        pl.BlockSpec((D_MODEL, bn), lambda s: (0, ffn_step(s))),
        pl.BlockSpec((D_MODEL, bn), lambda s: (0, n_j + ffn_step(s))),
        pl.BlockSpec((bn, D_MODEL), lambda s: (ffn_step(s), 0)),
    ]
    return pl.pallas_call(
        functools.partial(_ffn_s_kernel, sample_out, n_pre, n_j),
        grid=(n_pre + n_j,),
        in_specs=in_specs,
        out_specs=pl.BlockSpec(o_shape, lambda s: o_index),
        out_shape=jax.ShapeDtypeStruct(o_shape, F32),
        scratch_shapes=scratch,
        compiler_params=_params(1),
        name=name,
    )(*args, g, g_fin, w_gu, w_gu, w_dn)


def _gate_up_p(m, w_gu, bm, bn, casts, name):
    rows = m.shape[0]
    n_i, n_j = rows // bm, D_FF // bn
    return pl.pallas_call(
        functools.partial(_gate_up_p_kernel, len(casts)),
        grid=(n_i, n_j),
        in_specs=[
            pl.BlockSpec((bm, D_MODEL), lambda i, j: (i, 0)),
            pl.BlockSpec((D_MODEL, bn), lambda i, j: (0, j)),
            pl.BlockSpec((D_MODEL, bn), lambda i, j: (0, n_j + j)),
        ] + [c.in_spec for c in casts],
        out_specs=[pl.BlockSpec((bm, bn), lambda i, j: (i, j))] + [c.out_spec for c in casts],
        out_shape=[jax.ShapeDtypeStruct((rows, D_FF), BF16)] + [c.out_shape for c in casts],
        compiler_params=_params(2),
        name=name,
    )(m, w_gu, w_gu, *[c.src for c in casts])


def _pool_p_kernel(tiles_per_seq, h_ref, halo_p_ref, halo_s_ref, g_ref, wp_ref, sc_ref, gnext_ref,
                   o_ref, ntail_ref, m_ref, nbuf, pbuf, qbuf):
    i = pl.program_id(0)
    bm = h_ref.shape[0]
    g = g_ref[...]
    x = h_ref[...]
    n = _rms(x, g)
    rows = HALO + bm
    r0 = FRONT + HALO

    for buf in (nbuf, pbuf, qbuf):
        buf[0:FRONT, :] = jnp.zeros((FRONT, buf.shape[1]), F32)

    @pl.when(i % tiles_per_seq == 0)
    def _():
        nbuf[FRONT:r0, :] = _rms(halo_s_ref[...], g)

    @pl.when(i % tiles_per_seq != 0)
    def _():
        nbuf[FRONT:r0, :] = _rms(halo_p_ref[...], g)

    nbuf[r0:r0 + bm, :] = n
    ntail_ref[...] = n[bm - HALO:, :]
    for gi, win in enumerate(POOL_WINDOWS):
        cols = slice(gi * POOL_GROUP, (gi + 1) * POOL_GROUP)
        src, shift, level = nbuf, 1, 0
        while shift < win:
            dst = (pbuf, qbuf)[level % 2]
            dst[FRONT:FRONT + rows, cols] = (src[FRONT:FRONT + rows, cols]
                                             + src[FRONT - shift:FRONT - shift + rows, cols])
            src, shift, level = dst, 2 * shift, level + 1
        ng = n[:, cols]
        p = src[r0:r0 + bm, cols] * (1.0 / win) - ng
        y = _dot(p.astype(BF16), wp_ref[gi].astype(BF16)) * sc_ref[:, cols]
        o_ref[:, cols] = x[:, cols] + y
    m_ref[...] = _rms(o_ref[...], gnext_ref[...]).astype(BF16)


def _pool_p(h_p, h_s, g, wp, scale, g_next, bm):
    rows = h_p.shape[0]
    n_i = rows // bm
    ng = len(POOL_WINDOWS)
    return pl.pallas_call(
        functools.partial(_pool_p_kernel, SEQ // bm),
        grid=(n_i,),
        in_specs=[
            pl.BlockSpec((bm, D_MODEL), lambda i: (i, 0)),
            pl.BlockSpec((HALO, D_MODEL), lambda i: (jnp.maximum(i * (bm // HALO) - 1, 0), 0)),
            pl.BlockSpec((HALO, D_MODEL), lambda i: (META_ROW0 // HALO, 0)),
            pl.BlockSpec((1, D_MODEL), lambda i: (0, 0)),
            pl.BlockSpec((None, ng, POOL_GROUP, POOL_GROUP), lambda i: (0, 0, 0, 0)),
            pl.BlockSpec((1, D_MODEL), lambda i: (0, 0)),
            pl.BlockSpec((1, D_MODEL), lambda i: (0, 0)),
        ],
        out_specs=[
            pl.BlockSpec((bm, D_MODEL), lambda i: (i, 0)),
            pl.BlockSpec((HALO, D_MODEL), lambda i: (i, 0)),
            pl.BlockSpec((bm, D_MODEL), lambda i: (i, 0)),
        ],
        out_shape=[
            jax.ShapeDtypeStruct((rows, D_MODEL), F32),
            jax.ShapeDtypeStruct((n_i * HALO, D_MODEL), F32),
            jax.ShapeDtypeStruct((rows, D_MODEL), BF16),
        ],
        scratch_shapes=[pltpu.VMEM((FRONT + HALO + bm, D_MODEL), F32)] * 3,
        compiler_params=_params(1),
        name="pool_p",
    )(h_p, h_p, h_s, g, wp, scale, g_next)


def _pool_s_kernel(hfull_ref, hcol_ref, hp_ref, g_ref, wp_ref, sc_ref, o_ref, tail_ref, inv_sc, nbuf, sum_sc):
    j = pl.program_id(0)
    ns = N_SAMPLE_ROWS
    nb = N_SAMPLE_SEQ
    nh = POOL_HIST * nb

    @pl.when(j == 0)
    def _():
        xf = hfull_ref[...]
        inv_sc[...] = lax.rsqrt(jnp.mean(xf * xf, axis=-1, keepdims=True) + EPS)

    x = hcol_ref[...]
    n = ((x * inv_sc[...]) * g_ref[...])[0:ns, :]
    nbuf[0:nh, :] = hp_ref[...]
    nbuf[nh:nh + ns, :] = n
    tail_ref[...] = nbuf[ns:ns + nh, :]

    for gi, win in enumerate(POOL_WINDOWS):
        @pl.when(j == gi)
        def _(win=win):
            acc = n
            for k in range(1, win):
                acc = acc + nbuf[nh - k * nb:nh - k * nb + ns, :]
            sum_sc[...] = acc * (1.0 / win)

    p = sum_sc[...] - n
    y = _dot(p.astype(BF16), wp_ref[...].astype(BF16)) * sc_ref[...]
    o_ref[0:ns, :] = x[0:ns, :] + y
    o_ref[ns:, :] = x[ns:, :]


def _pool_s(h_s, state_pool, g, wp, scale):
    ns = N_SAMPLE_ROWS
    nb = N_SAMPLE_SEQ
    nh = POOL_HIST * nb
    pg = POOL_GROUP
    n_g = len(POOL_WINDOWS)
    state_spec = pl.BlockSpec((nh, pg), lambda j: (0, j))
    return pl.pallas_call(
        _pool_s_kernel,
        grid=(n_g,),
        in_specs=[
            _resident((S_ROWS, D_MODEL), lambda j: (0, 0)),
            pl.BlockSpec((S_ROWS, pg), lambda j: (0, j)),
            state_spec,
            pl.BlockSpec((1, pg), lambda j: (0, j)),
            pl.BlockSpec((None, None, pg, pg), lambda j: (0, j, 0, 0)),
            pl.BlockSpec((1, pg), lambda j: (0, j)),
        ],
        out_specs=[pl.BlockSpec((S_ROWS, pg), lambda j: (0, j)), state_spec],
        out_shape=[
            jax.ShapeDtypeStruct((S_ROWS, D_MODEL), F32),
            jax.ShapeDtypeStruct((nh, D_MODEL), F32),
        ],
        scratch_shapes=[
            pltpu.VMEM((S_ROWS, 1), F32),
            pltpu.VMEM((nh + ns, pg), F32),
            pltpu.VMEM((ns, pg), F32),
        ],
        compiler_params=_params(1),
        name="pool_s",
    )(h_s, h_s, state_pool, g, wp, scale)


BM_P = 1024
BN_MIX = 512
BM_CONV_OUT = 512
BM_GATE_UP = 2048
BN_P = 512
BM_DOWN = 512
BK_DOWN = D_FF
BM_POOL = 512
BN_S = 256
BN_S_FF = 512
BK_S = 512

assert SEQ % BM_P == 0 and SEQ % BM_POOL == 0 and BM_POOL % HALO == 0 and META_ROW0 % HALO == 0
assert D_MODEL % BN_MIX == 0 and D_MODEL % BN_S == 0 and BN_MIX % MXU_COLS == 0 and BN_S % MXU_COLS == 0
assert D_FF % BN_P == 0 and D_FF % BN_S_FF == 0 and D_FF % BK_S == 0 and D_MODEL % BK_S == 0


def kernel(x_prompt, x_sample, state_conv, state_pool, meta_tokens, norm_mix, norm_ffn, norm_final,
           conv_w_in, conv_w_dw, conv_w_out, pool_w, pool_scale, ffn_w_gate_up, ffn_w_down):
    d = D_MODEL
    nb, nt = N_SAMPLE_SEQ, SAMPLE_T
    batch = x_prompt.shape[0]
    assert x_prompt.shape == (batch, SEQ, d) and x_sample.shape == (nb, nt, d)
    assert state_conv.shape == (1, nb, CONV_HIST, d) and state_pool.shape == (1, nb, POOL_HIST, d)
    assert meta_tokens.shape == (N_META, d) and ffn_w_gate_up.shape == (2, d, 2 * D_FF)
    assert (batch * SEQ) % BM_GATE_UP == 0 and (batch * SEQ) % BM_DOWN == 0 and (batch * SEQ) % BM_CONV_OUT == 0

    h_p = x_prompt.reshape(batch * SEQ, d)
    hc = state_conv[0].transpose(1, 0, 2).reshape(CONV_HIST * nb, d)
    hp = state_pool[0].transpose(1, 0, 2).reshape(POOL_HIST * nb, d)
    row = lambda v: v.reshape(1, d)
    g_mix, g_ffn, g_fin = norm_mix, norm_ffn, row(norm_final)

    rows_p = batch * SEQ
    n_j_mix, n_j_ff = d // BN_MIX, D_FF // BN_P
    n_mix, n_ff = (rows_p // BM_P) * n_j_mix, (rows_p // BM_GATE_UP) * n_j_ff
    step_mix = lambda i, j: i * n_j_mix + j
    step_ff = lambda i, j: i * n_j_ff + j
    casts_mix = [_cast_job(conv_w_out, 0, n_mix, 0, step_mix), _cast_job(ffn_w_gate_up, 0, n_mix, 0, step_mix)]
    casts_ff0 = [_cast_job(ffn_w_down, 0, n_ff, 0, step_ff), _cast_job(ffn_w_gate_up, 1, n_ff, 1, step_ff)]
    casts_ff1 = [_cast_job(ffn_w_down, 1, n_ff, 0, step_ff)]

    z_s, uts, umeta, wb, wc, wv, h_s = _mix_conv_s(
        x_sample, meta_tokens, row(g_mix[0]), conv_w_in, conv_w_dw[0], hc, BN_S)
    z_p, utail_p, w_out, w_gu0 = _mix_conv_p(
        h_p, row(g_mix[0]), wb, wc, wv, conv_w_dw[0], umeta, BM_P, BN_MIX, casts_mix)
    h_p, m_p = _proj_res_p(z_p, w_out, h_p, row(g_ffn[0]), BM_CONV_OUT, D_MODEL, "emit", "conv_out_p")
    a_p, w_dn0, w_gu1 = _gate_up_p(m_p, w_gu0, BM_GATE_UP, BN_P, casts_ff0, "gate_up0_p")
    h_s = _ffn_s(h_s, row(g_ffn[0]), g_fin, w_gu0, w_dn0, BN_S_FF, False, "conv_out_ffn0_s", pre=(z_s, w_out, BK_S))
    (h_p,) = _proj_res_p(a_p, w_dn0, h_p, g_fin, BM_DOWN, BK_DOWN, None, "down0_p")
    h_p1, ntail_p, m_p = _pool_p(h_p, h_s, row(g_mix[1]), pool_w, row(pool_scale[0]), row(g_ffn[1]), BM_POOL)
    h_s1, pool_tail_s = _pool_s(h_s, hp, row(g_mix[1]), pool_w, row(pool_scale[0]))
    a_p, w_dn1 = _gate_up_p(m_p, w_gu1, BM_GATE_UP, BN_P, casts_ff1, "gate_up1_p")
    y_s = _ffn_s(h_s1, row(g_ffn[1]), g_fin, w_gu1, w_dn1, BN_S_FF, True, "ffn1_s")
    (y_p,) = _proj_res_p(a_p, w_dn1, h_p1, g_fin, BM_DOWN, BK_DOWN, "final", "down1_p")

    y_prompt = y_p.reshape(batch, SEQ, d)
    y_sample = y_s
    tps = SEQ // BM_P
    new_conv_prompt = utail_p.reshape(batch, tps, TAIL, d)[:, tps - 1, TAIL - CONV_HIST:, :][None]
    tpp = SEQ // BM_POOL
    new_pool_prompt = ntail_p.reshape(batch, tpp, HALO, d)[:, tpp - 1, HALO - POOL_HIST:, :][None]
    new_conv_sample = uts.reshape(CONV_HIST, nb, d).transpose(1, 0, 2)[None]
    new_pool_sample = pool_tail_s.reshape(POOL_HIST, nb, d).transpose(1, 0, 2)[None]
    return (y_prompt, y_sample, new_conv_prompt, new_pool_prompt, new_conv_sample, new_pool_sample)
```

```python
import functools
from typing import NamedTuple

import jax
import jax.numpy as jnp
from jax import lax
from jax.experimental import pallas as pl
from jax.experimental.pallas import tpu as pltpu

D_MODEL = 2048
D_FF = 5632
N_META = 16
SEQ = 2048
N_SAMPLE_SEQ = 128
SAMPLE_T = 8
CONV_HIST = 2
POOL_WINDOWS = (2, 4, 8, 16)
POOL_GROUP = D_MODEL // len(POOL_WINDOWS)
POOL_HIST = 15
EPS = 1e-6

N_SAMPLE_ROWS = N_SAMPLE_SEQ * SAMPLE_T
S_ROWS = N_SAMPLE_ROWS + N_META
META_ROW0 = N_SAMPLE_ROWS
TAIL = 8
HALO = 16
MXU_COLS = 256
FRONT = 8

V7X_SCOPED_VMEM_BYTES = 60000 * 1024

BF16 = jnp.bfloat16
F32 = jnp.float32


def _rms(x, g):
    ms = jnp.mean(x * x, axis=-1, keepdims=True)
    return (x * lax.rsqrt(ms + EPS)) * g


def _dot(a, b):
    return jnp.dot(a, b, preferred_element_type=F32)


def _params(n_axes):
    return pltpu.CompilerParams(
        dimension_semantics=("arbitrary",) * n_axes,
        vmem_limit_bytes=V7X_SCOPED_VMEM_BYTES,
    )


def _resident(shape, index_map):
    return pl.BlockSpec(shape, index_map, pipeline_mode=pl.Buffered(1))


class _CastJob(NamedTuple):
    src: jax.Array
    in_spec: pl.BlockSpec
    out_spec: pl.BlockSpec
    out_shape: jax.ShapeDtypeStruct


def _cast_job(w_stack, layer, n_steps, axis, step_of):
    _, r, c = w_stack.shape
    if axis == 0:
        blk = (r // n_steps, c)
        in_idx = lambda *ids: (layer, step_of(*ids), 0)
        out_idx = lambda *ids: (step_of(*ids), 0)
    else:
        blk = (r, c // n_steps)
        in_idx = lambda *ids: (layer, 0, step_of(*ids))
        out_idx = lambda *ids: (0, step_of(*ids))
    assert blk[0] * (n_steps if axis == 0 else 1) == r and blk[1] * (n_steps if axis == 1 else 1) == c
    return _CastJob(w_stack, pl.BlockSpec((None,) + blk, in_idx), pl.BlockSpec(blk, out_idx),
                    jax.ShapeDtypeStruct((r, c), BF16))


def _run_casts(src_refs, dst_refs):
    for src, dst in zip(src_refs, dst_refs):
        dst[...] = src[...].astype(BF16)


def _conv3(w, u, ubuf, off):
    t = u.shape[0]
    return (w[2:3] * u + w[1:2] * ubuf[off - 1:off - 1 + t, :] + w[0:1] * ubuf[off - 2:off - 2 + t, :])


def _mix_p_kernel(tiles_per_seq, n_cast, *refs):
    h_ref, g_ref, wb_ref, wc_ref, wv_ref, wdw_ref, umeta_ref = refs[:7]
    cast_src, refs = refs[7:7 + n_cast], refs[7 + n_cast:]
    z_ref, utail_ref = refs[:2]
    cast_dst, (n_sc, ubuf, carry) = refs[2:2 + n_cast], refs[2 + n_cast:]
    i = pl.program_id(0)
    j = pl.program_id(1)
    bm = h_ref.shape[0]
    _run_casts(cast_src, cast_dst)

    @pl.when(j == 0)
    def _():
        n_sc[...] = _rms(h_ref[...], g_ref[...]).astype(BF16)

    @pl.when(i % tiles_per_seq == 0)
    def _():
        ubuf[0:TAIL, :] = umeta_ref[...]

    @pl.when(i % tiles_per_seq != 0)
    def _():
        ubuf[0:TAIL, :] = carry[j]

    n = n_sc[...]
    for c0 in range(0, z_ref.shape[1], MXU_COLS):
        cols = slice(c0, c0 + MXU_COLS)
        b = _dot(n, wb_ref[:, cols])
        u = _dot(n, wc_ref[:, cols]) * _dot(n, wv_ref[:, cols])
        ubuf[TAIL:TAIL + bm, cols] = u
        z_ref[:, cols] = (b * _conv3(wdw_ref[:, cols], u, ubuf.at[:, cols], TAIL)).astype(BF16)
    tail = ubuf[bm:bm + TAIL, :]
    carry[j] = tail
    utail_ref[...] = tail


def _mix_s_kernel(xs_ref, meta_ref, g_ref, wb_ref, wc_ref, wv_ref, wdw_ref, hc_ref,
                  z_ref, uts_ref, umeta_ref, wb_o, wc_o, wv_o, h0_ref, n_sc, ubuf, mbuf):
    j = pl.program_id(0)
    ns = N_SAMPLE_ROWS
    nb = N_SAMPLE_SEQ
    nm = S_ROWS - ns

    @pl.when(j == 0)
    def _():
        for t in range(SAMPLE_T):
            h0_ref[t * nb:(t + 1) * nb, :] = xs_ref[:, t, :]
        h0_ref[ns:, :] = meta_ref[...]
        n_sc[...] = _rms(h0_ref[...], g_ref[...]).astype(BF16)

    wb_o[...] = wb_ref[...].astype(BF16)
    wc_o[...] = wc_ref[...].astype(BF16)
    wv_o[...] = wv_ref[...].astype(BF16)
    n = n_sc[...]
    nh = CONV_HIST * nb
    ubuf[0:nh, :] = hc_ref[...]
    mbuf[0:TAIL, :] = jnp.zeros((TAIL, mbuf.shape[1]), F32)
    for c0 in range(0, z_ref.shape[1], MXU_COLS):
        cols = slice(c0, c0 + MXU_COLS)
        b = _dot(n, wb_o[:, cols])
        u = _dot(n, wc_o[:, cols]) * _dot(n, wv_o[:, cols])
        w = wdw_ref[:, cols]

        us = u[0:ns, :]
        ubuf[nh:nh + ns, cols] = us
        conv_s = (w[2:3] * us + w[1:2] * ubuf[nb:nb + ns, cols] + w[0:1] * ubuf[0:ns, cols])
        z_ref[0:ns, cols] = (b[0:ns, :] * conv_s).astype(BF16)

        um = u[ns:, :]
        mbuf[TAIL:TAIL + nm, cols] = um
        z_ref[ns:, cols] = (b[ns:, :] * _conv3(w, um, mbuf.at[:, cols], TAIL)).astype(BF16)
    uts_ref[...] = ubuf[ns:ns + nh, :]
    umeta_ref[...] = mbuf[N_META:N_META + TAIL, :]


def _mix_conv_p(h, g, wb, wc, wv, w_dw, umeta, bm, bn, casts):
    rows = h.shape[0]
    n_i, n_j = rows // bm, D_MODEL // bn
    wspec = pl.BlockSpec((D_MODEL, bn), lambda i, j: (0, j))
    return pl.pallas_call(
        functools.partial(_mix_p_kernel, SEQ // bm, len(casts)),
        grid=(n_i, n_j),
        in_specs=[
            pl.BlockSpec((bm, D_MODEL), lambda i, j: (i, 0)),
            pl.BlockSpec((1, D_MODEL), lambda i, j: (0, 0)),
            wspec, wspec, wspec,
            pl.BlockSpec((3, bn), lambda i, j: (0, j)),
            pl.BlockSpec((TAIL, bn), lambda i, j: (0, j)),
        ] + [c.in_spec for c in casts],
        out_specs=[
            pl.BlockSpec((bm, bn), lambda i, j: (i, j)),
            pl.BlockSpec((TAIL, bn), lambda i, j: (i, j)),
        ] + [c.out_spec for c in casts],
        out_shape=[
            jax.ShapeDtypeStruct((rows, D_MODEL), BF16),
            jax.ShapeDtypeStruct((n_i * TAIL, D_MODEL), F32),
        ] + [c.out_shape for c in casts],
        scratch_shapes=[
            pltpu.VMEM((bm, D_MODEL), BF16),
            pltpu.VMEM((TAIL + bm, bn), F32),
            pltpu.VMEM((n_j, TAIL, bn), F32),
        ],
        compiler_params=_params(2),
        name="mix_conv_p",
    )(h, g, wb, wc, wv, w_dw, umeta, *[c.src for c in casts])


def _mix_conv_s(x_sample, meta, g, w_in, w_dw, hc, bn):
    n_j = D_MODEL // bn
    nb = N_SAMPLE_SEQ
    nm = S_ROWS - N_SAMPLE_ROWS
    wspec = lambda part: pl.BlockSpec((None, D_MODEL, bn), lambda j: (0, 0, part * n_j + j))
    wout = pl.BlockSpec((D_MODEL, bn), lambda j: (0, j))
    wshape = jax.ShapeDtypeStruct((D_MODEL, D_MODEL), BF16)
    return pl.pallas_call(
        _mix_s_kernel,
        grid=(n_j,),
        in_specs=[
            _resident((nb, SAMPLE_T, D_MODEL), lambda j: (0, 0, 0)),
            _resident((N_META, D_MODEL), lambda j: (0, 0)),
            pl.BlockSpec((1, D_MODEL), lambda j: (0, 0)),
            wspec(0), wspec(1), wspec(2),
            pl.BlockSpec((3, bn), lambda j: (0, j)),
            pl.BlockSpec((CONV_HIST * nb, bn), lambda j: (0, j)),
        ],
        out_specs=[
            pl.BlockSpec((S_ROWS, bn), lambda j: (0, j)),
            pl.BlockSpec((CONV_HIST * nb, bn), lambda j: (0, j)),
            pl.BlockSpec((TAIL, bn), lambda j: (0, j)),
            wout, wout, wout,
            pl.BlockSpec((S_ROWS, D_MODEL), lambda j: (0, 0)),
        ],
        out_shape=[
            jax.ShapeDtypeStruct((S_ROWS, D_MODEL), BF16),
            jax.ShapeDtypeStruct((CONV_HIST * nb, D_MODEL), F32),
            jax.ShapeDtypeStruct((TAIL, D_MODEL), F32),
            wshape, wshape, wshape,
            jax.ShapeDtypeStruct((S_ROWS, D_MODEL), F32),
        ],
        scratch_shapes=[
            pltpu.VMEM((S_ROWS, D_MODEL), BF16),
            pltpu.VMEM((CONV_HIST * nb + N_SAMPLE_ROWS, bn), F32),
            pltpu.VMEM((TAIL + nm, bn), F32),
        ],
        compiler_params=_params(1),
        name="mix_conv_s",
    )(x_sample, meta, g, w_in, w_in, w_in, w_dw, hc)


def _proj_res_body(norm, k, n_k, x_ref, w_ref, h_ref, g_ref, o_ref, m_ref):
    if n_k == 1:
        acc = h_ref[...] + _dot(x_ref[...], w_ref[...])
        o_ref[...] = _rms(acc, g_ref[...]) if norm == "final" else acc
        if norm == "emit":
            m_ref[...] = _rms(acc, g_ref[...]).astype(BF16)
        return

    @pl.when(k == 0)
    def _():
        o_ref[...] = h_ref[...] + _dot(x_ref[...], w_ref[...])

    @pl.when(k != 0)
    def _():
        o_ref[...] += _dot(x_ref[...], w_ref[...])

    if norm is not None:
        @pl.when(k == n_k - 1)
        def _():
            y = _rms(o_ref[...], g_ref[...])
            if norm == "final":
                o_ref[...] = y
            else:
                m_ref[...] = y.astype(BF16)


def _proj_res_p_kernel(norm, n_k, x_ref, w_ref, h_ref, g_ref, o_ref, m_ref=None):
    _proj_res_body(norm, pl.program_id(1), n_k, x_ref, w_ref, h_ref, g_ref, o_ref, m_ref)


def _proj_res_p(x, w, h, g, bm, bk, norm, name):
    rows, kdim = x.shape
    n_i, n_k = rows // bm, kdim // bk
    wspec = _resident if n_k == 1 else pl.BlockSpec
    row_spec = pl.BlockSpec((bm, D_MODEL), lambda i, k: (i, 0))
    out_specs, out_shape = [row_spec], [jax.ShapeDtypeStruct((rows, D_MODEL), F32)]
    if norm == "emit":
        out_specs.append(row_spec)
        out_shape.append(jax.ShapeDtypeStruct((rows, D_MODEL), BF16))
    return pl.pallas_call(
        functools.partial(_proj_res_p_kernel, norm, n_k),
        grid=(n_i, n_k),
        in_specs=[
            pl.BlockSpec((bm, bk), lambda i, k: (i, k)),
            wspec((bk, D_MODEL), lambda i, k: (k, 0)),
            row_spec,
            pl.BlockSpec((1, D_MODEL), lambda i, k: (0, 0)),
        ],
        out_specs=out_specs,
        out_shape=out_shape,
        compiler_params=_params(2),
        name=name,
    )(x, w, h, g)


def _swiglu_chunks(m, wg_ref, wu_ref, a_ref):
    for c0 in range(0, a_ref.shape[1], MXU_COLS):
        cols = slice(c0, c0 + MXU_COLS)
        gate = _dot(m, wg_ref[:, cols])
        up = _dot(m, wu_ref[:, cols])
        a_ref[:, cols] = (gate * (1.0 / (1.0 + jnp.exp(-gate))) * up).astype(BF16)


def _gate_up_p_kernel(n_cast, m_ref, wg_ref, wu_ref, *refs):
    _run_casts(refs[:n_cast], refs[n_cast + 1:])
    _swiglu_chunks(m_ref[...], wg_ref, wu_ref, refs[n_cast])


def _ffn_s_kernel(sample_out, n_pre, n_j, *refs):
    h_ref, refs = refs[0], refs[1:]
    if n_pre:
        (x_ref, wpre_ref), refs = refs[:2], refs[2:]
    g_ref, gfin_ref, wg_ref, wu_ref, wd_ref, o_ref, m_sc, a_sc = refs[:8]
    rest = list(refs[8:])
    hsrc = rest.pop(0) if n_pre else h_ref
    acc = rest.pop(0) if sample_out else o_ref
    s = pl.program_id(0)
    j = s - n_pre

    if n_pre:
        @pl.when(s < n_pre)
        def _():
            _proj_res_body(None, s, n_pre, x_ref, wpre_ref, h_ref, None, hsrc, None)

    @pl.when(s >= n_pre)
    def _():
        @pl.when(j == 0)
        def _():
            m_sc[...] = _rms(hsrc[...], g_ref[...]).astype(BF16)

        _swiglu_chunks(m_sc[...], wg_ref, wu_ref, a_sc)
        _proj_res_body(None, j, n_j, a_sc, wd_ref, hsrc, None, acc, None)

        if sample_out:
            @pl.when(j == n_j - 1)
            def _():
                y = _rms(acc[0:N_SAMPLE_ROWS, :], gfin_ref[...])
                for t in range(SAMPLE_T):
                    o_ref[:, t, :] = y[t * N_SAMPLE_SEQ:(t + 1) * N_SAMPLE_SEQ, :]


def _ffn_s(h, g, g_fin, w_gu, w_dn, bn, sample_out, name, pre=None):
    rows = h.shape[0]
    n_j = D_FF // bn
    n_pre = 0 if pre is None else pre[0].shape[1] // pre[2]
    ffn_step = lambda s: jnp.maximum(s - n_pre, 0)
    row_spec = pl.BlockSpec((1, D_MODEL), lambda s: (0, 0))
    o_shape = (N_SAMPLE_SEQ, SAMPLE_T, D_MODEL) if sample_out else (rows, D_MODEL)
    o_index = (0,) * len(o_shape)
    in_specs, args = [_resident((rows, D_MODEL), lambda s: (0, 0))], [h]
    scratch = [pltpu.VMEM((rows, D_MODEL), BF16), pltpu.VMEM((rows, bn), BF16)]
    if n_pre:
        x, w_pre, bk = pre
        pre_step = lambda s: jnp.minimum(s, n_pre - 1)
        in_specs += [pl.BlockSpec((rows, bk), lambda s: (0, pre_step(s))),
                     pl.BlockSpec((bk, D_MODEL), lambda s: (pre_step(s), 0))]
        args += [x, w_pre]
        scratch.append(pltpu.VMEM((rows, D_MODEL), F32))
    if sample_out:
        scratch.append(pltpu.VMEM((rows, D_MODEL), F32))
    in_specs += [
        row_spec, row_spec,
        pl.BlockSpec((D_MODEL, bn), lambda s: (0, ffn_step(s))),
        pl.BlockSpec((D_MODEL, bn), lambda s: (0, n_j + ffn_step(s))),
        pl.BlockSpec((bn, D_MODEL), lambda s: (ffn_step(s), 0)),
    ]
    return pl.pallas_call(
        functools.partial(_ffn_s_kernel, sample_out, n_pre, n_j),
        grid=(n_pre + n_j,),
        in_specs=in_specs,
        out_specs=pl.BlockSpec(o_shape, lambda s: o_index),
        out_shape=jax.ShapeDtypeStruct(o_shape, F32),
        scratch_shapes=scratch,
        compiler_params=_params(1),
        name=name,
    )(*args, g, g_fin, w_gu, w_gu, w_dn)


def _gate_up_p(m, w_gu, bm, bn, casts, name):
    rows = m.shape[0]
    n_i, n_j = rows // bm, D_FF // bn
    return pl.pallas_call(
        functools.partial(_gate_up_p_kernel, len(casts)),
        grid=(n_i, n_j),
        in_specs=[
            pl.BlockSpec((bm, D_MODEL), lambda i, j: (i, 0)),
            pl.BlockSpec((D_MODEL, bn), lambda i, j: (0, j)),
            pl.BlockSpec((D_MODEL, bn), lambda i, j: (0, n_j + j)),
        ] + [c.in_spec for c in casts],
        out_specs=[pl.BlockSpec((bm, bn), lambda i, j: (i, j))] + [c.out_spec for c in casts],
        out_shape=[jax.ShapeDtypeStruct((rows, D_FF), BF16)] + [c.out_shape for c in casts],
        compiler_params=_params(2),
        name=name,
    )(m, w_gu, w_gu, *[c.src for c in casts])


def _pool_p_kernel(tiles_per_seq, h_ref, halo_p_ref, halo_s_ref, g_ref, wp_ref, sc_ref, gnext_ref,
                   o_ref, ntail_ref, m_ref, nbuf, pbuf, qbuf):
    i = pl.program_id(0)
    bm = h_ref.shape[0]
    g = g_ref[...]
    x = h_ref[...]
    n = _rms(x, g)
    rows = HALO + bm
    r0 = FRONT + HALO

    for buf in (nbuf, pbuf, qbuf):
        buf[0:FRONT, :] = jnp.zeros((FRONT, buf.shape[1]), F32)

    @pl.when(i % tiles_per_seq == 0)
    def _():
        nbuf[FRONT:r0, :] = _rms(halo_s_ref[...], g)

    @pl.when(i % tiles_per_seq != 0)
    def _():
        nbuf[FRONT:r0, :] = _rms(halo_p_ref[...], g)

    nbuf[r0:r0 + bm, :] = n
    ntail_ref[...] = n[bm - HALO:, :]
    for gi, win in enumerate(POOL_WINDOWS):
        cols = slice(gi * POOL_GROUP, (gi + 1) * POOL_GROUP)
        src, shift, level = nbuf, 1, 0
        while shift < win:
            dst = (pbuf, qbuf)[level % 2]
            dst[FRONT:FRONT + rows, cols] = (src[FRONT:FRONT + rows, cols]
                                             + src[FRONT - shift:FRONT - shift + rows, cols])
            src, shift, level = dst, 2 * shift, level + 1
        ng = n[:, cols]
        p = src[r0:r0 + bm, cols] * (1.0 / win) - ng
        y = _dot(p.astype(BF16), wp_ref[gi].astype(BF16)) * sc_ref[:, cols]
        o_ref[:, cols] = x[:, cols] + y
    m_ref[...] = _rms(o_ref[...], gnext_ref[...]).astype(BF16)


def _pool_p(h_p, h_s, g, wp, scale, g_next, bm):
    rows = h_p.shape[0]
    n_i = rows // bm
    ng = len(POOL_WINDOWS)
    return pl.pallas_call(
        functools.partial(_pool_p_kernel, SEQ // bm),
        grid=(n_i,),
        in_specs=[
            pl.BlockSpec((bm, D_MODEL), lambda i: (i, 0)),
            pl.BlockSpec((HALO, D_MODEL), lambda i: (jnp.maximum(i * (bm // HALO) - 1, 0), 0)),
            pl.BlockSpec((HALO, D_MODEL), lambda i: (META_ROW0 // HALO, 0)),
            pl.BlockSpec((1, D_MODEL), lambda i: (0, 0)),
            pl.BlockSpec((None, ng, POOL_GROUP, POOL_GROUP), lambda i: (0, 0, 0, 0)),
            pl.BlockSpec((1, D_MODEL), lambda i: (0, 0)),
            pl.BlockSpec((1, D_MODEL), lambda i: (0, 0)),
        ],
        out_specs=[
            pl.BlockSpec((bm, D_MODEL), lambda i: (i, 0)),
            pl.BlockSpec((HALO, D_MODEL), lambda i: (i, 0)),
            pl.BlockSpec((bm, D_MODEL), lambda i: (i, 0)),
        ],
        out_shape=[
            jax.ShapeDtypeStruct((rows, D_MODEL), F32),
            jax.ShapeDtypeStruct((n_i * HALO, D_MODEL), F32),
            jax.ShapeDtypeStruct((rows, D_MODEL), BF16),
        ],
        scratch_shapes=[pltpu.VMEM((FRONT + HALO + bm, D_MODEL), F32)] * 3,
        compiler_params=_params(1),
        name="pool_p",
    )(h_p, h_p, h_s, g, wp, scale, g_next)


def _down_pool_kernel(tiles_per_seq, a_ref, w_ref, h_ref, meta_ref, g_ref, wp_ref, sc_ref, gnext_ref,
                      o_ref, ntail_ref, m_ref, h2buf, xbuf, ncarry, nbuf, pbuf, qbuf):
    s = pl.program_id(0)
    bm = h_ref.shape[0]
    slot = s % 2
    g = g_ref[...]
    rows = HALO + bm
    r0 = FRONT + HALO

    @pl.when(s == 0)
    def _():
        h2buf[...] = jnp.zeros(h2buf.shape, F32)
        ncarry[...] = jnp.zeros(ncarry.shape, F32)

    @pl.when((s - 1) % tiles_per_seq == 0)
    def _():
        nbuf[FRONT:r0, :] = _rms(meta_ref[...], g)

    @pl.when((s - 1) % tiles_per_seq != 0)
    def _():
        nbuf[FRONT:r0, :] = ncarry[...]

    for buf in (nbuf, pbuf, qbuf):
        buf[0:FRONT, :] = jnp.zeros((FRONT, buf.shape[1]), F32)
    xbuf[...] = h2buf[1 - slot]
    n = _rms(xbuf[...], g)
    nbuf[r0:r0 + bm, :] = n
    tail = n[bm - HALO:, :]
    ncarry[...] = tail
    ntail_ref[...] = tail
    for gi, win in enumerate(POOL_WINDOWS):
        cols = slice(gi * POOL_GROUP, (gi + 1) * POOL_GROUP)
        src, src_cols, shift, level = nbuf, cols, 1, 0
        while shift < win:
            dst = (pbuf, qbuf)[level % 2]
            dst[FRONT:FRONT + rows, :] = (src[FRONT:FRONT + rows, src_cols]
                                          + src[FRONT - shift:FRONT - shift + rows, src_cols])
            src, src_cols, shift, level = dst, slice(None), 2 * shift, level + 1
        p = src[r0:r0 + bm, src_cols] * (1.0 / win) - n[:, cols]
        y = _dot(p.astype(BF16), wp_ref[gi].astype(BF16)) * sc_ref[:, cols]
        o_ref[:, cols] = xbuf[:, cols] + y
    m_ref[...] = _rms(o_ref[...], gnext_ref[...]).astype(BF16)

    h2buf[slot] = h_ref[...] + _dot(a_ref[...], w_ref[...])


def _down_pool_p(a, w, h, h_s, g, wp, scale, g_next, bm):
    rows, kdim = a.shape
    n_t = rows // bm
    ng = len(POOL_WINDOWS)
    cur = lambda s: (jnp.minimum(s, n_t - 1), 0)
    prev = lambda s: (jnp.maximum(s - 1, 0), 0)
    vec = pl.BlockSpec((1, D_MODEL), lambda s: (0, 0))
    return pl.pallas_call(
        functools.partial(_down_pool_kernel, SEQ // bm),
        grid=(n_t + 1,),
        in_specs=[
            pl.BlockSpec((bm, kdim), cur),
            _resident((kdim, D_MODEL), lambda s: (0, 0)),
            pl.BlockSpec((bm, D_MODEL), cur),
            _resident((HALO, D_MODEL), lambda s: (META_ROW0 // HALO, 0)),
            vec,
            _resident((None, ng, POOL_GROUP, POOL_GROUP), lambda s: (0, 0, 0, 0)),
            vec, vec,
        ],
        out_specs=[
            pl.BlockSpec((bm, D_MODEL), prev),
            pl.BlockSpec((HALO, D_MODEL), prev),
            pl.BlockSpec((bm, D_MODEL), prev),
        ],
        out_shape=[
            jax.ShapeDtypeStruct((rows, D_MODEL), F32),
            jax.ShapeDtypeStruct((n_t * HALO, D_MODEL), F32),
            jax.ShapeDtypeStruct((rows, D_MODEL), BF16),
        ],
        scratch_shapes=[
            pltpu.VMEM((2, bm, D_MODEL), F32),
            pltpu.VMEM((bm, D_MODEL), F32),
            pltpu.VMEM((HALO, D_MODEL), F32),
            pltpu.VMEM((FRONT + HALO + bm, D_MODEL), F32),
            pltpu.VMEM((FRONT + HALO + bm, POOL_GROUP), F32),
            pltpu.VMEM((FRONT + HALO + bm, POOL_GROUP), F32),
        ],
        compiler_params=_params(1),
        name="down_pool_p",
    )(a, w, h, h_s, g, wp, scale, g_next)


def _pool_s_kernel(hfull_ref, hcol_ref, hp_ref, g_ref, wp_ref, sc_ref, o_ref, tail_ref, inv_sc, nbuf, sum_sc):
    j = pl.program_id(0)
    ns = N_SAMPLE_ROWS
    nb = N_SAMPLE_SEQ
    nh = POOL_HIST * nb

    @pl.when(j == 0)
    def _():
        xf = hfull_ref[...]
        inv_sc[...] = lax.rsqrt(jnp.mean(xf * xf, axis=-1, keepdims=True) + EPS)

    x = hcol_ref[...]
    n = ((x * inv_sc[...]) * g_ref[...])[0:ns, :]
    nbuf[0:nh, :] = hp_ref[...]
    nbuf[nh:nh + ns, :] = n
    tail_ref[...] = nbuf[ns:ns + nh, :]

    for gi, win in enumerate(POOL_WINDOWS):
        @pl.when(j == gi)
        def _(win=win):
            acc = n
            for k in range(1, win):
                acc = acc + nbuf[nh - k * nb:nh - k * nb + ns, :]
            sum_sc[...] = acc * (1.0 / win)

    p = sum_sc[...] - n
    y = _dot(p.astype(BF16), wp_ref[...].astype(BF16)) * sc_ref[...]
    o_ref[0:ns, :] = x[0:ns, :] + y
    o_ref[ns:, :] = x[ns:, :]


def _pool_s(h_s, state_pool, g, wp, scale):
    ns = N_SAMPLE_ROWS
    nb = N_SAMPLE_SEQ
    nh = POOL_HIST * nb
    pg = POOL_GROUP
    n_g = len(POOL_WINDOWS)
    state_spec = pl.BlockSpec((nh, pg), lambda j: (0, j))
    return pl.pallas_call(
        _pool_s_kernel,
        grid=(n_g,),
        in_specs=[
            _resident((S_ROWS, D_MODEL), lambda j: (0, 0)),
            pl.BlockSpec((S_ROWS, pg), lambda j: (0, j)),
            state_spec,
            pl.BlockSpec((1, pg), lambda j: (0, j)),
            pl.BlockSpec((None, None, pg, pg), lambda j: (0, j, 0, 0)),
            pl.BlockSpec((1, pg), lambda j: (0, j)),
        ],
        out_specs=[pl.BlockSpec((S_ROWS, pg), lambda j: (0, j)), state_spec],
        out_shape=[
            jax.ShapeDtypeStruct((S_ROWS, D_MODEL), F32),
            jax.ShapeDtypeStruct((nh, D_MODEL), F32),
        ],
        scratch_shapes=[
            pltpu.VMEM((S_ROWS, 1), F32),
            pltpu.VMEM((nh + ns, pg), F32),
            pltpu.VMEM((ns, pg), F32),
        ],
        compiler_params=_params(1),
        name="pool_s",
    )(h_s, h_s, state_pool, g, wp, scale)


BM_P = 1024
BN_MIX = 512
BM_CONV_OUT = 512
BM_GATE_UP = 2048
BN_P = 512
BM_DOWN = 512
BK_DOWN = D_FF
BM_POOL = 256
BN_S = 256
BN_S_FF = 512
BK_S = 512

assert SEQ % BM_P == 0 and SEQ % BM_POOL == 0 and BM_POOL % HALO == 0 and META_ROW0 % HALO == 0
assert D_MODEL % BN_MIX == 0 and D_MODEL % BN_S == 0 and BN_MIX % MXU_COLS == 0 and BN_S % MXU_COLS == 0
assert D_FF % BN_P == 0 and D_FF % BN_S_FF == 0 and D_FF % BK_S == 0 and D_MODEL % BK_S == 0


def kernel(x_prompt, x_sample, state_conv, state_pool, meta_tokens, norm_mix, norm_ffn, norm_final,
           conv_w_in, conv_w_dw, conv_w_out, pool_w, pool_scale, ffn_w_gate_up, ffn_w_down):
    d = D_MODEL
    nb, nt = N_SAMPLE_SEQ, SAMPLE_T
    batch = x_prompt.shape[0]
    assert x_prompt.shape == (batch, SEQ, d) and x_sample.shape == (nb, nt, d)
    assert state_conv.shape == (1, nb, CONV_HIST, d) and state_pool.shape == (1, nb, POOL_HIST, d)
    assert meta_tokens.shape == (N_META, d) and ffn_w_gate_up.shape == (2, d, 2 * D_FF)
    assert (batch * SEQ) % BM_GATE_UP == 0 and (batch * SEQ) % BM_DOWN == 0 and (batch * SEQ) % BM_CONV_OUT == 0

    h_p = x_prompt.reshape(batch * SEQ, d)
    hc = state_conv[0].transpose(1, 0, 2).reshape(CONV_HIST * nb, d)
    hp = state_pool[0].transpose(1, 0, 2).reshape(POOL_HIST * nb, d)
    row = lambda v: v.reshape(1, d)
    g_mix, g_ffn, g_fin = norm_mix, norm_ffn, row(norm_final)

    rows_p = batch * SEQ
    n_j_mix, n_j_ff = d // BN_MIX, D_FF // BN_P
    n_mix, n_ff = (rows_p // BM_P) * n_j_mix, (rows_p // BM_GATE_UP) * n_j_ff
    step_mix = lambda i, j: i * n_j_mix + j
    step_ff = lambda i, j: i * n_j_ff + j
    casts_mix = [_cast_job(conv_w_out, 0, n_mix, 0, step_mix), _cast_job(ffn_w_gate_up, 0, n_mix, 0, step_mix)]
    casts_ff0 = [_cast_job(ffn_w_down, 0, n_ff, 0, step_ff), _cast_job(ffn_w_gate_up, 1, n_ff, 1, step_ff)]
    casts_ff1 = [_cast_job(ffn_w_down, 1, n_ff, 0, step_ff)]

    z_s, uts, umeta, wb, wc, wv, h_s = _mix_conv_s(
        x_sample, meta_tokens, row(g_mix[0]), conv_w_in, conv_w_dw[0], hc, BN_S)
    z_p, utail_p, w_out, w_gu0 = _mix_conv_p(
        h_p, row(g_mix[0]), wb, wc, wv, conv_w_dw[0], umeta, BM_P, BN_MIX, casts_mix)
    h_p, m_p = _proj_res_p(z_p, w_out, h_p, row(g_ffn[0]), BM_CONV_OUT, D_MODEL, "emit", "conv_out_p")
    a_p, w_dn0, w_gu1 = _gate_up_p(m_p, w_gu0, BM_GATE_UP, BN_P, casts_ff0, "gate_up0_p")
    h_s = _ffn_s(h_s, row(g_ffn[0]), g_fin, w_gu0, w_dn0, BN_S_FF, False, "conv_out_ffn0_s", pre=(z_s, w_out, BK_S))
    h_p1, ntail_p, m_p = _down_pool_p(a_p, w_dn0, h_p, h_s, row(g_mix[1]), pool_w, row(pool_scale[0]),
                                      row(g_ffn[1]), BM_POOL)
    h_s1, pool_tail_s = _pool_s(h_s, hp, row(g_mix[1]), pool_w, row(pool_scale[0]))
    a_p, w_dn1 = _gate_up_p(m_p, w_gu1, BM_GATE_UP, BN_P, casts_ff1, "gate_up1_p")
    y_s = _ffn_s(h_s1, row(g_ffn[1]), g_fin, w_gu1, w_dn1, BN_S_FF, True, "ffn1_s")
    (y_p,) = _proj_res_p(a_p, w_dn1, h_p1, g_fin, BM_DOWN, BK_DOWN, "final", "down1_p")

    y_prompt = y_p.reshape(batch, SEQ, d)
    y_sample = y_s
    tps = SEQ // BM_P
    new_conv_prompt = utail_p.reshape(batch, tps, TAIL, d)[:, tps - 1, TAIL - CONV_HIST:, :][None]
    tpp = SEQ // BM_POOL
    new_pool_prompt = ntail_p.reshape(batch, tpp, HALO, d)[:, tpp - 1, HALO - POOL_HIST:, :][None]
    new_conv_sample = uts.reshape(CONV_HIST, nb, d).transpose(1, 0, 2)[None]
    new_pool_sample = pool_tail_s.reshape(POOL_HIST, nb, d).transpose(1, 0, 2)[None]
    return (y_prompt, y_sample, new_conv_prompt, new_pool_prompt, new_conv_sample, new_pool_sample)
```

```python
import functools
from typing import NamedTuple

import jax
import jax.numpy as jnp
from jax import lax
from jax.experimental import pallas as pl
from jax.experimental.pallas import tpu as pltpu

D_MODEL = 2048
D_FF = 5632
N_META = 16
SEQ = 2048
N_SAMPLE_SEQ = 128
SAMPLE_T = 8
CONV_HIST = 2
POOL_WINDOWS = (2, 4, 8, 16)
POOL_GROUP = D_MODEL // len(POOL_WINDOWS)
POOL_HIST = 15
EPS = 1e-6

N_SAMPLE_ROWS = N_SAMPLE_SEQ * SAMPLE_T
S_ROWS = N_SAMPLE_ROWS + N_META
META_ROW0 = N_SAMPLE_ROWS
TAIL = 8
HALO = 16
MXU_COLS = 256
FRONT = 8

V7X_SCOPED_VMEM_BYTES = 60000 * 1024

BF16 = jnp.bfloat16
F32 = jnp.float32


def _rms(x, g):
    ms = jnp.mean(x * x, axis=-1, keepdims=True)
    return (x * lax.rsqrt(ms + EPS)) * g


def _dot(a, b):
    return jnp.dot(a, b, preferred_element_type=F32)


def _params(n_axes):
    return pltpu.CompilerParams(
        dimension_semantics=("arbitrary",) * n_axes,
        vmem_limit_bytes=V7X_SCOPED_VMEM_BYTES,
    )


def _resident(shape, index_map):
    return pl.BlockSpec(shape, index_map, pipeline_mode=pl.Buffered(1))


class _CastJob(NamedTuple):
    src: jax.Array
    in_spec: pl.BlockSpec
    out_spec: pl.BlockSpec
    out_shape: jax.ShapeDtypeStruct


def _cast_job(w_stack, layer, n_steps, axis, step_of):
    _, r, c = w_stack.shape
    if axis == 0:
        blk = (r // n_steps, c)
        in_idx = lambda *ids: (layer, step_of(*ids), 0)
        out_idx = lambda *ids: (step_of(*ids), 0)
    else:
        blk = (r, c // n_steps)
        in_idx = lambda *ids: (layer, 0, step_of(*ids))
        out_idx = lambda *ids: (0, step_of(*ids))
    assert blk[0] * (n_steps if axis == 0 else 1) == r and blk[1] * (n_steps if axis == 1 else 1) == c
    return _CastJob(w_stack, pl.BlockSpec((None,) + blk, in_idx), pl.BlockSpec(blk, out_idx),
                    jax.ShapeDtypeStruct((r, c), BF16))


def _run_casts(src_refs, dst_refs):
    for src, dst in zip(src_refs, dst_refs):
        dst[...] = src[...].astype(BF16)


def _conv3(w, u, ubuf, off):
    t = u.shape[0]
    return (w[2:3] * u + w[1:2] * ubuf[off - 1:off - 1 + t, :] + w[0:1] * ubuf[off - 2:off - 2 + t, :])


def _mix_p_kernel(tiles_per_seq, n_cast, *refs):
    h_ref, g_ref, wb_ref, wc_ref, wv_ref, wdw_ref, umeta_ref = refs[:7]
    cast_src, refs = refs[7:7 + n_cast], refs[7 + n_cast:]
    z_ref, utail_ref = refs[:2]
    cast_dst, (n_sc, ubuf, carry) = refs[2:2 + n_cast], refs[2 + n_cast:]
    i = pl.program_id(0)
    j = pl.program_id(1)
    bm = h_ref.shape[0]
    _run_casts(cast_src, cast_dst)

    @pl.when(j == 0)
    def _():
        n_sc[...] = _rms(h_ref[...], g_ref[...]).astype(BF16)

    @pl.when(i % tiles_per_seq == 0)
    def _():
        ubuf[0:TAIL, :] = umeta_ref[...]

    @pl.when(i % tiles_per_seq != 0)
    def _():
        ubuf[0:TAIL, :] = carry[j]

    n = n_sc[...]
    for c0 in range(0, z_ref.shape[1], MXU_COLS):
        cols = slice(c0, c0 + MXU_COLS)
        b = _dot(n, wb_ref[:, cols])
        u = _dot(n, wc_ref[:, cols]) * _dot(n, wv_ref[:, cols])
        ubuf[TAIL:TAIL + bm, cols] = u
        z_ref[:, cols] = (b * _conv3(wdw_ref[:, cols], u, ubuf.at[:, cols], TAIL)).astype(BF16)
    tail = ubuf[bm:bm + TAIL, :]
    carry[j] = tail
    utail_ref[...] = tail


def _mix_s_kernel(xs_ref, meta_ref, g_ref, wb_ref, wc_ref, wv_ref, wdw_ref, hc_ref,
                  z_ref, uts_ref, umeta_ref, wb_o, wc_o, wv_o, h0_ref, n_sc, ubuf, mbuf):
    j = pl.program_id(0)
    ns = N_SAMPLE_ROWS
    nb = N_SAMPLE_SEQ
    nm = S_ROWS - ns

    @pl.when(j == 0)
    def _():
        for t in range(SAMPLE_T):
            h0_ref[t * nb:(t + 1) * nb, :] = xs_ref[:, t, :]
        h0_ref[ns:, :] = meta_ref[...]
        n_sc[...] = _rms(h0_ref[...], g_ref[...]).astype(BF16)

    wb_o[...] = wb_ref[...].astype(BF16)
    wc_o[...] = wc_ref[...].astype(BF16)
    wv_o[...] = wv_ref[...].astype(BF16)
    n = n_sc[...]
    nh = CONV_HIST * nb
    ubuf[0:nh, :] = hc_ref[...]
    mbuf[0:TAIL, :] = jnp.zeros((TAIL, mbuf.shape[1]), F32)
    for c0 in range(0, z_ref.shape[1], MXU_COLS):
        cols = slice(c0, c0 + MXU_COLS)
        b = _dot(n, wb_o[:, cols])
        u = _dot(n, wc_o[:, cols]) * _dot(n, wv_o[:, cols])
        w = wdw_ref[:, cols]

        us = u[0:ns, :]
        ubuf[nh:nh + ns, cols] = us
        conv_s = (w[2:3] * us + w[1:2] * ubuf[nb:nb + ns, cols] + w[0:1] * ubuf[0:ns, cols])
        z_ref[0:ns, cols] = (b[0:ns, :] * conv_s).astype(BF16)

        um = u[ns:, :]
        mbuf[TAIL:TAIL + nm, cols] = um
        z_ref[ns:, cols] = (b[ns:, :] * _conv3(w, um, mbuf.at[:, cols], TAIL)).astype(BF16)
    uts_ref[...] = ubuf[ns:ns + nh, :]
    umeta_ref[...] = mbuf[N_META:N_META + TAIL, :]


def _mix_conv_p(h, g, wb, wc, wv, w_dw, umeta, bm, bn, casts):
    rows = h.shape[0]
    n_i, n_j = rows // bm, D_MODEL // bn
    wspec = pl.BlockSpec((D_MODEL, bn), lambda i, j: (0, j))
    return pl.pallas_call(
        functools.partial(_mix_p_kernel, SEQ // bm, len(casts)),
        grid=(n_i, n_j),
        in_specs=[
            pl.BlockSpec((bm, D_MODEL), lambda i, j: (i, 0)),
            pl.BlockSpec((1, D_MODEL), lambda i, j: (0, 0)),
            wspec, wspec, wspec,
            pl.BlockSpec((3, bn), lambda i, j: (0, j)),
            pl.BlockSpec((TAIL, bn), lambda i, j: (0, j)),
        ] + [c.in_spec for c in casts],
        out_specs=[
            pl.BlockSpec((bm, bn), lambda i, j: (i, j)),
            pl.BlockSpec((TAIL, bn), lambda i, j: (i, j)),
        ] + [c.out_spec for c in casts],
        out_shape=[
            jax.ShapeDtypeStruct((rows, D_MODEL), BF16),
            jax.ShapeDtypeStruct((n_i * TAIL, D_MODEL), F32),
        ] + [c.out_shape for c in casts],
        scratch_shapes=[
            pltpu.VMEM((bm, D_MODEL), BF16),
            pltpu.VMEM((TAIL + bm, bn), F32),
            pltpu.VMEM((n_j, TAIL, bn), F32),
        ],
        compiler_params=_params(2),
        name="mix_conv_p",
    )(h, g, wb, wc, wv, w_dw, umeta, *[c.src for c in casts])


def _mix_conv_s(x_sample, meta, g, w_in, w_dw, hc, bn):
    n_j = D_MODEL // bn
    nb = N_SAMPLE_SEQ
    nm = S_ROWS - N_SAMPLE_ROWS
    wspec = lambda part: pl.BlockSpec((None, D_MODEL, bn), lambda j: (0, 0, part * n_j + j))
    wout = pl.BlockSpec((D_MODEL, bn), lambda j: (0, j))
    wshape = jax.ShapeDtypeStruct((D_MODEL, D_MODEL), BF16)
    return pl.pallas_call(
        _mix_s_kernel,
        grid=(n_j,),
        in_specs=[
            _resident((nb, SAMPLE_T, D_MODEL), lambda j: (0, 0, 0)),
            _resident((N_META, D_MODEL), lambda j: (0, 0)),
            pl.BlockSpec((1, D_MODEL), lambda j: (0, 0)),
            wspec(0), wspec(1), wspec(2),
            pl.BlockSpec((3, bn), lambda j: (0, j)),
            pl.BlockSpec((CONV_HIST * nb, bn), lambda j: (0, j)),
        ],
        out_specs=[
            pl.BlockSpec((S_ROWS, bn), lambda j: (0, j)),
            pl.BlockSpec((CONV_HIST * nb, bn), lambda j: (0, j)),
            pl.BlockSpec((TAIL, bn), lambda j: (0, j)),
            wout, wout, wout,
            pl.BlockSpec((S_ROWS, D_MODEL), lambda j: (0, 0)),
        ],
        out_shape=[
            jax.ShapeDtypeStruct((S_ROWS, D_MODEL), BF16),
            jax.ShapeDtypeStruct((CONV_HIST * nb, D_MODEL), F32),
            jax.ShapeDtypeStruct((TAIL, D_MODEL), F32),
            wshape, wshape, wshape,
            jax.ShapeDtypeStruct((S_ROWS, D_MODEL), F32),
        ],
        scratch_shapes=[
            pltpu.VMEM((S_ROWS, D_MODEL), BF16),
            pltpu.VMEM((CONV_HIST * nb + N_SAMPLE_ROWS, bn), F32),
            pltpu.VMEM((TAIL + nm, bn), F32),
        ],
        compiler_params=_params(1),
        name="mix_conv_s",
    )(x_sample, meta, g, w_in, w_in, w_in, w_dw, hc)


def _proj_res_body(norm, k, n_k, x_ref, w_ref, h_ref, g_ref, o_ref, m_ref):
    if n_k == 1:
        acc = h_ref[...] + _dot(x_ref[...], w_ref[...])
        o_ref[...] = _rms(acc, g_ref[...]) if norm == "final" else acc
        if norm == "emit":
            m_ref[...] = _rms(acc, g_ref[...]).astype(BF16)
        return

    @pl.when(k == 0)
    def _():
        o_ref[...] = h_ref[...] + _dot(x_ref[...], w_ref[...])

    @pl.when(k != 0)
    def _():
        o_ref[...] += _dot(x_ref[...], w_ref[...])

    if norm is not None:
        @pl.when(k == n_k - 1)
        def _():
            y = _rms(o_ref[...], g_ref[...])
            if norm == "final":
                o_ref[...] = y
            else:
                m_ref[...] = y.astype(BF16)


def _proj_res_p_kernel(norm, n_k, x_ref, w_ref, h_ref, g_ref, o_ref, m_ref=None):
    _proj_res_body(norm, pl.program_id(1), n_k, x_ref, w_ref, h_ref, g_ref, o_ref, m_ref)


def _proj_res_p(x, w, h, g, bm, bk, norm, name):
    rows, kdim = x.shape
    n_i, n_k = rows // bm, kdim // bk
    wspec = _resident if n_k == 1 else pl.BlockSpec
    row_spec = pl.BlockSpec((bm, D_MODEL), lambda i, k: (i, 0))
    out_specs, out_shape = [row_spec], [jax.ShapeDtypeStruct((rows, D_MODEL), F32)]
    if norm == "emit":
        out_specs.append(row_spec)
        out_shape.append(jax.ShapeDtypeStruct((rows, D_MODEL), BF16))
    return pl.pallas_call(
        functools.partial(_proj_res_p_kernel, norm, n_k),
        grid=(n_i, n_k),
        in_specs=[
            pl.BlockSpec((bm, bk), lambda i, k: (i, k)),
            wspec((bk, D_MODEL), lambda i, k: (k, 0)),
            row_spec,
            pl.BlockSpec((1, D_MODEL), lambda i, k: (0, 0)),
        ],
        out_specs=out_specs,
        out_shape=out_shape,
        compiler_params=_params(2),
        name=name,
    )(x, w, h, g)


def _swiglu_chunks(m, wg_ref, wu_ref, a_ref):
    for c0 in range(0, a_ref.shape[1], MXU_COLS):
        cols = slice(c0, c0 + MXU_COLS)
        gate = _dot(m, wg_ref[:, cols])
        up = _dot(m, wu_ref[:, cols])
        a_ref[:, cols] = (gate * (1.0 / (1.0 + jnp.exp(-gate))) * up).astype(BF16)


def _gate_up_p_kernel(n_cast, m_ref, wg_ref, wu_ref, *refs):
    _run_casts(refs[:n_cast], refs[n_cast + 1:])
    _swiglu_chunks(m_ref[...], wg_ref, wu_ref, refs[n_cast])


def _ffn_s_kernel(sample_out, n_pre, n_j, *refs):
    h_ref, refs = refs[0], refs[1:]
    if n_pre:
        (x_ref, wpre_ref), refs = refs[:2], refs[2:]
    g_ref, gfin_ref, wg_ref, wu_ref, wd_ref, o_ref, m_sc, a_sc = refs[:8]
    rest = list(refs[8:])
    hsrc = rest.pop(0) if n_pre else h_ref
    acc = rest.pop(0) if sample_out else o_ref
    s = pl.program_id(0)
    j = s - n_pre

    if n_pre:
        @pl.when(s < n_pre)
        def _():
            _proj_res_body(None, s, n_pre, x_ref, wpre_ref, h_ref, None, hsrc, None)

    @pl.when(s >= n_pre)
    def _():
        @pl.when(j == 0)
        def _():
            m_sc[...] = _rms(hsrc[...], g_ref[...]).astype(BF16)

        _swiglu_chunks(m_sc[...], wg_ref, wu_ref, a_sc)
        _proj_res_body(None, j, n_j, a_sc, wd_ref, hsrc, None, acc, None)

        if sample_out:
            @pl.when(j == n_j - 1)
            def _():
                y = _rms(acc[0:N_SAMPLE_ROWS, :], gfin_ref[...])
                for t in range(SAMPLE_T):
                    o_ref[:, t, :] = y[t * N_SAMPLE_SEQ:(t + 1) * N_SAMPLE_SEQ, :]


def _ffn_s(h, g, g_fin, w_gu, w_dn, bn, sample_out, name, pre=None):
    rows = h.shape[0]
    n_j = D_FF // bn
    n_pre = 0 if pre is None else pre[0].shape[1] // pre[2]
    ffn_step = lambda s: jnp.maximum(s - n_pre, 0)
    row_spec = pl.BlockSpec((1, D_MODEL), lambda s: (0, 0))
    o_shape = (N_SAMPLE_SEQ, SAMPLE_T, D_MODEL) if sample_out else (rows, D_MODEL)
    o_index = (0,) * len(o_shape)
    in_specs, args = [_resident((rows, D_MODEL), lambda s: (0, 0))], [h]
    scratch = [pltpu.VMEM((rows, D_MODEL), BF16), pltpu.VMEM((rows, bn), BF16)]
    if n_pre:
        x, w_pre, bk = pre
        pre_step = lambda s: jnp.minimum(s, n_pre - 1)
        in_specs += [pl.BlockSpec((rows, bk), lambda s: (0, pre_step(s))),
                     pl.BlockSpec((bk, D_MODEL), lambda s: (pre_step(s), 0))]
        args += [x, w_pre]
        scratch.append(pltpu.VMEM((rows, D_MODEL), F32))
    if sample_out:
        scratch.append(pltpu.VMEM((rows, D_MODEL), F32))
    in_specs += [
        row_spec, row_spec,
        pl.BlockSpec((D_MODEL, bn), lambda s: (0, ffn_step(s))),
        pl.BlockSpec((D_MODEL, bn), lambda s: (0, n_j + ffn_step(s))),
        pl.BlockSpec((bn, D_MODEL), lambda s: (ffn_step(s), 0)),
    ]
    return pl.pallas_call(
        functools.partial(_ffn_s_kernel, sample_out, n_pre, n_j),
        grid=(n_pre + n_j,),
        in_specs=in_specs,
        out_specs=pl.BlockSpec(o_shape, lambda s: o_index),
        out_shape=jax.ShapeDtypeStruct(o_shape, F32),
        scratch_shapes=scratch,
        compiler_params=_params(1),
        name=name,
    )(*args, g, g_fin, w_gu, w_gu, w_dn)


def _gate_up_p(m, w_gu, bm, bn, casts, name):
    rows = m.shape[0]
    n_i, n_j = rows // bm, D_FF // bn
    return pl.pallas_call(
        functools.partial(_gate_up_p_kernel, len(casts)),
        grid=(n_i, n_j),
        in_specs=[
            pl.BlockSpec((bm, D_MODEL), lambda i, j: (i, 0)),
            pl.BlockSpec((D_MODEL, bn), lambda i, j: (0, j)),
            pl.BlockSpec((D_MODEL, bn), lambda i, j: (0, n_j + j)),
        ] + [c.in_spec for c in casts],
        out_specs=[pl.BlockSpec((bm, bn), lambda i, j: (i, j))] + [c.out_spec for c in casts],
        out_shape=[jax.ShapeDtypeStruct((rows, D_FF), BF16)] + [c.out_shape for c in casts],
        compiler_params=_params(2),
        name=name,
    )(m, w_gu, w_gu, *[c.src for c in casts])


def _down_pool_kernel(tiles_per_seq, a_ref, w_ref, h_ref, meta_ref, g_ref, wp_ref, sc_ref, gnext_ref,
                      o_ref, ntail_ref, m_ref, h2_even, h2_odd, ncarry, nbuf, pbuf, qbuf):
    s = pl.program_id(0)
    bm = h_ref.shape[0]
    g = g_ref[...]
    rows = HALO + bm
    r0 = FRONT + HALO
    split = (3 * D_MODEL) // 4

    @pl.when(s == 0)
    def _():
        h2_odd[...] = jnp.zeros(h2_odd.shape, F32)
        ncarry[...] = jnp.zeros(ncarry.shape, F32)

    @pl.when((s - 1) % tiles_per_seq == 0)
    def _():
        nbuf[FRONT:r0, :] = _rms(meta_ref[...], g)

    @pl.when((s - 1) % tiles_per_seq != 0)
    def _():
        nbuf[FRONT:r0, :] = ncarry[...]

    def project(h2_new, c0, c1):
        h2_new[:, c0:c1] = h_ref[:, c0:c1] + _dot(a_ref[...], w_ref[:, c0:c1])

    def step(h2_new, h2_old):
        project(h2_new, 0, split)
        for buf in (nbuf, pbuf, qbuf):
            buf[0:FRONT, :] = jnp.zeros((FRONT, buf.shape[1]), F32)
        n = _rms(h2_old[...], g)
        nbuf[r0:r0 + bm, :] = n
        tail = n[bm - HALO:, :]
        ncarry[...] = tail
        ntail_ref[...] = tail
        for gi, win in enumerate(POOL_WINDOWS):
            cols = slice(gi * POOL_GROUP, (gi + 1) * POOL_GROUP)
            src, src_cols, shift, level = nbuf, cols, 1, 0
            while shift < win:
                dst = (pbuf, qbuf)[level % 2]
                dst[FRONT:FRONT + rows, :] = (src[FRONT:FRONT + rows, src_cols]
                                              + src[FRONT - shift:FRONT - shift + rows, src_cols])
                src, src_cols, shift, level = dst, slice(None), 2 * shift, level + 1
            p = src[r0:r0 + bm, src_cols] * (1.0 / win) - n[:, cols]
            y = _dot(p.astype(BF16), wp_ref[gi].astype(BF16)) * sc_ref[:, cols]
            o_ref[:, cols] = h2_old[:, cols] + y
        project(h2_new, split, D_MODEL)
        m_ref[...] = _rms(o_ref[...], gnext_ref[...]).astype(BF16)

    @pl.when(s % 2 == 0)
    def _():
        step(h2_even, h2_odd)

    @pl.when(s % 2 == 1)
    def _():
        step(h2_odd, h2_even)


def _down_pool_p(a, w, h, h_s, g, wp, scale, g_next, bm):
    rows, kdim = a.shape
    n_t = rows // bm
    ng = len(POOL_WINDOWS)
    cur = lambda s: (jnp.minimum(s, n_t - 1), 0)
    prev = lambda s: (jnp.maximum(s - 1, 0), 0)
    vec = pl.BlockSpec((1, D_MODEL), lambda s: (0, 0))
    return pl.pallas_call(
        functools.partial(_down_pool_kernel, SEQ // bm),
        grid=(n_t + 1,),
        in_specs=[
            pl.BlockSpec((bm, kdim), cur),
            _resident((kdim, D_MODEL), lambda s: (0, 0)),
            pl.BlockSpec((bm, D_MODEL), cur),
            _resident((HALO, D_MODEL), lambda s: (META_ROW0 // HALO, 0)),
            vec,
            _resident((None, ng, POOL_GROUP, POOL_GROUP), lambda s: (0, 0, 0, 0)),
            vec, vec,
        ],
        out_specs=[
            pl.BlockSpec((bm, D_MODEL), prev),
            pl.BlockSpec((HALO, D_MODEL), prev),
            pl.BlockSpec((bm, D_MODEL), prev),
        ],
        out_shape=[
            jax.ShapeDtypeStruct((rows, D_MODEL), F32),
            jax.ShapeDtypeStruct((n_t * HALO, D_MODEL), F32),
            jax.ShapeDtypeStruct((rows, D_MODEL), BF16),
        ],
        scratch_shapes=[
            pltpu.VMEM((bm, D_MODEL), F32),
            pltpu.VMEM((bm, D_MODEL), F32),
            pltpu.VMEM((HALO, D_MODEL), F32),
            pltpu.VMEM((FRONT + HALO + bm, D_MODEL), F32),
            pltpu.VMEM((FRONT + HALO + bm, POOL_GROUP), F32),
            pltpu.VMEM((FRONT + HALO + bm, POOL_GROUP), F32),
        ],
        compiler_params=_params(1),
        name="down_pool_p",
    )(a, w, h, h_s, g, wp, scale, g_next)


def _pool_s_kernel(hfull_ref, hcol_ref, hp_ref, g_ref, wp_ref, sc_ref, o_ref, tail_ref, inv_sc, nbuf, sum_sc):
    j = pl.program_id(0)
    ns = N_SAMPLE_ROWS
    nb = N_SAMPLE_SEQ
    nh = POOL_HIST * nb

    @pl.when(j == 0)
    def _():
        xf = hfull_ref[...]
        inv_sc[...] = lax.rsqrt(jnp.mean(xf * xf, axis=-1, keepdims=True) + EPS)

    x = hcol_ref[...]
    n = ((x * inv_sc[...]) * g_ref[...])[0:ns, :]
    nbuf[0:nh, :] = hp_ref[...]
    nbuf[nh:nh + ns, :] = n
    tail_ref[...] = nbuf[ns:ns + nh, :]

    for gi, win in enumerate(POOL_WINDOWS):
        @pl.when(j == gi)
        def _(win=win):
            acc = n
            for k in range(1, win):
                acc = acc + nbuf[nh - k * nb:nh - k * nb + ns, :]
            sum_sc[...] = acc * (1.0 / win)

    p = sum_sc[...] - n
    y = _dot(p.astype(BF16), wp_ref[...].astype(BF16)) * sc_ref[...]
    o_ref[0:ns, :] = x[0:ns, :] + y
    o_ref[ns:, :] = x[ns:, :]


def _pool_s(h_s, state_pool, g, wp, scale):
    ns = N_SAMPLE_ROWS
    nb = N_SAMPLE_SEQ
    nh = POOL_HIST * nb
    pg = POOL_GROUP
    n_g = len(POOL_WINDOWS)
    state_spec = pl.BlockSpec((nh, pg), lambda j: (0, j))
    return pl.pallas_call(
        _pool_s_kernel,
        grid=(n_g,),
        in_specs=[
            _resident((S_ROWS, D_MODEL), lambda j: (0, 0)),
            pl.BlockSpec((S_ROWS, pg), lambda j: (0, j)),
            state_spec,
            pl.BlockSpec((1, pg), lambda j: (0, j)),
            pl.BlockSpec((None, None, pg, pg), lambda j: (0, j, 0, 0)),
            pl.BlockSpec((1, pg), lambda j: (0, j)),
        ],
        out_specs=[pl.BlockSpec((S_ROWS, pg), lambda j: (0, j)), state_spec],
        out_shape=[
            jax.ShapeDtypeStruct((S_ROWS, D_MODEL), F32),
            jax.ShapeDtypeStruct((nh, D_MODEL), F32),
        ],
        scratch_shapes=[
            pltpu.VMEM((S_ROWS, 1), F32),
            pltpu.VMEM((nh + ns, pg), F32),
            pltpu.VMEM((ns, pg), F32),
        ],
        compiler_params=_params(1),
        name="pool_s",
    )(h_s, h_s, state_pool, g, wp, scale)


BM_P = 1024
BN_MIX = 512
BM_CONV_OUT = 512
BM_GATE_UP = 2048
BN_P = 512
BM_DOWN = 512
BK_DOWN = D_FF
BM_POOL = 256
BN_S = 256
BN_S_FF = 512
BK_S = 512

assert SEQ % BM_P == 0 and SEQ % BM_POOL == 0 and BM_POOL % HALO == 0 and META_ROW0 % HALO == 0
assert D_MODEL % BN_MIX == 0 and D_MODEL % BN_S == 0 and BN_MIX % MXU_COLS == 0 and BN_S % MXU_COLS == 0
assert D_FF % BN_P == 0 and D_FF % BN_S_FF == 0 and D_FF % BK_S == 0 and D_MODEL % BK_S == 0


def kernel(x_prompt, x_sample, state_conv, state_pool, meta_tokens, norm_mix, norm_ffn, norm_final,
           conv_w_in, conv_w_dw, conv_w_out, pool_w, pool_scale, ffn_w_gate_up, ffn_w_down):
    d = D_MODEL
    nb, nt = N_SAMPLE_SEQ, SAMPLE_T
    batch = x_prompt.shape[0]
    assert x_prompt.shape == (batch, SEQ, d) and x_sample.shape == (nb, nt, d)
    assert state_conv.shape == (1, nb, CONV_HIST, d) and state_pool.shape == (1, nb, POOL_HIST, d)
    assert meta_tokens.shape == (N_META, d) and ffn_w_gate_up.shape == (2, d, 2 * D_FF)
    assert (batch * SEQ) % BM_GATE_UP == 0 and (batch * SEQ) % BM_DOWN == 0 and (batch * SEQ) % BM_CONV_OUT == 0

    h_p = x_prompt.reshape(batch * SEQ, d)
    hc = state_conv[0].transpose(1, 0, 2).reshape(CONV_HIST * nb, d)
    hp = state_pool[0].transpose(1, 0, 2).reshape(POOL_HIST * nb, d)
    row = lambda v: v.reshape(1, d)
    g_mix, g_ffn, g_fin = norm_mix, norm_ffn, row(norm_final)

    rows_p = batch * SEQ
    n_j_mix, n_j_ff = d // BN_MIX, D_FF // BN_P
    n_mix, n_ff = (rows_p // BM_P) * n_j_mix, (rows_p // BM_GATE_UP) * n_j_ff
    step_mix = lambda i, j: i * n_j_mix + j
    step_ff = lambda i, j: i * n_j_ff + j
    casts_mix = [_cast_job(conv_w_out, 0, n_mix, 0, step_mix), _cast_job(ffn_w_gate_up, 0, n_mix, 0, step_mix)]
    casts_ff0 = [_cast_job(ffn_w_down, 0, n_ff, 0, step_ff), _cast_job(ffn_w_gate_up, 1, n_ff, 1, step_ff)]
    casts_ff1 = [_cast_job(ffn_w_down, 1, n_ff, 0, step_ff)]

    z_s, uts, umeta, wb, wc, wv, h_s = _mix_conv_s(
        x_sample, meta_tokens, row(g_mix[0]), conv_w_in, conv_w_dw[0], hc, BN_S)
    z_p, utail_p, w_out, w_gu0 = _mix_conv_p(
        h_p, row(g_mix[0]), wb, wc, wv, conv_w_dw[0], umeta, BM_P, BN_MIX, casts_mix)
    h_p, m_p = _proj_res_p(z_p, w_out, h_p, row(g_ffn[0]), BM_CONV_OUT, D_MODEL, "emit", "conv_out_p")
    a_p, w_dn0, w_gu1 = _gate_up_p(m_p, w_gu0, BM_GATE_UP, BN_P, casts_ff0, "gate_up0_p")
    h_s = _ffn_s(h_s, row(g_ffn[0]), g_fin, w_gu0, w_dn0, BN_S_FF, False, "conv_out_ffn0_s", pre=(z_s, w_out, BK_S))
    h_p1, ntail_p, m_p = _down_pool_p(a_p, w_dn0, h_p, h_s, row(g_mix[1]), pool_w, row(pool_scale[0]),
                                      row(g_ffn[1]), BM_POOL)
    h_s1, pool_tail_s = _pool_s(h_s, hp, row(g_mix[1]), pool_w, row(pool_scale[0]))
    a_p, w_dn1 = _gate_up_p(m_p, w_gu1, BM_GATE_UP, BN_P, casts_ff1, "gate_up1_p")
    y_s = _ffn_s(h_s1, row(g_ffn[1]), g_fin, w_gu1, w_dn1, BN_S_FF, True, "ffn1_s")
    (y_p,) = _proj_res_p(a_p, w_dn1, h_p1, g_fin, BM_DOWN, BK_DOWN, "final", "down1_p")

    y_prompt = y_p.reshape(batch, SEQ, d)
    y_sample = y_s
    tps = SEQ // BM_P
    new_conv_prompt = utail_p.reshape(batch, tps, TAIL, d)[:, tps - 1, TAIL - CONV_HIST:, :][None]
    tpp = SEQ // BM_POOL
    new_pool_prompt = ntail_p.reshape(batch, tpp, HALO, d)[:, tpp - 1, HALO - POOL_HIST:, :][None]
    new_conv_sample = uts.reshape(CONV_HIST, nb, d).transpose(1, 0, 2)[None]
    new_pool_sample = pool_tail_s.reshape(POOL_HIST, nb, d).transpose(1, 0, 2)[None]
    return (y_prompt, y_sample, new_conv_prompt, new_pool_prompt, new_conv_sample, new_pool_sample)
```

```python
import functools
from typing import NamedTuple

import jax
import jax.numpy as jnp
from jax import lax
from jax.experimental import pallas as pl
from jax.experimental.pallas import tpu as pltpu

D_MODEL = 2048
D_FF = 5632
N_META = 16
SEQ = 2048
N_SAMPLE_SEQ = 128
SAMPLE_T = 8
CONV_HIST = 2
POOL_WINDOWS = (2, 4, 8, 16)
POOL_GROUP = D_MODEL // len(POOL_WINDOWS)
POOL_HIST = 15
EPS = 1e-6

N_SAMPLE_ROWS = N_SAMPLE_SEQ * SAMPLE_T
S_ROWS = N_SAMPLE_ROWS + N_META
META_ROW0 = N_SAMPLE_ROWS
TAIL = 8
HALO = 16
MXU_COLS = 256
FRONT = 8

V7X_SCOPED_VMEM_BYTES = 60000 * 1024

BF16 = jnp.bfloat16
F32 = jnp.float32


def _rms(x, g):
    ms = jnp.mean(x * x, axis=-1, keepdims=True)
    return (x * lax.rsqrt(ms + EPS)) * g


def _dot(a, b):
    return jnp.dot(a, b, preferred_element_type=F32)


def _params(n_axes):
    return pltpu.CompilerParams(
        dimension_semantics=("arbitrary",) * n_axes,
        vmem_limit_bytes=V7X_SCOPED_VMEM_BYTES,
    )


def _resident(shape, index_map):
    return pl.BlockSpec(shape, index_map, pipeline_mode=pl.Buffered(1))


class _CastJob(NamedTuple):
    src: jax.Array
    in_spec: pl.BlockSpec
    out_spec: pl.BlockSpec
    out_shape: jax.ShapeDtypeStruct


def _cast_job(w_stack, layer, n_steps, axis, step_of):
    _, r, c = w_stack.shape
    if axis == 0:
        blk = (r // n_steps, c)
        in_idx = lambda *ids: (layer, step_of(*ids), 0)
        out_idx = lambda *ids: (step_of(*ids), 0)
    else:
        blk = (r, c // n_steps)
        in_idx = lambda *ids: (layer, 0, step_of(*ids))
        out_idx = lambda *ids: (0, step_of(*ids))
    assert blk[0] * (n_steps if axis == 0 else 1) == r and blk[1] * (n_steps if axis == 1 else 1) == c
    return _CastJob(w_stack, pl.BlockSpec((None,) + blk, in_idx), pl.BlockSpec(blk, out_idx),
                    jax.ShapeDtypeStruct((r, c), BF16))


def _run_casts(src_refs, dst_refs):
    for src, dst in zip(src_refs, dst_refs):
        dst[...] = src[...].astype(BF16)


def _conv3(w, u, ubuf, off):
    t = u.shape[0]
    return (w[2:3] * u + w[1:2] * ubuf[off - 1:off - 1 + t, :] + w[0:1] * ubuf[off - 2:off - 2 + t, :])


def _mix_p_kernel(tiles_per_seq, n_cast, *refs):
    h_ref, g_ref, wb_ref, wc_ref, wv_ref, wdw_ref, umeta_ref = refs[:7]
    cast_src, refs = refs[7:7 + n_cast], refs[7 + n_cast:]
    z_ref, utail_ref = refs[:2]
    cast_dst, (n_sc, ubuf, carry) = refs[2:2 + n_cast], refs[2 + n_cast:]
    i = pl.program_id(0)
    j = pl.program_id(1)
    bm = h_ref.shape[0]
    _run_casts(cast_src, cast_dst)

    @pl.when(j == 0)
    def _():
        n_sc[...] = _rms(h_ref[...], g_ref[...]).astype(BF16)

    @pl.when(i % tiles_per_seq == 0)
    def _():
        ubuf[0:TAIL, :] = umeta_ref[...]

    @pl.when(i % tiles_per_seq != 0)
    def _():
        ubuf[0:TAIL, :] = carry[j]

    n = n_sc[...]
    for c0 in range(0, z_ref.shape[1], MXU_COLS):
        cols = slice(c0, c0 + MXU_COLS)
        b = _dot(n, wb_ref[:, cols])
        u = _dot(n, wc_ref[:, cols]) * _dot(n, wv_ref[:, cols])
        ubuf[TAIL:TAIL + bm, cols] = u
        z_ref[:, cols] = (b * _conv3(wdw_ref[:, cols], u, ubuf.at[:, cols], TAIL)).astype(BF16)
    tail = ubuf[bm:bm + TAIL, :]
    carry[j] = tail
    utail_ref[...] = tail


def _mix_s_kernel(xs_ref, meta_ref, g_ref, wb_ref, wc_ref, wv_ref, wdw_ref, hc_ref,
                  z_ref, uts_ref, umeta_ref, wb_o, wc_o, wv_o, h0_ref, n_sc, ubuf, mbuf):
    j = pl.program_id(0)
    ns = N_SAMPLE_ROWS
    nb = N_SAMPLE_SEQ
    nm = S_ROWS - ns

    @pl.when(j == 0)
    def _():
        for t in range(SAMPLE_T):
            h0_ref[t * nb:(t + 1) * nb, :] = xs_ref[:, t, :]
        h0_ref[ns:, :] = meta_ref[...]
        n_sc[...] = _rms(h0_ref[...], g_ref[...]).astype(BF16)

    wb_o[...] = wb_ref[...].astype(BF16)
    wc_o[...] = wc_ref[...].astype(BF16)
    wv_o[...] = wv_ref[...].astype(BF16)
    n = n_sc[...]
    nh = CONV_HIST * nb
    ubuf[0:nh, :] = hc_ref[...]
    mbuf[0:TAIL, :] = jnp.zeros((TAIL, mbuf.shape[1]), F32)
    for c0 in range(0, z_ref.shape[1], MXU_COLS):
        cols = slice(c0, c0 + MXU_COLS)
        b = _dot(n, wb_o[:, cols])
        u = _dot(n, wc_o[:, cols]) * _dot(n, wv_o[:, cols])
        w = wdw_ref[:, cols]

        us = u[0:ns, :]
        ubuf[nh:nh + ns, cols] = us
        conv_s = (w[2:3] * us + w[1:2] * ubuf[nb:nb + ns, cols] + w[0:1] * ubuf[0:ns, cols])
        z_ref[0:ns, cols] = (b[0:ns, :] * conv_s).astype(BF16)

        um = u[ns:, :]
        mbuf[TAIL:TAIL + nm, cols] = um
        z_ref[ns:, cols] = (b[ns:, :] * _conv3(w, um, mbuf.at[:, cols], TAIL)).astype(BF16)
    uts_ref[...] = ubuf[ns:ns + nh, :]
    umeta_ref[...] = mbuf[N_META:N_META + TAIL, :]


def _mix_conv_p(h, g, wb, wc, wv, w_dw, umeta, bm, bn, casts):
    rows = h.shape[0]
    n_i, n_j = rows // bm, D_MODEL // bn
    wspec = pl.BlockSpec((D_MODEL, bn), lambda i, j: (0, j))
    return pl.pallas_call(
        functools.partial(_mix_p_kernel, SEQ // bm, len(casts)),
        grid=(n_i, n_j),
        in_specs=[
            pl.BlockSpec((bm, D_MODEL), lambda i, j: (i, 0)),
            pl.BlockSpec((1, D_MODEL), lambda i, j: (0, 0)),
            wspec, wspec, wspec,
            pl.BlockSpec((3, bn), lambda i, j: (0, j)),
            pl.BlockSpec((TAIL, bn), lambda i, j: (0, j)),
        ] + [c.in_spec for c in casts],
        out_specs=[
            pl.BlockSpec((bm, bn), lambda i, j: (i, j)),
            pl.BlockSpec((TAIL, bn), lambda i, j: (i, j)),
        ] + [c.out_spec for c in casts],
        out_shape=[
            jax.ShapeDtypeStruct((rows, D_MODEL), BF16),
            jax.ShapeDtypeStruct((n_i * TAIL, D_MODEL), F32),
        ] + [c.out_shape for c in casts],
        scratch_shapes=[
            pltpu.VMEM((bm, D_MODEL), BF16),
            pltpu.VMEM((TAIL + bm, bn), F32),
            pltpu.VMEM((n_j, TAIL, bn), F32),
        ],
        compiler_params=_params(2),
        name="mix_conv_p",
    )(h, g, wb, wc, wv, w_dw, umeta, *[c.src for c in casts])


def _mix_conv_s(x_sample, meta, g, w_in, w_dw, hc, bn):
    n_j = D_MODEL // bn
    nb = N_SAMPLE_SEQ
    nm = S_ROWS - N_SAMPLE_ROWS
    wspec = lambda part: pl.BlockSpec((None, D_MODEL, bn), lambda j: (0, 0, part * n_j + j))
    wout = pl.BlockSpec((D_MODEL, bn), lambda j: (0, j))
    wshape = jax.ShapeDtypeStruct((D_MODEL, D_MODEL), BF16)
    return pl.pallas_call(
        _mix_s_kernel,
        grid=(n_j,),
        in_specs=[
            _resident((nb, SAMPLE_T, D_MODEL), lambda j: (0, 0, 0)),
            _resident((N_META, D_MODEL), lambda j: (0, 0)),
            pl.BlockSpec((1, D_MODEL), lambda j: (0, 0)),
            wspec(0), wspec(1), wspec(2),
            pl.BlockSpec((3, bn), lambda j: (0, j)),
            pl.BlockSpec((CONV_HIST * nb, bn), lambda j: (0, j)),
        ],
        out_specs=[
            pl.BlockSpec((S_ROWS, bn), lambda j: (0, j)),
            pl.BlockSpec((CONV_HIST * nb, bn), lambda j: (0, j)),
            pl.BlockSpec((TAIL, bn), lambda j: (0, j)),
            wout, wout, wout,
            pl.BlockSpec((S_ROWS, D_MODEL), lambda j: (0, 0)),
        ],
        out_shape=[
            jax.ShapeDtypeStruct((S_ROWS, D_MODEL), BF16),
            jax.ShapeDtypeStruct((CONV_HIST * nb, D_MODEL), F32),
            jax.ShapeDtypeStruct((TAIL, D_MODEL), F32),
            wshape, wshape, wshape,
            jax.ShapeDtypeStruct((S_ROWS, D_MODEL), F32),
        ],
        scratch_shapes=[
            pltpu.VMEM((S_ROWS, D_MODEL), BF16),
            pltpu.VMEM((CONV_HIST * nb + N_SAMPLE_ROWS, bn), F32),
            pltpu.VMEM((TAIL + nm, bn), F32),
        ],
        compiler_params=_params(1),
        name="mix_conv_s",
    )(x_sample, meta, g, w_in, w_in, w_in, w_dw, hc)


def _proj_res_body(norm, k, n_k, x_ref, w_ref, h_ref, g_ref, o_ref, m_ref):
    if n_k == 1:
        acc = h_ref[...] + _dot(x_ref[...], w_ref[...])
        o_ref[...] = _rms(acc, g_ref[...]) if norm == "final" else acc
        if norm == "emit":
            m_ref[...] = _rms(acc, g_ref[...]).astype(BF16)
        return

    @pl.when(k == 0)
    def _():
        o_ref[...] = h_ref[...] + _dot(x_ref[...], w_ref[...])

    @pl.when(k != 0)
    def _():
        o_ref[...] += _dot(x_ref[...], w_ref[...])

    if norm is not None:
        @pl.when(k == n_k - 1)
        def _():
            y = _rms(o_ref[...], g_ref[...])
            if norm == "final":
                o_ref[...] = y
            else:
                m_ref[...] = y.astype(BF16)


def _proj_res_p_kernel(norm, n_k, x_ref, w_ref, h_ref, g_ref, o_ref, m_ref=None):
    _proj_res_body(norm, pl.program_id(1), n_k, x_ref, w_ref, h_ref, g_ref, o_ref, m_ref)


def _proj_res_p(x, w, h, g, bm, bk, norm, name):
    rows, kdim = x.shape
    n_i, n_k = rows // bm, kdim // bk
    wspec = _resident if n_k == 1 else pl.BlockSpec
    row_spec = pl.BlockSpec((bm, D_MODEL), lambda i, k: (i, 0))
    out_specs, out_shape = [row_spec], [jax.ShapeDtypeStruct((rows, D_MODEL), F32)]
    if norm == "emit":
        out_specs.append(row_spec)
        out_shape.append(jax.ShapeDtypeStruct((rows, D_MODEL), BF16))
    return pl.pallas_call(
        functools.partial(_proj_res_p_kernel, norm, n_k),
        grid=(n_i, n_k),
        in_specs=[
            pl.BlockSpec((bm, bk), lambda i, k: (i, k)),
            wspec((bk, D_MODEL), lambda i, k: (k, 0)),
            row_spec,
            pl.BlockSpec((1, D_MODEL), lambda i, k: (0, 0)),
        ],
        out_specs=out_specs,
        out_shape=out_shape,
        compiler_params=_params(2),
        name=name,
    )(x, w, h, g)


def _swiglu_chunks(m, wg_ref, wu_ref, a_ref):
    for c0 in range(0, a_ref.shape[1], MXU_COLS):
        cols = slice(c0, c0 + MXU_COLS)
        gate = _dot(m, wg_ref[:, cols])
        up = _dot(m, wu_ref[:, cols])
        a_ref[:, cols] = (gate * (1.0 / (1.0 + jnp.exp(-gate))) * up).astype(BF16)


def _gate_up_p_kernel(n_cast, m_ref, wg_ref, wu_ref, *refs):
    _run_casts(refs[:n_cast], refs[n_cast + 1:])
    _swiglu_chunks(m_ref[...], wg_ref, wu_ref, refs[n_cast])


def _ffn_s_kernel(sample_out, n_pre, n_j, *refs):
    h_ref, refs = refs[0], refs[1:]
    if n_pre:
        (x_ref, wpre_ref), refs = refs[:2], refs[2:]
    g_ref, gfin_ref, wg_ref, wu_ref, wd_ref, o_ref, m_sc, a_sc = refs[:8]
    rest = list(refs[8:])
    hsrc = rest.pop(0) if n_pre else h_ref
    acc = rest.pop(0) if sample_out else o_ref
    s = pl.program_id(0)
    j = s - n_pre

    if n_pre:
        @pl.when(s < n_pre)
        def _():
            _proj_res_body(None, s, n_pre, x_ref, wpre_ref, h_ref, None, hsrc, None)

    @pl.when(s >= n_pre)
    def _():
        @pl.when(j == 0)
        def _():
            m_sc[...] = _rms(hsrc[...], g_ref[...]).astype(BF16)

        _swiglu_chunks(m_sc[...], wg_ref, wu_ref, a_sc)
        _proj_res_body(None, j, n_j, a_sc, wd_ref, hsrc, None, acc, None)

        if sample_out:
            @pl.when(j == n_j - 1)
            def _():
                y = _rms(acc[0:N_SAMPLE_ROWS, :], gfin_ref[...])
                for t in range(SAMPLE_T):
                    o_ref[:, t, :] = y[t * N_SAMPLE_SEQ:(t + 1) * N_SAMPLE_SEQ, :]


def _ffn_s(h, g, g_fin, w_gu, w_dn, bn, sample_out, name, pre=None):
    rows = h.shape[0]
    n_j = D_FF // bn
    n_pre = 0 if pre is None else pre[0].shape[1] // pre[2]
    ffn_step = lambda s: jnp.maximum(s - n_pre, 0)
    row_spec = pl.BlockSpec((1, D_MODEL), lambda s: (0, 0))
    o_shape = (N_SAMPLE_SEQ, SAMPLE_T, D_MODEL) if sample_out else (rows, D_MODEL)
    o_index = (0,) * len(o_shape)
    in_specs, args = [_resident((rows, D_MODEL), lambda s: (0, 0))], [h]
    scratch = [pltpu.VMEM((rows, D_MODEL), BF16), pltpu.VMEM((rows, bn), BF16)]
    if n_pre:
        x, w_pre, bk = pre
        pre_step = lambda s: jnp.minimum(s, n_pre - 1)
        in_specs += [pl.BlockSpec((rows, bk), lambda s: (0, pre_step(s))),
                     pl.BlockSpec((bk, D_MODEL), lambda s: (pre_step(s), 0))]
        args += [x, w_pre]
        scratch.append(pltpu.VMEM((rows, D_MODEL), F32))
    if sample_out:
        scratch.append(pltpu.VMEM((rows, D_MODEL), F32))
    in_specs += [
        row_spec, row_spec,
        pl.BlockSpec((D_MODEL, bn), lambda s: (0, ffn_step(s))),
        pl.BlockSpec((D_MODEL, bn), lambda s: (0, n_j + ffn_step(s))),
        pl.BlockSpec((bn, D_MODEL), lambda s: (ffn_step(s), 0)),
    ]
    return pl.pallas_call(
        functools.partial(_ffn_s_kernel, sample_out, n_pre, n_j),
        grid=(n_pre + n_j,),
        in_specs=in_specs,
        out_specs=pl.BlockSpec(o_shape, lambda s: o_index),
        out_shape=jax.ShapeDtypeStruct(o_shape, F32),
        scratch_shapes=scratch,
        compiler_params=_params(1),
        name=name,
    )(*args, g, g_fin, w_gu, w_gu, w_dn)


def _gate_up_p(m, w_gu, bm, bn, casts, name):
    rows = m.shape[0]
    n_i, n_j = rows // bm, D_FF // bn
    return pl.pallas_call(
        functools.partial(_gate_up_p_kernel, len(casts)),
        grid=(n_i, n_j),
        in_specs=[
            pl.BlockSpec((bm, D_MODEL), lambda i, j: (i, 0)),
            pl.BlockSpec((D_MODEL, bn), lambda i, j: (0, j)),
            pl.BlockSpec((D_MODEL, bn), lambda i, j: (0, n_j + j)),
        ] + [c.in_spec for c in casts],
        out_specs=[pl.BlockSpec((bm, bn), lambda i, j: (i, j))] + [c.out_spec for c in casts],
        out_shape=[jax.ShapeDtypeStruct((rows, D_FF), BF16)] + [c.out_shape for c in casts],
        compiler_params=_params(2),
        name=name,
    )(m, w_gu, w_gu, *[c.src for c in casts])


def _down_pool_kernel(tiles_per_seq, n_t, a_ref, w_ref, h_ref, meta_ref, g_ref, wp_ref, sc_ref, gnext_ref,
                      o_ref, ntail_ref, m_ref, h2_even, h2_odd, ncarry, nbuf, pbuf, qbuf):
    s = pl.program_id(0)
    bm = h_ref.shape[0]
    g = g_ref[...]
    rows = HALO + bm
    r0 = FRONT + HALO
    split = (3 * D_MODEL) // 4

    @pl.when(s == 0)
    def _():
        h2_odd[...] = jnp.zeros(h2_odd.shape, F32)
        ncarry[...] = jnp.zeros(ncarry.shape, F32)

    @pl.when((s - 1) % tiles_per_seq == 0)
    def _():
        nbuf[FRONT:r0, :] = _rms(meta_ref[...], g)

    @pl.when((s - 1) % tiles_per_seq != 0)
    def _():
        nbuf[FRONT:r0, :] = ncarry[...]

    def project(h2_new, c0, c1):
        h2_new[:, c0:c1] = h_ref[:, c0:c1] + _dot(a_ref[...], w_ref[:, c0:c1])

    def step(h2_new, h2_old):
        if h2_new is not None:
            project(h2_new, 0, split)
        for buf in (nbuf, pbuf, qbuf):
            buf[0:FRONT, :] = jnp.zeros((FRONT, buf.shape[1]), F32)
        n = _rms(h2_old[...], g)
        nbuf[r0:r0 + bm, :] = n
        tail = n[bm - HALO:, :]
        ncarry[...] = tail
        ntail_ref[...] = tail
        for gi, win in enumerate(POOL_WINDOWS):
            cols = slice(gi * POOL_GROUP, (gi + 1) * POOL_GROUP)
            src, src_cols, shift, level = nbuf, cols, 1, 0
            while shift < win:
                dst = (pbuf, qbuf)[level % 2]
                dst[FRONT:FRONT + rows, :] = (src[FRONT:FRONT + rows, src_cols]
                                              + src[FRONT - shift:FRONT - shift + rows, src_cols])
                src, src_cols, shift, level = dst, slice(None), 2 * shift, level + 1
            p = src[r0:r0 + bm, src_cols] * (1.0 / win) - n[:, cols]
            y = _dot(p.astype(BF16), wp_ref[gi].astype(BF16)) * sc_ref[:, cols]
            o_ref[:, cols] = h2_old[:, cols] + y
        if h2_new is not None:
            project(h2_new, split, D_MODEL)
        m_ref[...] = _rms(o_ref[...], gnext_ref[...]).astype(BF16)

    @pl.when(jnp.logical_and(s % 2 == 0, s < n_t))
    def _():
        step(h2_even, h2_odd)

    @pl.when(jnp.logical_and(s % 2 == 1, s < n_t))
    def _():
        step(h2_odd, h2_even)

    @pl.when(s == n_t)
    def _():
        step(None, (h2_even, h2_odd)[(n_t - 1) % 2])


def _down_pool_p(a, w, h, h_s, g, wp, scale, g_next, bm):
    rows, kdim = a.shape
    n_t = rows // bm
    ng = len(POOL_WINDOWS)
    cur = lambda s: (jnp.minimum(s, n_t - 1), 0)
    prev = lambda s: (jnp.maximum(s - 1, 0), 0)
    vec = pl.BlockSpec((1, D_MODEL), lambda s: (0, 0))
    return pl.pallas_call(
        functools.partial(_down_pool_kernel, SEQ // bm, n_t),
        grid=(n_t + 1,),
        in_specs=[
            pl.BlockSpec((bm, kdim), cur),
            _resident((kdim, D_MODEL), lambda s: (0, 0)),
            pl.BlockSpec((bm, D_MODEL), cur),
            _resident((HALO, D_MODEL), lambda s: (META_ROW0 // HALO, 0)),
            vec,
            _resident((None, ng, POOL_GROUP, POOL_GROUP), lambda s: (0, 0, 0, 0)),
            vec, vec,
        ],
        out_specs=[
            pl.BlockSpec((bm, D_MODEL), prev),
            pl.BlockSpec((HALO, D_MODEL), prev),
            pl.BlockSpec((bm, D_MODEL), prev),
        ],
        out_shape=[
            jax.ShapeDtypeStruct((rows, D_MODEL), F32),
            jax.ShapeDtypeStruct((n_t * HALO, D_MODEL), F32),
            jax.ShapeDtypeStruct((rows, D_MODEL), BF16),
        ],
        scratch_shapes=[
            pltpu.VMEM((bm, D_MODEL), F32),
            pltpu.VMEM((bm, D_MODEL), F32),
            pltpu.VMEM((HALO, D_MODEL), F32),
            pltpu.VMEM((FRONT + HALO + bm, D_MODEL), F32),
            pltpu.VMEM((FRONT + HALO + bm, POOL_GROUP), F32),
            pltpu.VMEM((FRONT + HALO + bm, POOL_GROUP), F32),
        ],
        compiler_params=_params(1),
        name="down_pool_p",
    )(a, w, h, h_s, g, wp, scale, g_next)


def _pool_s_kernel(hfull_ref, hcol_ref, hp_ref, g_ref, wp_ref, sc_ref, o_ref, tail_ref, inv_sc, nbuf, sum_sc):
    j = pl.program_id(0)
    ns = N_SAMPLE_ROWS
    nb = N_SAMPLE_SEQ
    nh = POOL_HIST * nb

    @pl.when(j == 0)
    def _():
        xf = hfull_ref[...]
        inv_sc[...] = lax.rsqrt(jnp.mean(xf * xf, axis=-1, keepdims=True) + EPS)

    x = hcol_ref[...]
    n = ((x * inv_sc[...]) * g_ref[...])[0:ns, :]
    nbuf[0:nh, :] = hp_ref[...]
    nbuf[nh:nh + ns, :] = n
    tail_ref[...] = nbuf[ns:ns + nh, :]

    for gi, win in enumerate(POOL_WINDOWS):
        @pl.when(j == gi)
        def _(win=win):
            acc = n
            for k in range(1, win):
                acc = acc + nbuf[nh - k * nb:nh - k * nb + ns, :]
            sum_sc[...] = acc * (1.0 / win)

    p = sum_sc[...] - n
    y = _dot(p.astype(BF16), wp_ref[...].astype(BF16)) * sc_ref[...]
    o_ref[0:ns, :] = x[0:ns, :] + y
    o_ref[ns:, :] = x[ns:, :]


def _pool_s(h_s, state_pool, g, wp, scale):
    ns = N_SAMPLE_ROWS
    nb = N_SAMPLE_SEQ
    nh = POOL_HIST * nb
    pg = POOL_GROUP
    n_g = len(POOL_WINDOWS)
    state_spec = pl.BlockSpec((nh, pg), lambda j: (0, j))
    return pl.pallas_call(
        _pool_s_kernel,
        grid=(n_g,),
        in_specs=[
            _resident((S_ROWS, D_MODEL), lambda j: (0, 0)),
            pl.BlockSpec((S_ROWS, pg), lambda j: (0, j)),
            state_spec,
            pl.BlockSpec((1, pg), lambda j: (0, j)),
            pl.BlockSpec((None, None, pg, pg), lambda j: (0, j, 0, 0)),
            pl.BlockSpec((1, pg), lambda j: (0, j)),
        ],
        out_specs=[pl.BlockSpec((S_ROWS, pg), lambda j: (0, j)), state_spec],
        out_shape=[
            jax.ShapeDtypeStruct((S_ROWS, D_MODEL), F32),
            jax.ShapeDtypeStruct((nh, D_MODEL), F32),
        ],
        scratch_shapes=[
            pltpu.VMEM((S_ROWS, 1), F32),
            pltpu.VMEM((nh + ns, pg), F32),
            pltpu.VMEM((ns, pg), F32),
        ],
        compiler_params=_params(1),
        name="pool_s",
    )(h_s, h_s, state_pool, g, wp, scale)


BM_P = 1024
BN_MIX = 512
BM_CONV_OUT = 512
BM_GATE_UP = 2048
BN_P = 512
BM_DOWN = 512
BK_DOWN = D_FF
BM_POOL = 256
BN_S = 256
BN_S_FF = 512
BK_S = 512

assert SEQ % BM_P == 0 and SEQ % BM_POOL == 0 and BM_POOL % HALO == 0 and META_ROW0 % HALO == 0
assert D_MODEL % BN_MIX == 0 and D_MODEL % BN_S == 0 and BN_MIX % MXU_COLS == 0 and BN_S % MXU_COLS == 0
assert D_FF % BN_P == 0 and D_FF % BN_S_FF == 0 and D_FF % BK_S == 0 and D_MODEL % BK_S == 0


def kernel(x_prompt, x_sample, state_conv, state_pool, meta_tokens, norm_mix, norm_ffn, norm_final,
           conv_w_in, conv_w_dw, conv_w_out, pool_w, pool_scale, ffn_w_gate_up, ffn_w_down):
    d = D_MODEL
    nb, nt = N_SAMPLE_SEQ, SAMPLE_T
    batch = x_prompt.shape[0]
    assert x_prompt.shape == (batch, SEQ, d) and x_sample.shape == (nb, nt, d)
    assert state_conv.shape == (1, nb, CONV_HIST, d) and state_pool.shape == (1, nb, POOL_HIST, d)
    assert meta_tokens.shape == (N_META, d) and ffn_w_gate_up.shape == (2, d, 2 * D_FF)
    assert (batch * SEQ) % BM_GATE_UP == 0 and (batch * SEQ) % BM_DOWN == 0 and (batch * SEQ) % BM_CONV_OUT == 0

    h_p = x_prompt.reshape(batch * SEQ, d)
    hc = state_conv[0].transpose(1, 0, 2).reshape(CONV_HIST * nb, d)
    hp = state_pool[0].transpose(1, 0, 2).reshape(POOL_HIST * nb, d)
    row = lambda v: v.reshape(1, d)
    g_mix, g_ffn, g_fin = norm_mix, norm_ffn, row(norm_final)

    rows_p = batch * SEQ
    n_j_mix, n_j_ff = d // BN_MIX, D_FF // BN_P
    n_mix, n_ff = (rows_p // BM_P) * n_j_mix, (rows_p // BM_GATE_UP) * n_j_ff
    step_mix = lambda i, j: i * n_j_mix + j
    step_ff = lambda i, j: i * n_j_ff + j
    casts_mix = [_cast_job(conv_w_out, 0, n_mix, 0, step_mix), _cast_job(ffn_w_gate_up, 0, n_mix, 0, step_mix)]
    casts_ff0 = [_cast_job(ffn_w_down, 0, n_ff, 0, step_ff), _cast_job(ffn_w_gate_up, 1, n_ff, 1, step_ff)]
    casts_ff1 = [_cast_job(ffn_w_down, 1, n_ff, 0, step_ff)]

    z_s, uts, umeta, wb, wc, wv, h_s = _mix_conv_s(
        x_sample, meta_tokens, row(g_mix[0]), conv_w_in, conv_w_dw[0], hc, BN_S)
    z_p, utail_p, w_out, w_gu0 = _mix_conv_p(
        h_p, row(g_mix[0]), wb, wc, wv, conv_w_dw[0], umeta, BM_P, BN_MIX, casts_mix)
    h_p, m_p = _proj_res_p(z_p, w_out, h_p, row(g_ffn[0]), BM_CONV_OUT, D_MODEL, "emit", "conv_out_p")
    a_p, w_dn0, w_gu1 = _gate_up_p(m_p, w_gu0, BM_GATE_UP, BN_P, casts_ff0, "gate_up0_p")
    h_s = _ffn_s(h_s, row(g_ffn[0]), g_fin, w_gu0, w_dn0, BN_S_FF, False, "conv_out_ffn0_s", pre=(z_s, w_out, BK_S))
    h_p1, ntail_p, m_p = _down_pool_p(a_p, w_dn0, h_p, h_s, row(g_mix[1]), pool_w, row(pool_scale[0]),
                                      row(g_ffn[1]), BM_POOL)
    h_s1, pool_tail_s = _pool_s(h_s, hp, row(g_mix[1]), pool_w, row(pool_scale[0]))
    a_p, w_dn1 = _gate_up_p(m_p, w_gu1, BM_GATE_UP, BN_P, casts_ff1, "gate_up1_p")
    y_s = _ffn_s(h_s1, row(g_ffn[1]), g_fin, w_gu1, w_dn1, BN_S_FF, True, "ffn1_s")
    (y_p,) = _proj_res_p(a_p, w_dn1, h_p1, g_fin, BM_DOWN, BK_DOWN, "final", "down1_p")

    y_prompt = y_p.reshape(batch, SEQ, d)
    y_sample = y_s
    tps = SEQ // BM_P
    new_conv_prompt = utail_p.reshape(batch, tps, TAIL, d)[:, tps - 1, TAIL - CONV_HIST:, :][None]
    tpp = SEQ // BM_POOL
    new_pool_prompt = ntail_p.reshape(batch, tpp, HALO, d)[:, tpp - 1, HALO - POOL_HIST:, :][None]
    new_conv_sample = uts.reshape(CONV_HIST, nb, d).transpose(1, 0, 2)[None]
    new_pool_sample = pool_tail_s.reshape(POOL_HIST, nb, d).transpose(1, 0, 2)[None]
    return (y_prompt, y_sample, new_conv_prompt, new_pool_prompt, new_conv_sample, new_pool_sample)
```

```python
import functools
from typing import NamedTuple

import jax
import jax.numpy as jnp
from jax import lax
from jax.experimental import pallas as pl
from jax.experimental.pallas import tpu as pltpu

D_MODEL = 2048
D_FF = 5632
N_META = 16
SEQ = 2048
N_SAMPLE_SEQ = 128
SAMPLE_T = 8
CONV_HIST = 2
POOL_WINDOWS = (2, 4, 8, 16)
POOL_GROUP = D_MODEL // len(POOL_WINDOWS)
POOL_HIST = 15
EPS = 1e-6

N_SAMPLE_ROWS = N_SAMPLE_SEQ * SAMPLE_T
S_ROWS = N_SAMPLE_ROWS + N_META
META_ROW0 = N_SAMPLE_ROWS
TAIL = 8
HALO = 16
MXU_COLS = 256
FRONT = 8

V7X_SCOPED_VMEM_BYTES = 60000 * 1024

BF16 = jnp.bfloat16
F32 = jnp.float32


def _rms(x, g):
    ms = jnp.mean(x * x, axis=-1, keepdims=True)
    return (x * lax.rsqrt(ms + EPS)) * g


def _dot(a, b):
    return jnp.dot(a, b, preferred_element_type=F32)


def _params(n_axes):
    return pltpu.CompilerParams(
        dimension_semantics=("arbitrary",) * n_axes,
        vmem_limit_bytes=V7X_SCOPED_VMEM_BYTES,
    )


def _resident(shape, index_map):
    return pl.BlockSpec(shape, index_map, pipeline_mode=pl.Buffered(1))


class _CastJob(NamedTuple):
    src: jax.Array
    in_spec: pl.BlockSpec
    out_spec: pl.BlockSpec
    out_shape: jax.ShapeDtypeStruct


def _cast_job(w_stack, layer, n_steps, axis, step_of):
    _, r, c = w_stack.shape
    if axis == 0:
        blk = (r // n_steps, c)
        in_idx = lambda *ids: (layer, step_of(*ids), 0)
        out_idx = lambda *ids: (step_of(*ids), 0)
    else:
        blk = (r, c // n_steps)
        in_idx = lambda *ids: (layer, 0, step_of(*ids))
        out_idx = lambda *ids: (0, step_of(*ids))
    assert blk[0] * (n_steps if axis == 0 else 1) == r and blk[1] * (n_steps if axis == 1 else 1) == c
    return _CastJob(w_stack, pl.BlockSpec((None,) + blk, in_idx), pl.BlockSpec(blk, out_idx),
                    jax.ShapeDtypeStruct((r, c), BF16))


def _run_casts(src_refs, dst_refs):
    for src, dst in zip(src_refs, dst_refs):
        dst[...] = src[...].astype(BF16)


def _conv3(w, u, ubuf, off):
    t = u.shape[0]
    return (w[2:3] * u + w[1:2] * ubuf[off - 1:off - 1 + t, :] + w[0:1] * ubuf[off - 2:off - 2 + t, :])


def _mix_p_kernel(tiles_per_seq, n_cast, *refs):
    h_ref, g_ref, wb_ref, wc_ref, wv_ref, wdw_ref, umeta_ref = refs[:7]
    cast_src, refs = refs[7:7 + n_cast], refs[7 + n_cast:]
    z_ref, utail_ref = refs[:2]
    cast_dst, (n_sc, ubuf, carry) = refs[2:2 + n_cast], refs[2 + n_cast:]
    i = pl.program_id(0)
    j = pl.program_id(1)
    bm = h_ref.shape[0]
    _run_casts(cast_src, cast_dst)

    @pl.when(j == 0)
    def _():
        n_sc[...] = _rms(h_ref[...], g_ref[...]).astype(BF16)

    @pl.when(i % tiles_per_seq == 0)
    def _():
        ubuf[0:TAIL, :] = umeta_ref[...]

    @pl.when(i % tiles_per_seq != 0)
    def _():
        ubuf[0:TAIL, :] = carry[j]

    n = n_sc[...]
    for c0 in range(0, z_ref.shape[1], MXU_COLS):
        cols = slice(c0, c0 + MXU_COLS)
        b = _dot(n, wb_ref[:, cols])
        u = _dot(n, wc_ref[:, cols]) * _dot(n, wv_ref[:, cols])
        ubuf[TAIL:TAIL + bm, cols] = u
        z_ref[:, cols] = (b * _conv3(wdw_ref[:, cols], u, ubuf.at[:, cols], TAIL)).astype(BF16)
    tail = ubuf[bm:bm + TAIL, :]
    carry[j] = tail
    utail_ref[...] = tail


def _mix_s_kernel(xs_ref, meta_ref, g_ref, wb_ref, wc_ref, wv_ref, wdw_ref, hc_ref,
                  z_ref, uts_ref, umeta_ref, wb_o, wc_o, wv_o, h0_ref, n_sc, ubuf, mbuf):
    j = pl.program_id(0)
    ns = N_SAMPLE_ROWS
    nb = N_SAMPLE_SEQ
    nm = S_ROWS - ns

    @pl.when(j == 0)
    def _():
        for t in range(SAMPLE_T):
            h0_ref[t * nb:(t + 1) * nb, :] = xs_ref[:, t, :]
        h0_ref[ns:, :] = meta_ref[...]
        n_sc[...] = _rms(h0_ref[...], g_ref[...]).astype(BF16)

    wb_o[...] = wb_ref[...].astype(BF16)
    wc_o[...] = wc_ref[...].astype(BF16)
    wv_o[...] = wv_ref[...].astype(BF16)
    n = n_sc[...]
    nh = CONV_HIST * nb
    ubuf[0:nh, :] = hc_ref[...]
    mbuf[0:TAIL, :] = jnp.zeros((TAIL, mbuf.shape[1]), F32)
    for c0 in range(0, z_ref.shape[1], MXU_COLS):
        cols = slice(c0, c0 + MXU_COLS)
        b = _dot(n, wb_o[:, cols])
        u = _dot(n, wc_o[:, cols]) * _dot(n, wv_o[:, cols])
        w = wdw_ref[:, cols]

        us = u[0:ns, :]
        ubuf[nh:nh + ns, cols] = us
        conv_s = (w[2:3] * us + w[1:2] * ubuf[nb:nb + ns, cols] + w[0:1] * ubuf[0:ns, cols])
        z_ref[0:ns, cols] = (b[0:ns, :] * conv_s).astype(BF16)

        um = u[ns:, :]
        mbuf[TAIL:TAIL + nm, cols] = um
        z_ref[ns:, cols] = (b[ns:, :] * _conv3(w, um, mbuf.at[:, cols], TAIL)).astype(BF16)
    uts_ref[...] = ubuf[ns:ns + nh, :]
    umeta_ref[...] = mbuf[N_META:N_META + TAIL, :]


def _mix_conv_p(h, g, wb, wc, wv, w_dw, umeta, bm, bn, casts):
    rows = h.shape[0]
    n_i, n_j = rows // bm, D_MODEL // bn
    wspec = pl.BlockSpec((D_MODEL, bn), lambda i, j: (0, j))
    return pl.pallas_call(
        functools.partial(_mix_p_kernel, SEQ // bm, len(casts)),
        grid=(n_i, n_j),
        in_specs=[
            pl.BlockSpec((bm, D_MODEL), lambda i, j: (i, 0)),
            pl.BlockSpec((1, D_MODEL), lambda i, j: (0, 0)),
            wspec, wspec, wspec,
            pl.BlockSpec((3, bn), lambda i, j: (0, j)),
            pl.BlockSpec((TAIL, bn), lambda i, j: (0, j)),
        ] + [c.in_spec for c in casts],
        out_specs=[
            pl.BlockSpec((bm, bn), lambda i, j: (i, j)),
            pl.BlockSpec((TAIL, bn), lambda i, j: (i, j)),
        ] + [c.out_spec for c in casts],
        out_shape=[
            jax.ShapeDtypeStruct((rows, D_MODEL), BF16),
            jax.ShapeDtypeStruct((n_i * TAIL, D_MODEL), F32),
        ] + [c.out_shape for c in casts],
        scratch_shapes=[
            pltpu.VMEM((bm, D_MODEL), BF16),
            pltpu.VMEM((TAIL + bm, bn), F32),
            pltpu.VMEM((n_j, TAIL, bn), F32),
        ],
        compiler_params=_params(2),
        name="mix_conv_p",
    )(h, g, wb, wc, wv, w_dw, umeta, *[c.src for c in casts])


def _mix_conv_s(x_sample, meta, g, w_in, w_dw, hc, bn):
    n_j = D_MODEL // bn
    nb = N_SAMPLE_SEQ
    nm = S_ROWS - N_SAMPLE_ROWS
    wspec = lambda part: pl.BlockSpec((None, D_MODEL, bn), lambda j: (0, 0, part * n_j + j))
    wout = pl.BlockSpec((D_MODEL, bn), lambda j: (0, j))
    wshape = jax.ShapeDtypeStruct((D_MODEL, D_MODEL), BF16)
    return pl.pallas_call(
        _mix_s_kernel,
        grid=(n_j,),
        in_specs=[
            _resident((nb, SAMPLE_T, D_MODEL), lambda j: (0, 0, 0)),
            _resident((N_META, D_MODEL), lambda j: (0, 0)),
            pl.BlockSpec((1, D_MODEL), lambda j: (0, 0)),
            wspec(0), wspec(1), wspec(2),
            pl.BlockSpec((3, bn), lambda j: (0, j)),
            pl.BlockSpec((CONV_HIST * nb, bn), lambda j: (0, j)),
        ],
        out_specs=[
            pl.BlockSpec((S_ROWS, bn), lambda j: (0, j)),
            pl.BlockSpec((CONV_HIST * nb, bn), lambda j: (0, j)),
            pl.BlockSpec((TAIL, bn), lambda j: (0, j)),
            wout, wout, wout,
            pl.BlockSpec((S_ROWS, D_MODEL), lambda j: (0, 0)),
        ],
        out_shape=[
            jax.ShapeDtypeStruct((S_ROWS, D_MODEL), BF16),
            jax.ShapeDtypeStruct((CONV_HIST * nb, D_MODEL), F32),
            jax.ShapeDtypeStruct((TAIL, D_MODEL), F32),
            wshape, wshape, wshape,
            jax.ShapeDtypeStruct((S_ROWS, D_MODEL), F32),
        ],
        scratch_shapes=[
            pltpu.VMEM((S_ROWS, D_MODEL), BF16),
            pltpu.VMEM((CONV_HIST * nb + N_SAMPLE_ROWS, bn), F32),
            pltpu.VMEM((TAIL + nm, bn), F32),
        ],
        compiler_params=_params(1),
        name="mix_conv_s",
    )(x_sample, meta, g, w_in, w_in, w_in, w_dw, hc)


def _proj_res_body(norm, k, n_k, x_ref, w_ref, h_ref, g_ref, o_ref, m_ref):
    if n_k == 1:
        acc = h_ref[...] + _dot(x_ref[...], w_ref[...])
        o_ref[...] = _rms(acc, g_ref[...]) if norm == "final" else acc
        if norm == "emit":
            m_ref[...] = _rms(acc, g_ref[...]).astype(BF16)
        return

    @pl.when(k == 0)
    def _():
        o_ref[...] = h_ref[...] + _dot(x_ref[...], w_ref[...])

    @pl.when(k != 0)
    def _():
        o_ref[...] += _dot(x_ref[...], w_ref[...])

    if norm is not None:
        @pl.when(k == n_k - 1)
        def _():
            y = _rms(o_ref[...], g_ref[...])
            if norm == "final":
                o_ref[...] = y
            else:
                m_ref[...] = y.astype(BF16)


def _proj_res_p_kernel(norm, n_k, x_ref, w_ref, h_ref, g_ref, o_ref, m_ref=None):
    _proj_res_body(norm, pl.program_id(1), n_k, x_ref, w_ref, h_ref, g_ref, o_ref, m_ref)


def _proj_res_p(x, w, h, g, bm, bk, norm, name):
    rows, kdim = x.shape
    n_i, n_k = rows // bm, kdim // bk
    wspec = _resident if n_k == 1 else pl.BlockSpec
    row_spec = pl.BlockSpec((bm, D_MODEL), lambda i, k: (i, 0))
    out_specs, out_shape = [row_spec], [jax.ShapeDtypeStruct((rows, D_MODEL), F32)]
    if norm == "emit":
        out_specs.append(row_spec)
        out_shape.append(jax.ShapeDtypeStruct((rows, D_MODEL), BF16))
    return pl.pallas_call(
        functools.partial(_proj_res_p_kernel, norm, n_k),
        grid=(n_i, n_k),
        in_specs=[
            pl.BlockSpec((bm, bk), lambda i, k: (i, k)),
            wspec((bk, D_MODEL), lambda i, k: (k, 0)),
            row_spec,
            pl.BlockSpec((1, D_MODEL), lambda i, k: (0, 0)),
        ],
        out_specs=out_specs,
        out_shape=out_shape,
        compiler_params=_params(2),
        name=name,
    )(x, w, h, g)


def _swiglu_chunks(m, wg_ref, wu_ref, a_ref):
    for c0 in range(0, a_ref.shape[1], MXU_COLS):
        cols = slice(c0, c0 + MXU_COLS)
        gate = _dot(m, wg_ref[:, cols])
        up = _dot(m, wu_ref[:, cols])
        a_ref[:, cols] = (gate * (1.0 / (1.0 + jnp.exp(-gate))) * up).astype(BF16)


def _gate_up_p_kernel(n_cast, m_ref, wg_ref, wu_ref, *refs):
    _run_casts(refs[:n_cast], refs[n_cast + 1:])
    _swiglu_chunks(m_ref[...], wg_ref, wu_ref, refs[n_cast])


def _ffn_s_kernel(sample_out, n_pre, n_j, *refs):
    h_ref, refs = refs[0], refs[1:]
    if n_pre:
        (x_ref, wpre_ref), refs = refs[:2], refs[2:]
    g_ref, gfin_ref, wg_ref, wu_ref, wd_ref, o_ref, m_sc, a_sc = refs[:8]
    rest = list(refs[8:])
    hsrc = rest.pop(0) if n_pre else h_ref
    acc = rest.pop(0) if sample_out else o_ref
    s = pl.program_id(0)
    j = s - n_pre

    if n_pre:
        @pl.when(s < n_pre)
        def _():
            _proj_res_body(None, s, n_pre, x_ref, wpre_ref, h_ref, None, hsrc, None)

    @pl.when(s >= n_pre)
    def _():
        @pl.when(j == 0)
        def _():
            m_sc[...] = _rms(hsrc[...], g_ref[...]).astype(BF16)

        _swiglu_chunks(m_sc[...], wg_ref, wu_ref, a_sc)
        _proj_res_body(None, j, n_j, a_sc, wd_ref, hsrc, None, acc, None)

        if sample_out:
            @pl.when(j == n_j - 1)
            def _():
                y = _rms(acc[0:N_SAMPLE_ROWS, :], gfin_ref[...])
                for t in range(SAMPLE_T):
                    o_ref[:, t, :] = y[t * N_SAMPLE_SEQ:(t + 1) * N_SAMPLE_SEQ, :]


def _ffn_s(h, g, g_fin, w_gu, w_dn, bn, sample_out, name, pre=None):
    rows = h.shape[0]
    n_j = D_FF // bn
    n_pre = 0 if pre is None else pre[0].shape[1] // pre[2]
    ffn_step = lambda s: jnp.maximum(s - n_pre, 0)
    row_spec = pl.BlockSpec((1, D_MODEL), lambda s: (0, 0))
    o_shape = (N_SAMPLE_SEQ, SAMPLE_T, D_MODEL) if sample_out else (rows, D_MODEL)
    o_index = (0,) * len(o_shape)
    in_specs, args = [_resident((rows, D_MODEL), lambda s: (0, 0))], [h]
    scratch = [pltpu.VMEM((rows, D_MODEL), BF16), pltpu.VMEM((rows, bn), BF16)]
    if n_pre:
        x, w_pre, bk = pre
        pre_step = lambda s: jnp.minimum(s, n_pre - 1)
        in_specs += [pl.BlockSpec((rows, bk), lambda s: (0, pre_step(s))),
                     pl.BlockSpec((bk, D_MODEL), lambda s: (pre_step(s), 0))]
        args += [x, w_pre]
        scratch.append(pltpu.VMEM((rows, D_MODEL), F32))
    if sample_out:
        scratch.append(pltpu.VMEM((rows, D_MODEL), F32))
    in_specs += [
        row_spec, row_spec,
        pl.BlockSpec((D_MODEL, bn), lambda s: (0, ffn_step(s))),
        pl.BlockSpec((D_MODEL, bn), lambda s: (0, n_j + ffn_step(s))),
        pl.BlockSpec((bn, D_MODEL), lambda s: (ffn_step(s), 0)),
    ]
    return pl.pallas_call(
        functools.partial(_ffn_s_kernel, sample_out, n_pre, n_j),
        grid=(n_pre + n_j,),
        in_specs=in_specs,
        out_specs=pl.BlockSpec(o_shape, lambda s: o_index),
        out_shape=jax.ShapeDtypeStruct(o_shape, F32),
        scratch_shapes=scratch,
        compiler_params=_params(1),
        name=name,
    )(*args, g, g_fin, w_gu, w_gu, w_dn)


def _gate_up_p(m, w_gu, bm, bn, casts, name):
    rows = m.shape[0]
    n_i, n_j = rows // bm, D_FF // bn
    return pl.pallas_call(
        functools.partial(_gate_up_p_kernel, len(casts)),
        grid=(n_i, n_j),
        in_specs=[
            pl.BlockSpec((bm, D_MODEL), lambda i, j: (i, 0)),
            pl.BlockSpec((D_MODEL, bn), lambda i, j: (0, j)),
            pl.BlockSpec((D_MODEL, bn), lambda i, j: (0, n_j + j)),
        ] + [c.in_spec for c in casts],
        out_specs=[pl.BlockSpec((bm, bn), lambda i, j: (i, j))] + [c.out_spec for c in casts],
        out_shape=[jax.ShapeDtypeStruct((rows, D_FF), BF16)] + [c.out_shape for c in casts],
        compiler_params=_params(2),
        name=name,
    )(m, w_gu, w_gu, *[c.src for c in casts])


def _down_pool_kernel(tiles_per_seq, n_t, a_ref, w_ref, h_ref, meta_ref, g_ref, wp_ref, sc_ref, gnext_ref,
                      o_ref, ntail_ref, m_ref, h2_even, h2_odd, ncarry, nbuf, pbuf, qbuf):
    s = pl.program_id(0)
    bm = h_ref.shape[0]
    g = g_ref[...]
    rows = HALO + bm
    r0 = FRONT + HALO
    split = (3 * D_MODEL) // 4

    @pl.when((s - 1) % tiles_per_seq == 0)
    def _():
        nbuf[FRONT:r0, :] = _rms(meta_ref[...], g)

    @pl.when(jnp.logical_and(s >= 1, (s - 1) % tiles_per_seq != 0))
    def _():
        nbuf[FRONT:r0, :] = ncarry[...]

    def project(h2_new, c0, c1):
        h2_new[:, c0:c1] = h_ref[:, c0:c1] + _dot(a_ref[...], w_ref[:, c0:c1])

    def step(h2_new, h2_old):
        if h2_old is None:
            project(h2_new, 0, D_MODEL)
            return
        if h2_new is not None:
            project(h2_new, 0, split)
        for buf in (nbuf, pbuf, qbuf):
            buf[0:FRONT, :] = jnp.zeros((FRONT, buf.shape[1]), F32)
        n = _rms(h2_old[...], g)
        nbuf[r0:r0 + bm, :] = n
        tail = n[bm - HALO:, :]
        ncarry[...] = tail
        ntail_ref[...] = tail
        for gi, win in enumerate(POOL_WINDOWS):
            cols = slice(gi * POOL_GROUP, (gi + 1) * POOL_GROUP)
            src, src_cols, shift, level = nbuf, cols, 1, 0
            while shift < win:
                dst = (pbuf, qbuf)[level % 2]
                dst[FRONT:FRONT + rows, :] = (src[FRONT:FRONT + rows, src_cols]
                                              + src[FRONT - shift:FRONT - shift + rows, src_cols])
                src, src_cols, shift, level = dst, slice(None), 2 * shift, level + 1
            p = src[r0:r0 + bm, src_cols] * (1.0 / win) - n[:, cols]
            y = _dot(p.astype(BF16), wp_ref[gi].astype(BF16)) * sc_ref[:, cols]
            o_ref[:, cols] = h2_old[:, cols] + y
        if h2_new is not None:
            project(h2_new, split, D_MODEL)
        m_ref[...] = _rms(o_ref[...], gnext_ref[...]).astype(BF16)

    @pl.when(s == 0)
    def _():
        step(h2_even, None)

    @pl.when(jnp.logical_and(s % 2 == 0, jnp.logical_and(s > 0, s < n_t)))
    def _():
        step(h2_even, h2_odd)

    @pl.when(jnp.logical_and(s % 2 == 1, s < n_t))
    def _():
        step(h2_odd, h2_even)

    @pl.when(s == n_t)
    def _():
        step(None, (h2_even, h2_odd)[(n_t - 1) % 2])


def _down_pool_p(a, w, h, h_s, g, wp, scale, g_next, bm):
    rows, kdim = a.shape
    n_t = rows // bm
    ng = len(POOL_WINDOWS)
    cur = lambda s: (jnp.minimum(s, n_t - 1), 0)
    prev = lambda s: (jnp.maximum(s - 1, 0), 0)
    vec = pl.BlockSpec((1, D_MODEL), lambda s: (0, 0))
    return pl.pallas_call(
        functools.partial(_down_pool_kernel, SEQ // bm, n_t),
        grid=(n_t + 1,),
        in_specs=[
            pl.BlockSpec((bm, kdim), cur),
            _resident((kdim, D_MODEL), lambda s: (0, 0)),
            pl.BlockSpec((bm, D_MODEL), cur),
            _resident((HALO, D_MODEL), lambda s: (META_ROW0 // HALO, 0)),
            vec,
            _resident((None, ng, POOL_GROUP, POOL_GROUP), lambda s: (0, 0, 0, 0)),
            vec, vec,
        ],
        out_specs=[
            pl.BlockSpec((bm, D_MODEL), prev),
            pl.BlockSpec((HALO, D_MODEL), prev),
            pl.BlockSpec((bm, D_MODEL), prev),
        ],
        out_shape=[
            jax.ShapeDtypeStruct((rows, D_MODEL), F32),
            jax.ShapeDtypeStruct((n_t * HALO, D_MODEL), F32),
            jax.ShapeDtypeStruct((rows, D_MODEL), BF16),
        ],
        scratch_shapes=[
            pltpu.VMEM((bm, D_MODEL), F32),
            pltpu.VMEM((bm, D_MODEL), F32),
            pltpu.VMEM((HALO, D_MODEL), F32),
            pltpu.VMEM((FRONT + HALO + bm, D_MODEL), F32),
            pltpu.VMEM((FRONT + HALO + bm, POOL_GROUP), F32),
            pltpu.VMEM((FRONT + HALO + bm, POOL_GROUP), F32),
        ],
        compiler_params=_params(1),
        name="down_pool_p",
    )(a, w, h, h_s, g, wp, scale, g_next)


def _pool_s_kernel(hfull_ref, hcol_ref, hp_ref, g_ref, wp_ref, sc_ref, o_ref, tail_ref, inv_sc, nbuf, sum_sc):
    j = pl.program_id(0)
    ns = N_SAMPLE_ROWS
    nb = N_SAMPLE_SEQ
    nh = POOL_HIST * nb

    @pl.when(j == 0)
    def _():
        xf = hfull_ref[...]
        inv_sc[...] = lax.rsqrt(jnp.mean(xf * xf, axis=-1, keepdims=True) + EPS)

    x = hcol_ref[...]
    n = ((x * inv_sc[...]) * g_ref[...])[0:ns, :]
    nbuf[0:nh, :] = hp_ref[...]
    nbuf[nh:nh + ns, :] = n
    tail_ref[...] = nbuf[ns:ns + nh, :]

    for gi, win in enumerate(POOL_WINDOWS):
        @pl.when(j == gi)
        def _(win=win):
            acc = n
            for k in range(1, win):
                acc = acc + nbuf[nh - k * nb:nh - k * nb + ns, :]
            sum_sc[...] = acc * (1.0 / win)

    p = sum_sc[...] - n
    y = _dot(p.astype(BF16), wp_ref[...].astype(BF16)) * sc_ref[...]
    o_ref[...] = x[0:ns, :] + y


def _pool_s(h_s, state_pool, g, wp, scale):
    ns = N_SAMPLE_ROWS
    nb = N_SAMPLE_SEQ
    nh = POOL_HIST * nb
    pg = POOL_GROUP
    n_g = len(POOL_WINDOWS)
    state_spec = pl.BlockSpec((nh, pg), lambda j: (0, j))
    return pl.pallas_call(
        _pool_s_kernel,
        grid=(n_g,),
        in_specs=[
            _resident((S_ROWS, D_MODEL), lambda j: (0, 0)),
            pl.BlockSpec((S_ROWS, pg), lambda j: (0, j)),
            state_spec,
            pl.BlockSpec((1, pg), lambda j: (0, j)),
            pl.BlockSpec((None, None, pg, pg), lambda j: (0, j, 0, 0)),
            pl.BlockSpec((1, pg), lambda j: (0, j)),
        ],
        out_specs=[pl.BlockSpec((ns, pg), lambda j: (0, j)), state_spec],
        out_shape=[
            jax.ShapeDtypeStruct((ns, D_MODEL), F32),
            jax.ShapeDtypeStruct((nh, D_MODEL), F32),
        ],
        scratch_shapes=[
            pltpu.VMEM((S_ROWS, 1), F32),
            pltpu.VMEM((nh + ns, pg), F32),
            pltpu.VMEM((ns, pg), F32),
        ],
        compiler_params=_params(1),
        name="pool_s",
    )(h_s, h_s, state_pool, g, wp, scale)


BM_P = 1024
BN_MIX = 512
BM_CONV_OUT = 512
BM_GATE_UP = 2048
BN_P = 512
BM_DOWN = 512
BK_DOWN = D_FF
BM_POOL = 256
BN_S = 256
BN_S_FF = 512
BK_S = 512

assert SEQ % BM_P == 0 and SEQ % BM_POOL == 0 and BM_POOL % HALO == 0 and META_ROW0 % HALO == 0
assert D_MODEL % BN_MIX == 0 and D_MODEL % BN_S == 0 and BN_MIX % MXU_COLS == 0 and BN_S % MXU_COLS == 0
assert D_FF % BN_P == 0 and D_FF % BN_S_FF == 0 and D_FF % BK_S == 0 and D_MODEL % BK_S == 0


def kernel(x_prompt, x_sample, state_conv, state_pool, meta_tokens, norm_mix, norm_ffn, norm_final,
           conv_w_in, conv_w_dw, conv_w_out, pool_w, pool_scale, ffn_w_gate_up, ffn_w_down):
    d = D_MODEL
    nb, nt = N_SAMPLE_SEQ, SAMPLE_T
    batch = x_prompt.shape[0]
    assert x_prompt.shape == (batch, SEQ, d) and x_sample.shape == (nb, nt, d)
    assert state_conv.shape == (1, nb, CONV_HIST, d) and state_pool.shape == (1, nb, POOL_HIST, d)
    assert meta_tokens.shape == (N_META, d) and ffn_w_gate_up.shape == (2, d, 2 * D_FF)
    assert (batch * SEQ) % BM_GATE_UP == 0 and (batch * SEQ) % BM_DOWN == 0 and (batch * SEQ) % BM_CONV_OUT == 0

    h_p = x_prompt.reshape(batch * SEQ, d)
    hc = state_conv[0].transpose(1, 0, 2).reshape(CONV_HIST * nb, d)
    hp = state_pool[0].transpose(1, 0, 2).reshape(POOL_HIST * nb, d)
    row = lambda v: v.reshape(1, d)
    g_mix, g_ffn, g_fin = norm_mix, norm_ffn, row(norm_final)

    rows_p = batch * SEQ
    n_j_mix, n_j_ff = d // BN_MIX, D_FF // BN_P
    n_mix, n_ff = (rows_p // BM_P) * n_j_mix, (rows_p // BM_GATE_UP) * n_j_ff
    step_mix = lambda i, j: i * n_j_mix + j
    step_ff = lambda i, j: i * n_j_ff + j
    casts_mix = [_cast_job(conv_w_out, 0, n_mix, 0, step_mix), _cast_job(ffn_w_gate_up, 0, n_mix, 0, step_mix)]
    casts_ff0 = [_cast_job(ffn_w_down, 0, n_ff, 0, step_ff), _cast_job(ffn_w_gate_up, 1, n_ff, 1, step_ff)]
    casts_ff1 = [_cast_job(ffn_w_down, 1, n_ff, 0, step_ff)]

    z_s, uts, umeta, wb, wc, wv, h_s = _mix_conv_s(
        x_sample, meta_tokens, row(g_mix[0]), conv_w_in, conv_w_dw[0], hc, BN_S)
    z_p, utail_p, w_out, w_gu0 = _mix_conv_p(
        h_p, row(g_mix[0]), wb, wc, wv, conv_w_dw[0], umeta, BM_P, BN_MIX, casts_mix)
    h_p, m_p = _proj_res_p(z_p, w_out, h_p, row(g_ffn[0]), BM_CONV_OUT, D_MODEL, "emit", "conv_out_p")
    a_p, w_dn0, w_gu1 = _gate_up_p(m_p, w_gu0, BM_GATE_UP, BN_P, casts_ff0, "gate_up0_p")
    h_s = _ffn_s(h_s, row(g_ffn[0]), g_fin, w_gu0, w_dn0, BN_S_FF, False, "conv_out_ffn0_s", pre=(z_s, w_out, BK_S))
    h_p1, ntail_p, m_p = _down_pool_p(a_p, w_dn0, h_p, h_s, row(g_mix[1]), pool_w, row(pool_scale[0]),
                                      row(g_ffn[1]), BM_POOL)
    h_s1, pool_tail_s = _pool_s(h_s, hp, row(g_mix[1]), pool_w, row(pool_scale[0]))
    a_p, w_dn1 = _gate_up_p(m_p, w_gu1, BM_GATE_UP, BN_P, casts_ff1, "gate_up1_p")
    y_s = _ffn_s(h_s1, row(g_ffn[1]), g_fin, w_gu1, w_dn1, BN_S_FF, True, "ffn1_s")
    (y_p,) = _proj_res_p(a_p, w_dn1, h_p1, g_fin, BM_DOWN, BK_DOWN, "final", "down1_p")

    y_prompt = y_p.reshape(batch, SEQ, d)
    y_sample = y_s
    tps = SEQ // BM_P
    new_conv_prompt = utail_p.reshape(batch, tps, TAIL, d)[:, tps - 1, TAIL - CONV_HIST:, :][None]
    tpp = SEQ // BM_POOL
    new_pool_prompt = ntail_p.reshape(batch, tpp, HALO, d)[:, tpp - 1, HALO - POOL_HIST:, :][None]
    new_conv_sample = uts.reshape(CONV_HIST, nb, d).transpose(1, 0, 2)[None]
    new_pool_sample = pool_tail_s.reshape(POOL_HIST, nb, d).transpose(1, 0, 2)[None]
    return (y_prompt, y_sample, new_conv_prompt, new_pool_prompt, new_conv_sample, new_pool_sample)
```

```python
import functools
from typing import NamedTuple

import jax
import jax.numpy as jnp
from jax import lax
from jax.experimental import pallas as pl
from jax.experimental.pallas import tpu as pltpu

D_MODEL = 2048
D_FF = 5632
N_META = 16
SEQ = 2048
N_SAMPLE_SEQ = 128
SAMPLE_T = 8
CONV_HIST = 2
POOL_WINDOWS = (2, 4, 8, 16)
POOL_GROUP = D_MODEL // len(POOL_WINDOWS)
POOL_HIST = 15
EPS = 1e-6

N_SAMPLE_ROWS = N_SAMPLE_SEQ * SAMPLE_T
S_ROWS = N_SAMPLE_ROWS + N_META
META_ROW0 = N_SAMPLE_ROWS
TAIL = 8
HALO = 16
MXU_COLS = 256
FRONT = 8
W_CHUNK_COLS = 512

V7X_SCOPED_VMEM_BYTES = 60000 * 1024

BF16 = jnp.bfloat16
F32 = jnp.float32


def _rms(x, g):
    ms = jnp.mean(x * x, axis=-1, keepdims=True)
    return (x * lax.rsqrt(ms + EPS)) * g


def _dot(a, b):
    return jnp.dot(a, b, preferred_element_type=F32)


def _params(n_axes):
    return pltpu.CompilerParams(
        dimension_semantics=("arbitrary",) * n_axes,
        vmem_limit_bytes=V7X_SCOPED_VMEM_BYTES,
    )


def _resident(shape, index_map):
    return pl.BlockSpec(shape, index_map, pipeline_mode=pl.Buffered(1))


class _CastJob(NamedTuple):
    src: jax.Array
    in_spec: pl.BlockSpec
    out_spec: pl.BlockSpec
    out_shape: jax.ShapeDtypeStruct


def _cast_job(w_stack, layer, n_steps, axis, step_of):
    _, r, c = w_stack.shape
    if axis == 0:
        blk = (r // n_steps, c)
        in_idx = lambda *ids: (layer, step_of(*ids), 0)
        out_idx = lambda *ids: (step_of(*ids), 0)
    else:
        blk = (r, c // n_steps)
        in_idx = lambda *ids: (layer, 0, step_of(*ids))
        out_idx = lambda *ids: (0, step_of(*ids))
    assert blk[0] * (n_steps if axis == 0 else 1) == r and blk[1] * (n_steps if axis == 1 else 1) == c
    return _CastJob(w_stack, pl.BlockSpec((None,) + blk, in_idx), pl.BlockSpec(blk, out_idx),
                    jax.ShapeDtypeStruct((r, c), BF16))


def _run_casts(src_refs, dst_refs):
    for src, dst in zip(src_refs, dst_refs):
        dst[...] = src[...].astype(BF16)


def _conv3(w, u, ubuf, off):
    t = u.shape[0]
    return (w[2:3] * u + w[1:2] * ubuf[off - 1:off - 1 + t, :] + w[0:1] * ubuf[off - 2:off - 2 + t, :])


def _mix_p_kernel(tiles_per_seq, n_cast, *refs):
    h_ref, g_ref, wb_ref, wc_ref, wv_ref, wdw_ref, umeta_ref = refs[:7]
    cast_src, refs = refs[7:7 + n_cast], refs[7 + n_cast:]
    z_ref, utail_ref = refs[:2]
    cast_dst, (n_sc, ubuf, carry) = refs[2:2 + n_cast], refs[2 + n_cast:]
    i = pl.program_id(0)
    j = pl.program_id(1)
    bm = h_ref.shape[0]
    _run_casts(cast_src, cast_dst)

    @pl.when(j == 0)
    def _():
        n_sc[...] = _rms(h_ref[...], g_ref[...]).astype(BF16)

    @pl.when(i % tiles_per_seq == 0)
    def _():
        ubuf[0:TAIL, :] = umeta_ref[...]

    @pl.when(i % tiles_per_seq != 0)
    def _():
        ubuf[0:TAIL, :] = carry[j]

    n = n_sc[...]
    for c0 in range(0, z_ref.shape[1], MXU_COLS):
        cols = slice(c0, c0 + MXU_COLS)
        b = _dot(n, wb_ref[:, cols])
        u = _dot(n, wc_ref[:, cols]) * _dot(n, wv_ref[:, cols])
        ubuf[TAIL:TAIL + bm, cols] = u
        z_ref[:, cols] = (b * _conv3(wdw_ref[:, cols], u, ubuf.at[:, cols], TAIL)).astype(BF16)
    tail = ubuf[bm:bm + TAIL, :]
    carry[j] = tail
    utail_ref[...] = tail


def _mix_s_kernel(xs_ref, meta_ref, g_ref, wb_ref, wc_ref, wv_ref, wdw_ref, hc_ref,
                  z_ref, uts_ref, umeta_ref, wb_o, wc_o, wv_o, h0_ref, n_sc, ubuf, mbuf):
    j = pl.program_id(0)
    ns = N_SAMPLE_ROWS
    nb = N_SAMPLE_SEQ
    nm = S_ROWS - ns

    @pl.when(j == 0)
    def _():
        for t in range(SAMPLE_T):
            h0_ref[t * nb:(t + 1) * nb, :] = xs_ref[:, t, :]
        h0_ref[ns:, :] = meta_ref[...]
        n_sc[...] = _rms(h0_ref[...], g_ref[...]).astype(BF16)

    wb_o[...] = wb_ref[...].astype(BF16)
    wc_o[...] = wc_ref[...].astype(BF16)
    wv_o[...] = wv_ref[...].astype(BF16)
    n = n_sc[...]
    nh = CONV_HIST * nb
    ubuf[0:nh, :] = hc_ref[...]
    mbuf[0:TAIL, :] = jnp.zeros((TAIL, mbuf.shape[1]), F32)
    for c0 in range(0, z_ref.shape[1], MXU_COLS):
        cols = slice(c0, c0 + MXU_COLS)
        b = _dot(n, wb_o[:, cols])
        u = _dot(n, wc_o[:, cols]) * _dot(n, wv_o[:, cols])
        w = wdw_ref[:, cols]

        us = u[0:ns, :]
        ubuf[nh:nh + ns, cols] = us
        conv_s = (w[2:3] * us + w[1:2] * ubuf[nb:nb + ns, cols] + w[0:1] * ubuf[0:ns, cols])
        z_ref[0:ns, cols] = (b[0:ns, :] * conv_s).astype(BF16)

        um = u[ns:, :]
        mbuf[TAIL:TAIL + nm, cols] = um
        z_ref[ns:, cols] = (b[ns:, :] * _conv3(w, um, mbuf.at[:, cols], TAIL)).astype(BF16)
    uts_ref[...] = ubuf[ns:ns + nh, :]
    umeta_ref[...] = mbuf[N_META:N_META + TAIL, :]


def _mix_conv_p(h, g, wb, wc, wv, w_dw, umeta, bm, bn, casts):
    rows = h.shape[0]
    n_i, n_j = rows // bm, D_MODEL // bn
    wspec = pl.BlockSpec((D_MODEL, bn), lambda i, j: (0, j))
    return pl.pallas_call(
        functools.partial(_mix_p_kernel, SEQ // bm, len(casts)),
        grid=(n_i, n_j),
        in_specs=[
            pl.BlockSpec((bm, D_MODEL), lambda i, j: (i, 0)),
            pl.BlockSpec((1, D_MODEL), lambda i, j: (0, 0)),
            wspec, wspec, wspec,
            pl.BlockSpec((3, bn), lambda i, j: (0, j)),
            pl.BlockSpec((TAIL, bn), lambda i, j: (0, j)),
        ] + [c.in_spec for c in casts],
        out_specs=[
            pl.BlockSpec((bm, bn), lambda i, j: (i, j)),
            pl.BlockSpec((TAIL, bn), lambda i, j: (i, j)),
        ] + [c.out_spec for c in casts],
        out_shape=[
            jax.ShapeDtypeStruct((rows, D_MODEL), BF16),
            jax.ShapeDtypeStruct((n_i * TAIL, D_MODEL), F32),
        ] + [c.out_shape for c in casts],
        scratch_shapes=[
            pltpu.VMEM((bm, D_MODEL), BF16),
            pltpu.VMEM((TAIL + bm, bn), F32),
            pltpu.VMEM((n_j, TAIL, bn), F32),
        ],
        compiler_params=_params(2),
        name="mix_conv_p",
    )(h, g, wb, wc, wv, w_dw, umeta, *[c.src for c in casts])


def _mix_conv_s(x_sample, meta, g, w_in, w_dw, hc, bn):
    n_j = D_MODEL // bn
    nb = N_SAMPLE_SEQ
    nm = S_ROWS - N_SAMPLE_ROWS
    wspec = lambda part: pl.BlockSpec((None, D_MODEL, bn), lambda j: (0, 0, part * n_j + j))
    wout = pl.BlockSpec((D_MODEL, bn), lambda j: (0, j))
    wshape = jax.ShapeDtypeStruct((D_MODEL, D_MODEL), BF16)
    return pl.pallas_call(
        _mix_s_kernel,
        grid=(n_j,),
        in_specs=[
            _resident((nb, SAMPLE_T, D_MODEL), lambda j: (0, 0, 0)),
            _resident((N_META, D_MODEL), lambda j: (0, 0)),
            pl.BlockSpec((1, D_MODEL), lambda j: (0, 0)),
            wspec(0), wspec(1), wspec(2),
            pl.BlockSpec((3, bn), lambda j: (0, j)),
            pl.BlockSpec((CONV_HIST * nb, bn), lambda j: (0, j)),
        ],
        out_specs=[
            pl.BlockSpec((S_ROWS, bn), lambda j: (0, j)),
            pl.BlockSpec((CONV_HIST * nb, bn), lambda j: (0, j)),
            pl.BlockSpec((TAIL, bn), lambda j: (0, j)),
            wout, wout, wout,
            pl.BlockSpec((S_ROWS, D_MODEL), lambda j: (0, 0)),
        ],
        out_shape=[
            jax.ShapeDtypeStruct((S_ROWS, D_MODEL), BF16),
            jax.ShapeDtypeStruct((CONV_HIST * nb, D_MODEL), F32),
            jax.ShapeDtypeStruct((TAIL, D_MODEL), F32),
            wshape, wshape, wshape,
            jax.ShapeDtypeStruct((S_ROWS, D_MODEL), F32),
        ],
        scratch_shapes=[
            pltpu.VMEM((S_ROWS, D_MODEL), BF16),
            pltpu.VMEM((CONV_HIST * nb + N_SAMPLE_ROWS, bn), F32),
            pltpu.VMEM((TAIL + nm, bn), F32),
        ],
        compiler_params=_params(1),
        name="mix_conv_s",
    )(x_sample, meta, g, w_in, w_in, w_in, w_dw, hc)


def _proj_res_body(norm, k, n_k, x_ref, w_ref, h_ref, g_ref, o_ref, m_ref):
    @pl.when(k == 0)
    def _():
        o_ref[...] = h_ref[...] + _dot(x_ref[...], w_ref[...])

    @pl.when(k != 0)
    def _():
        o_ref[...] += _dot(x_ref[...], w_ref[...])

    if norm is not None:
        @pl.when(k == n_k - 1)
        def _():
            y = _rms(o_ref[...], g_ref[...])
            if norm == "final":
                o_ref[...] = y
            else:
                m_ref[...] = y.astype(BF16)


def _weight_chunk_copy(w_hbm, w_vmem, sem, c):
    cols = pl.ds(c * W_CHUNK_COLS, W_CHUNK_COLS)
    return pltpu.make_async_copy(w_hbm.at[:, cols], w_vmem.at[:, cols], sem.at[c])


def _proj_res_kept_w_kernel(norm, x_ref, w_hbm, h_ref, g_ref, o_ref, *rest):
    m_ref = rest[0] if norm == "emit" else None
    w_vmem, sem = rest[-2:]
    n_chunks = w_vmem.shape[1] // W_CHUNK_COLS

    def finish(acc):
        o_ref[...] = _rms(acc, g_ref[...]) if norm == "final" else acc
        if norm == "emit":
            m_ref[...] = _rms(acc, g_ref[...]).astype(BF16)

    @pl.when(pl.program_id(0) == 0)
    def _():
        for c in range(n_chunks):
            _weight_chunk_copy(w_hbm, w_vmem, sem, c).start()
        for c in range(n_chunks):
            _weight_chunk_copy(w_hbm, w_vmem, sem, c).wait()
            cols = slice(c * W_CHUNK_COLS, (c + 1) * W_CHUNK_COLS)
            o_ref[:, cols] = h_ref[:, cols] + _dot(x_ref[...], w_vmem[:, cols])
        if norm is not None:
            finish(o_ref[...])

    @pl.when(pl.program_id(0) != 0)
    def _():
        finish(h_ref[...] + _dot(x_ref[...], w_vmem[...]))


def _proj_res_p(x, w, h, g, bm, norm, name):
    rows, kdim = x.shape
    row_spec = pl.BlockSpec((bm, D_MODEL), lambda i: (i, 0))
    out_specs, out_shape = [row_spec], [jax.ShapeDtypeStruct((rows, D_MODEL), F32)]
    if norm == "emit":
        out_specs.append(row_spec)
        out_shape.append(jax.ShapeDtypeStruct((rows, D_MODEL), BF16))
    return pl.pallas_call(
        functools.partial(_proj_res_kept_w_kernel, norm),
        grid=(rows // bm,),
        in_specs=[
            pl.BlockSpec((bm, kdim), lambda i: (i, 0)),
            pl.BlockSpec(memory_space=pl.ANY),
            row_spec,
            pl.BlockSpec((1, D_MODEL), lambda i: (0, 0)),
        ],
        out_specs=out_specs,
        out_shape=out_shape,
        scratch_shapes=[
            pltpu.VMEM((kdim, D_MODEL), BF16),
            pltpu.SemaphoreType.DMA((D_MODEL // W_CHUNK_COLS,)),
        ],
        compiler_params=_params(1),
        name=name,
    )(x, w, h, g)


def _swiglu_chunks(m, wg_ref, wu_ref, a_ref):
    for c0 in range(0, a_ref.shape[1], MXU_COLS):
        cols = slice(c0, c0 + MXU_COLS)
        gate = _dot(m, wg_ref[:, cols])
        up = _dot(m, wu_ref[:, cols])
        a_ref[:, cols] = (gate * (1.0 / (1.0 + jnp.exp(-gate))) * up).astype(BF16)


def _gate_up_p_kernel(n_cast, m_ref, wg_ref, wu_ref, *refs):
    _run_casts(refs[:n_cast], refs[n_cast + 1:])
    _swiglu_chunks(m_ref[...], wg_ref, wu_ref, refs[n_cast])


def _ffn_s_kernel(sample_out, n_pre, n_j, *refs):
    h_ref, refs = refs[0], refs[1:]
    if n_pre:
        (x_ref, wpre_ref), refs = refs[:2], refs[2:]
    g_ref, gfin_ref, wg_ref, wu_ref, wd_ref, o_ref, m_sc, a_sc = refs[:8]
    rest = list(refs[8:])
    hsrc = rest.pop(0) if n_pre else h_ref
    acc = rest.pop(0) if sample_out else o_ref
    s = pl.program_id(0)
    j = s - n_pre

    if n_pre:
        @pl.when(s < n_pre)
        def _():
            _proj_res_body(None, s, n_pre, x_ref, wpre_ref, h_ref, None, hsrc, None)

    @pl.when(s >= n_pre)
    def _():
        @pl.when(j == 0)
        def _():
            m_sc[...] = _rms(hsrc[...], g_ref[...]).astype(BF16)

        _swiglu_chunks(m_sc[...], wg_ref, wu_ref, a_sc)
        _proj_res_body(None, j, n_j, a_sc, wd_ref, hsrc, None, acc, None)

        if sample_out:
            @pl.when(j == n_j - 1)
            def _():
                y = _rms(acc[0:N_SAMPLE_ROWS, :], gfin_ref[...])
                for t in range(SAMPLE_T):
                    o_ref[:, t, :] = y[t * N_SAMPLE_SEQ:(t + 1) * N_SAMPLE_SEQ, :]


def _ffn_s(h, g, g_fin, w_gu, w_dn, bn, sample_out, name, pre=None):
    rows = h.shape[0]
    n_j = D_FF // bn
    n_pre = 0 if pre is None else pre[0].shape[1] // pre[2]
    ffn_step = lambda s: jnp.maximum(s - n_pre, 0)
    row_spec = pl.BlockSpec((1, D_MODEL), lambda s: (0, 0))
    o_shape = (N_SAMPLE_SEQ, SAMPLE_T, D_MODEL) if sample_out else (rows, D_MODEL)
    o_index = (0,) * len(o_shape)
    in_specs, args = [_resident((rows, D_MODEL), lambda s: (0, 0))], [h]
    scratch = [pltpu.VMEM((rows, D_MODEL), BF16), pltpu.VMEM((rows, bn), BF16)]
    if n_pre:
        x, w_pre, bk = pre
        pre_step = lambda s: jnp.minimum(s, n_pre - 1)
        in_specs += [pl.BlockSpec((rows, bk), lambda s: (0, pre_step(s))),
                     pl.BlockSpec((bk, D_MODEL), lambda s: (pre_step(s), 0))]
        args += [x, w_pre]
        scratch.append(pltpu.VMEM((rows, D_MODEL), F32))
    if sample_out:
        scratch.append(pltpu.VMEM((rows, D_MODEL), F32))
    in_specs += [
        row_spec, row_spec,
        pl.BlockSpec((D_MODEL, bn), lambda s: (0, ffn_step(s))),
        pl.BlockSpec((D_MODEL, bn), lambda s: (0, n_j + ffn_step(s))),
        pl.BlockSpec((bn, D_MODEL), lambda s: (ffn_step(s), 0)),
    ]
    return pl.pallas_call(
        functools.partial(_ffn_s_kernel, sample_out, n_pre, n_j),
        grid=(n_pre + n_j,),
        in_specs=in_specs,
        out_specs=pl.BlockSpec(o_shape, lambda s: o_index),
        out_shape=jax.ShapeDtypeStruct(o_shape, F32),
        scratch_shapes=scratch,
        compiler_params=_params(1),
        name=name,
    )(*args, g, g_fin, w_gu, w_gu, w_dn)


def _gate_up_p(m, w_gu, bm, bn, casts, name):
    rows = m.shape[0]
    n_i, n_j = rows // bm, D_FF // bn
    return pl.pallas_call(
        functools.partial(_gate_up_p_kernel, len(casts)),
        grid=(n_i, n_j),
        in_specs=[
            pl.BlockSpec((bm, D_MODEL), lambda i, j: (i, 0)),
            pl.BlockSpec((D_MODEL, bn), lambda i, j: (0, j)),
            pl.BlockSpec((D_MODEL, bn), lambda i, j: (0, n_j + j)),
        ] + [c.in_spec for c in casts],
        out_specs=[pl.BlockSpec((bm, bn), lambda i, j: (i, j))] + [c.out_spec for c in casts],
        out_shape=[jax.ShapeDtypeStruct((rows, D_FF), BF16)] + [c.out_shape for c in casts],
        compiler_params=_params(2),
        name=name,
    )(m, w_gu, w_gu, *[c.src for c in casts])


def _down_pool_kernel(tiles_per_seq, n_t, a_ref, w_hbm, h_ref, meta_ref, g_ref, wp_ref, sc_ref, gnext_ref,
                      o_ref, ntail_ref, m_ref, h2_even, h2_odd, ncarry, nbuf, pbuf, qbuf, w_vmem, sem):
    s = pl.program_id(0)
    bm = h_ref.shape[0]
    g = g_ref[...]
    rows = HALO + bm
    r0 = FRONT + HALO
    split = (3 * D_MODEL) // 4

    @pl.when((s - 1) % tiles_per_seq == 0)
    def _():
        nbuf[FRONT:r0, :] = _rms(meta_ref[...], g)

    @pl.when(jnp.logical_and(s >= 1, (s - 1) % tiles_per_seq != 0))
    def _():
        nbuf[FRONT:r0, :] = ncarry[...]

    def project(h2_new, c0, c1):
        h2_new[:, c0:c1] = h_ref[:, c0:c1] + _dot(a_ref[...], w_vmem[:, c0:c1])

    def step(h2_new, h2_old):
        if h2_old is None:
            n_chunks = D_MODEL // W_CHUNK_COLS
            for c in range(n_chunks):
                _weight_chunk_copy(w_hbm, w_vmem, sem, c).start()
            for c in range(n_chunks):
                _weight_chunk_copy(w_hbm, w_vmem, sem, c).wait()
                project(h2_new, c * W_CHUNK_COLS, (c + 1) * W_CHUNK_COLS)
            return
        if h2_new is not None:
            project(h2_new, 0, split)
        for buf in (nbuf, pbuf, qbuf):
            buf[0:FRONT, :] = jnp.zeros((FRONT, buf.shape[1]), F32)
        n = _rms(h2_old[...], g)
        nbuf[r0:r0 + bm, :] = n
        tail = n[bm - HALO:, :]
        ncarry[...] = tail
        ntail_ref[...] = tail
        for gi, win in enumerate(POOL_WINDOWS):
            cols = slice(gi * POOL_GROUP, (gi + 1) * POOL_GROUP)
            src, src_cols, shift, level = nbuf, cols, 1, 0
            while shift < win:
                dst = (pbuf, qbuf)[level % 2]
                dst[FRONT:FRONT + rows, :] = (src[FRONT:FRONT + rows, src_cols]
                                              + src[FRONT - shift:FRONT - shift + rows, src_cols])
                src, src_cols, shift, level = dst, slice(None), 2 * shift, level + 1
            p = src[r0:r0 + bm, src_cols] * (1.0 / win) - n[:, cols]
            y = _dot(p.astype(BF16), wp_ref[gi].astype(BF16)) * sc_ref[:, cols]
            o_ref[:, cols] = h2_old[:, cols] + y
        if h2_new is not None:
            project(h2_new, split, D_MODEL)
        m_ref[...] = _rms(o_ref[...], gnext_ref[...]).astype(BF16)

    @pl.when(s == 0)
    def _():
        step(h2_even, None)

    @pl.when(jnp.logical_and(s % 2 == 0, jnp.logical_and(s > 0, s < n_t)))
    def _():
        step(h2_even, h2_odd)

    @pl.when(jnp.logical_and(s % 2 == 1, s < n_t))
    def _():
        step(h2_odd, h2_even)

    @pl.when(s == n_t)
    def _():
        step(None, (h2_even, h2_odd)[(n_t - 1) % 2])


def _down_pool_p(a, w, h, h_s, g, wp, scale, g_next, bm):
    rows, kdim = a.shape
    n_t = rows // bm
    ng = len(POOL_WINDOWS)
    cur = lambda s: (jnp.minimum(s, n_t - 1), 0)
    prev = lambda s: (jnp.maximum(s - 1, 0), 0)
    vec = pl.BlockSpec((1, D_MODEL), lambda s: (0, 0))
    return pl.pallas_call(
        functools.partial(_down_pool_kernel, SEQ // bm, n_t),
        grid=(n_t + 1,),
        in_specs=[
            pl.BlockSpec((bm, kdim), cur),
            pl.BlockSpec(memory_space=pl.ANY),
            pl.BlockSpec((bm, D_MODEL), cur),
            _resident((HALO, D_MODEL), lambda s: (META_ROW0 // HALO, 0)),
            vec,
            _resident((None, ng, POOL_GROUP, POOL_GROUP), lambda s: (0, 0, 0, 0)),
            vec, vec,
        ],
        out_specs=[
            pl.BlockSpec((bm, D_MODEL), prev),
            pl.BlockSpec((HALO, D_MODEL), prev),
            pl.BlockSpec((bm, D_MODEL), prev),
        ],
        out_shape=[
            jax.ShapeDtypeStruct((rows, D_MODEL), F32),
            jax.ShapeDtypeStruct((n_t * HALO, D_MODEL), F32),
            jax.ShapeDtypeStruct((rows, D_MODEL), BF16),
        ],
        scratch_shapes=[
            pltpu.VMEM((bm, D_MODEL), F32),
            pltpu.VMEM((bm, D_MODEL), F32),
            pltpu.VMEM((HALO, D_MODEL), F32),
            pltpu.VMEM((FRONT + HALO + bm, D_MODEL), F32),
            pltpu.VMEM((FRONT + HALO + bm, POOL_GROUP), F32),
            pltpu.VMEM((FRONT + HALO + bm, POOL_GROUP), F32),
            pltpu.VMEM((kdim, D_MODEL), BF16),
            pltpu.SemaphoreType.DMA((D_MODEL // W_CHUNK_COLS,)),
        ],
        compiler_params=_params(1),
        name="down_pool_p",
    )(a, w, h, h_s, g, wp, scale, g_next)


def _pool_s_kernel(hfull_ref, hcol_ref, hp_ref, g_ref, wp_ref, sc_ref, o_ref, tail_ref, inv_sc, nbuf, sum_sc):
    j = pl.program_id(0)
    ns = N_SAMPLE_ROWS
    nb = N_SAMPLE_SEQ
    nh = POOL_HIST * nb

    @pl.when(j == 0)
    def _():
        xf = hfull_ref[...]
        inv_sc[...] = lax.rsqrt(jnp.mean(xf * xf, axis=-1, keepdims=True) + EPS)

    x = hcol_ref[...]
    n = ((x * inv_sc[...]) * g_ref[...])[0:ns, :]
    nbuf[0:nh, :] = hp_ref[...]
    nbuf[nh:nh + ns, :] = n
    tail_ref[...] = nbuf[ns:ns + nh, :]

    for gi, win in enumerate(POOL_WINDOWS):
        @pl.when(j == gi)
        def _(win=win):
            acc = n
            for k in range(1, win):
                acc = acc + nbuf[nh - k * nb:nh - k * nb + ns, :]
            sum_sc[...] = acc * (1.0 / win)

    p = sum_sc[...] - n
    y = _dot(p.astype(BF16), wp_ref[...].astype(BF16)) * sc_ref[...]
    o_ref[...] = x[0:ns, :] + y


def _pool_s(h_s, state_pool, g, wp, scale):
    ns = N_SAMPLE_ROWS
    nb = N_SAMPLE_SEQ
    nh = POOL_HIST * nb
    pg = POOL_GROUP
    n_g = len(POOL_WINDOWS)
    state_spec = pl.BlockSpec((nh, pg), lambda j: (0, j))
    return pl.pallas_call(
        _pool_s_kernel,
        grid=(n_g,),
        in_specs=[
            _resident((S_ROWS, D_MODEL), lambda j: (0, 0)),
            pl.BlockSpec((S_ROWS, pg), lambda j: (0, j)),
            state_spec,
            pl.BlockSpec((1, pg), lambda j: (0, j)),
            pl.BlockSpec((None, None, pg, pg), lambda j: (0, j, 0, 0)),
            pl.BlockSpec((1, pg), lambda j: (0, j)),
        ],
        out_specs=[pl.BlockSpec((ns, pg), lambda j: (0, j)), state_spec],
        out_shape=[
            jax.ShapeDtypeStruct((ns, D_MODEL), F32),
            jax.ShapeDtypeStruct((nh, D_MODEL), F32),
        ],
        scratch_shapes=[
            pltpu.VMEM((S_ROWS, 1), F32),
            pltpu.VMEM((nh + ns, pg), F32),
            pltpu.VMEM((ns, pg), F32),
        ],
        compiler_params=_params(1),
        name="pool_s",
    )(h_s, h_s, state_pool, g, wp, scale)


BM_P = 1024
BN_MIX = 512
BM_CONV_OUT = 512
BM_GATE_UP = 2048
BN_P = 512
BM_DOWN = 512
BM_POOL = 256
BN_S = 256
BN_S_FF = 512
BK_S = 512

assert SEQ % BM_P == 0 and SEQ % BM_POOL == 0 and BM_POOL % HALO == 0 and META_ROW0 % HALO == 0
assert D_MODEL % BN_MIX == 0 and D_MODEL % BN_S == 0 and BN_MIX % MXU_COLS == 0 and BN_S % MXU_COLS == 0
assert D_FF % BN_P == 0 and D_FF % BN_S_FF == 0 and D_FF % BK_S == 0 and D_MODEL % BK_S == 0


def kernel(x_prompt, x_sample, state_conv, state_pool, meta_tokens, norm_mix, norm_ffn, norm_final,
           conv_w_in, conv_w_dw, conv_w_out, pool_w, pool_scale, ffn_w_gate_up, ffn_w_down):
    d = D_MODEL
    nb, nt = N_SAMPLE_SEQ, SAMPLE_T
    batch = x_prompt.shape[0]
    assert x_prompt.shape == (batch, SEQ, d) and x_sample.shape == (nb, nt, d)
    assert state_conv.shape == (1, nb, CONV_HIST, d) and state_pool.shape == (1, nb, POOL_HIST, d)
    assert meta_tokens.shape == (N_META, d) and ffn_w_gate_up.shape == (2, d, 2 * D_FF)
    assert (batch * SEQ) % BM_GATE_UP == 0 and (batch * SEQ) % BM_DOWN == 0 and (batch * SEQ) % BM_CONV_OUT == 0

    h_p = x_prompt.reshape(batch * SEQ, d)
    hc = state_conv[0].transpose(1, 0, 2).reshape(CONV_HIST * nb, d)
    hp = state_pool[0].transpose(1, 0, 2).reshape(POOL_HIST * nb, d)
    row = lambda v: v.reshape(1, d)
    g_mix, g_ffn, g_fin = norm_mix, norm_ffn, row(norm_final)

    rows_p = batch * SEQ
    n_j_mix, n_j_ff = d // BN_MIX, D_FF // BN_P
    n_mix, n_ff = (rows_p // BM_P) * n_j_mix, (rows_p // BM_GATE_UP) * n_j_ff
    step_mix = lambda i, j: i * n_j_mix + j
    step_ff = lambda i, j: i * n_j_ff + j
    casts_mix = [_cast_job(conv_w_out, 0, n_mix, 0, step_mix), _cast_job(ffn_w_gate_up, 0, n_mix, 0, step_mix)]
    casts_ff0 = [_cast_job(ffn_w_down, 0, n_ff, 0, step_ff), _cast_job(ffn_w_gate_up, 1, n_ff, 1, step_ff)]
    casts_ff1 = [_cast_job(ffn_w_down, 1, n_ff, 0, step_ff)]

    z_s, uts, umeta, wb, wc, wv, h_s = _mix_conv_s(
        x_sample, meta_tokens, row(g_mix[0]), conv_w_in, conv_w_dw[0], hc, BN_S)
    z_p, utail_p, w_out, w_gu0 = _mix_conv_p(
        h_p, row(g_mix[0]), wb, wc, wv, conv_w_dw[0], umeta, BM_P, BN_MIX, casts_mix)
    h_p, m_p = _proj_res_p(z_p, w_out, h_p, row(g_ffn[0]), BM_CONV_OUT, "emit", "conv_out_p")
    a_p, w_dn0, w_gu1 = _gate_up_p(m_p, w_gu0, BM_GATE_UP, BN_P, casts_ff0, "gate_up0_p")
    h_s = _ffn_s(h_s, row(g_ffn[0]), g_fin, w_gu0, w_dn0, BN_S_FF, False, "conv_out_ffn0_s", pre=(z_s, w_out, BK_S))
    h_p1, ntail_p, m_p = _down_pool_p(a_p, w_dn0, h_p, h_s, row(g_mix[1]), pool_w, row(pool_scale[0]),
                                      row(g_ffn[1]), BM_POOL)
    h_s1, pool_tail_s = _pool_s(h_s, hp, row(g_mix[1]), pool_w, row(pool_scale[0]))
    a_p, w_dn1 = _gate_up_p(m_p, w_gu1, BM_GATE_UP, BN_P, casts_ff1, "gate_up1_p")
    y_s = _ffn_s(h_s1, row(g_ffn[1]), g_fin, w_gu1, w_dn1, BN_S_FF, True, "ffn1_s")
    (y_p,) = _proj_res_p(a_p, w_dn1, h_p1, g_fin, BM_DOWN, "final", "down1_p")

    y_prompt = y_p.reshape(batch, SEQ, d)
    y_sample = y_s
    tps = SEQ // BM_P
    new_conv_prompt = utail_p.reshape(batch, tps, TAIL, d)[:, tps - 1, TAIL - CONV_HIST:, :][None]
    tpp = SEQ // BM_POOL
    new_pool_prompt = ntail_p.reshape(batch, tpp, HALO, d)[:, tpp - 1, HALO - POOL_HIST:, :][None]
    new_conv_sample = uts.reshape(CONV_HIST, nb, d).transpose(1, 0, 2)[None]
    new_pool_sample = pool_tail_s.reshape(POOL_HIST, nb, d).transpose(1, 0, 2)[None]
    return (y_prompt, y_sample, new_conv_prompt, new_pool_prompt, new_conv_sample, new_pool_sample)
```

```python
import functools
from typing import NamedTuple

import jax
import jax.numpy as jnp
from jax import lax
from jax.experimental import pallas as pl
from jax.experimental.pallas import tpu as pltpu

D_MODEL = 2048
D_FF = 5632
N_META = 16
SEQ = 2048
N_SAMPLE_SEQ = 128
SAMPLE_T = 8
CONV_HIST = 2
POOL_WINDOWS = (2, 4, 8, 16)
POOL_GROUP = D_MODEL // len(POOL_WINDOWS)
POOL_HIST = 15
EPS = 1e-6

N_SAMPLE_ROWS = N_SAMPLE_SEQ * SAMPLE_T
S_ROWS = N_SAMPLE_ROWS + N_META
META_ROW0 = N_SAMPLE_ROWS
TAIL = 8
HALO = 16
MXU_COLS = 256
FRONT = 8
SWIGLU_ROWS = 1024
W_CHUNK_COLS = 512

V7X_SCOPED_VMEM_BYTES = 60000 * 1024

BF16 = jnp.bfloat16
F32 = jnp.float32


def _rms(x, g):
    ms = jnp.mean(x * x, axis=-1, keepdims=True)
    return (x * lax.rsqrt(ms + EPS)) * g


def _dot(a, b):
    return jnp.dot(a, b, preferred_element_type=F32)


def _params(n_axes):
    return pltpu.CompilerParams(
        dimension_semantics=("arbitrary",) * n_axes,
        vmem_limit_bytes=V7X_SCOPED_VMEM_BYTES,
    )


def _resident(shape, index_map):
    return pl.BlockSpec(shape, index_map, pipeline_mode=pl.Buffered(1))


class _CastJob(NamedTuple):
    src: jax.Array
    in_spec: pl.BlockSpec
    out_spec: pl.BlockSpec
    out_shape: jax.ShapeDtypeStruct


def _cast_job(w_stack, layer, n_steps, axis, step_of):
    _, r, c = w_stack.shape
    if axis == 0:
        blk = (r // n_steps, c)
        in_idx = lambda *ids: (layer, step_of(*ids), 0)
        out_idx = lambda *ids: (step_of(*ids), 0)
    else:
        blk = (r, c // n_steps)
        in_idx = lambda *ids: (layer, 0, step_of(*ids))
        out_idx = lambda *ids: (0, step_of(*ids))
    assert blk[0] * (n_steps if axis == 0 else 1) == r and blk[1] * (n_steps if axis == 1 else 1) == c
    return _CastJob(w_stack, pl.BlockSpec((None,) + blk, in_idx), pl.BlockSpec(blk, out_idx),
                    jax.ShapeDtypeStruct((r, c), BF16))


def _run_casts(src_refs, dst_refs):
    for src, dst in zip(src_refs, dst_refs):
        dst[...] = src[...].astype(BF16)


def _conv3(w, u, ubuf, off):
    t = u.shape[0]
    return (w[2:3] * u + w[1:2] * ubuf[off - 1:off - 1 + t, :] + w[0:1] * ubuf[off - 2:off - 2 + t, :])


def _mix_p_kernel(tiles_per_seq, n_cast, *refs):
    h_ref, g_ref, wb_ref, wc_ref, wv_ref, wdw_ref, umeta_ref = refs[:7]
    cast_src, refs = refs[7:7 + n_cast], refs[7 + n_cast:]
    z_ref, utail_ref = refs[:2]
    cast_dst, (n_sc, ubuf, carry) = refs[2:2 + n_cast], refs[2 + n_cast:]
    i = pl.program_id(0)
    j = pl.program_id(1)
    bm = h_ref.shape[0]
    _run_casts(cast_src, cast_dst)

    @pl.when(j == 0)
    def _():
        n_sc[...] = _rms(h_ref[...], g_ref[...]).astype(BF16)

    @pl.when(i % tiles_per_seq == 0)
    def _():
        ubuf[0:TAIL, :] = umeta_ref[...]

    @pl.when(i % tiles_per_seq != 0)
    def _():
        ubuf[0:TAIL, :] = carry[j]

    n = n_sc[...]
    for c0 in range(0, z_ref.shape[1], MXU_COLS):
        cols = slice(c0, c0 + MXU_COLS)
        b = _dot(n, wb_ref[:, cols])
        u = _dot(n, wc_ref[:, cols]) * _dot(n, wv_ref[:, cols])
        ubuf[TAIL:TAIL + bm, cols] = u
        z_ref[:, cols] = (b * _conv3(wdw_ref[:, cols], u, ubuf.at[:, cols], TAIL)).astype(BF16)
    tail = ubuf[bm:bm + TAIL, :]
    carry[j] = tail
    utail_ref[...] = tail


def _mix_s_kernel(xs_ref, meta_ref, g_ref, wb_ref, wc_ref, wv_ref, wdw_ref, hc_ref,
                  z_ref, uts_ref, umeta_ref, wb_o, wc_o, wv_o, h0_ref, n_sc, ubuf, mbuf):
    j = pl.program_id(0)
    ns = N_SAMPLE_ROWS
    nb = N_SAMPLE_SEQ
    nm = S_ROWS - ns

    @pl.when(j == 0)
    def _():
        for t in range(SAMPLE_T):
            h0_ref[t * nb:(t + 1) * nb, :] = xs_ref[:, t, :]
        h0_ref[ns:, :] = meta_ref[...]
        n_sc[...] = _rms(h0_ref[...], g_ref[...]).astype(BF16)

    wb_o[...] = wb_ref[...].astype(BF16)
    wc_o[...] = wc_ref[...].astype(BF16)
    wv_o[...] = wv_ref[...].astype(BF16)
    n = n_sc[...]
    nh = CONV_HIST * nb
    ubuf[0:nh, :] = hc_ref[...]
    mbuf[0:TAIL, :] = jnp.zeros((TAIL, mbuf.shape[1]), F32)
    for c0 in range(0, z_ref.shape[1], MXU_COLS):
        cols = slice(c0, c0 + MXU_COLS)
        b = _dot(n, wb_o[:, cols])
        u = _dot(n, wc_o[:, cols]) * _dot(n, wv_o[:, cols])
        w = wdw_ref[:, cols]

        us = u[0:ns, :]
        ubuf[nh:nh + ns, cols] = us
        conv_s = (w[2:3] * us + w[1:2] * ubuf[nb:nb + ns, cols] + w[0:1] * ubuf[0:ns, cols])
        z_ref[0:ns, cols] = (b[0:ns, :] * conv_s).astype(BF16)

        um = u[ns:, :]
        mbuf[TAIL:TAIL + nm, cols] = um
        z_ref[ns:, cols] = (b[ns:, :] * _conv3(w, um, mbuf.at[:, cols], TAIL)).astype(BF16)
    uts_ref[...] = ubuf[ns:ns + nh, :]
    umeta_ref[...] = mbuf[N_META:N_META + TAIL, :]


def _mix_conv_p(h, g, wb, wc, wv, w_dw, umeta, bm, bn, casts):
    rows = h.shape[0]
    n_i, n_j = rows // bm, D_MODEL // bn
    wspec = pl.BlockSpec((D_MODEL, bn), lambda i, j: (0, j))
    return pl.pallas_call(
        functools.partial(_mix_p_kernel, SEQ // bm, len(casts)),
        grid=(n_i, n_j),
        in_specs=[
            pl.BlockSpec((bm, D_MODEL), lambda i, j: (i, 0)),
            pl.BlockSpec((1, D_MODEL), lambda i, j: (0, 0)),
            wspec, wspec, wspec,
            pl.BlockSpec((3, bn), lambda i, j: (0, j)),
            pl.BlockSpec((TAIL, bn), lambda i, j: (0, j)),
        ] + [c.in_spec for c in casts],
        out_specs=[
            pl.BlockSpec((bm, bn), lambda i, j: (i, j)),
            pl.BlockSpec((TAIL, bn), lambda i, j: (i, j)),
        ] + [c.out_spec for c in casts],
        out_shape=[
            jax.ShapeDtypeStruct((rows, D_MODEL), BF16),
            jax.ShapeDtypeStruct((n_i * TAIL, D_MODEL), F32),
        ] + [c.out_shape for c in casts],
        scratch_shapes=[
            pltpu.VMEM((bm, D_MODEL), BF16),
            pltpu.VMEM((TAIL + bm, bn), F32),
            pltpu.VMEM((n_j, TAIL, bn), F32),
        ],
        compiler_params=_params(2),
        name="mix_conv_p",
    )(h, g, wb, wc, wv, w_dw, umeta, *[c.src for c in casts])


def _mix_conv_s(x_sample, meta, g, w_in, w_dw, hc, bn):
    n_j = D_MODEL // bn
    nb = N_SAMPLE_SEQ
    nm = S_ROWS - N_SAMPLE_ROWS
    wspec = lambda part: pl.BlockSpec((None, D_MODEL, bn), lambda j: (0, 0, part * n_j + j))
    wout = pl.BlockSpec((D_MODEL, bn), lambda j: (0, j))
    wshape = jax.ShapeDtypeStruct((D_MODEL, D_MODEL), BF16)
    return pl.pallas_call(
        _mix_s_kernel,
        grid=(n_j,),
        in_specs=[
            _resident((nb, SAMPLE_T, D_MODEL), lambda j: (0, 0, 0)),
            _resident((N_META, D_MODEL), lambda j: (0, 0)),
            pl.BlockSpec((1, D_MODEL), lambda j: (0, 0)),
            wspec(0), wspec(1), wspec(2),
            pl.BlockSpec((3, bn), lambda j: (0, j)),
            pl.BlockSpec((CONV_HIST * nb, bn), lambda j: (0, j)),
        ],
        out_specs=[
            pl.BlockSpec((S_ROWS, bn), lambda j: (0, j)),
            pl.BlockSpec((CONV_HIST * nb, bn), lambda j: (0, j)),
            pl.BlockSpec((TAIL, bn), lambda j: (0, j)),
            wout, wout, wout,
            pl.BlockSpec((S_ROWS, D_MODEL), lambda j: (0, 0)),
        ],
        out_shape=[
            jax.ShapeDtypeStruct((S_ROWS, D_MODEL), BF16),
            jax.ShapeDtypeStruct((CONV_HIST * nb, D_MODEL), F32),
            jax.ShapeDtypeStruct((TAIL, D_MODEL), F32),
            wshape, wshape, wshape,
            jax.ShapeDtypeStruct((S_ROWS, D_MODEL), F32),
        ],
        scratch_shapes=[
            pltpu.VMEM((S_ROWS, D_MODEL), BF16),
            pltpu.VMEM((CONV_HIST * nb + N_SAMPLE_ROWS, bn), F32),
            pltpu.VMEM((TAIL + nm, bn), F32),
        ],
        compiler_params=_params(1),
        name="mix_conv_s",
    )(x_sample, meta, g, w_in, w_in, w_in, w_dw, hc)


def _proj_res_body(norm, k, n_k, x_ref, w_ref, h_ref, g_ref, o_ref, m_ref):
    @pl.when(k == 0)
    def _():
        o_ref[...] = h_ref[...] + _dot(x_ref[...], w_ref[...])

    @pl.when(k != 0)
    def _():
        o_ref[...] += _dot(x_ref[...], w_ref[...])

    if norm is not None:
        @pl.when(k == n_k - 1)
        def _():
            y = _rms(o_ref[...], g_ref[...])
            if norm == "final":
                o_ref[...] = y
            else:
                m_ref[...] = y.astype(BF16)


def _weight_chunk_copy(w_hbm, w_vmem, sem, c):
    cols = pl.ds(c * W_CHUNK_COLS, W_CHUNK_COLS)
    return pltpu.make_async_copy(w_hbm.at[:, cols], w_vmem.at[:, cols], sem.at[c])


def _proj_res_kept_w_kernel(norm, x_ref, w_hbm, h_ref, g_ref, o_ref, *rest):
    m_ref = rest[0] if norm == "emit" else None
    w_vmem, sem = rest[-2:]
    n_chunks = w_vmem.shape[1] // W_CHUNK_COLS

    def finish(acc):
        o_ref[...] = _rms(acc, g_ref[...]) if norm == "final" else acc
        if norm == "emit":
            m_ref[...] = _rms(acc, g_ref[...]).astype(BF16)

    @pl.when(pl.program_id(0) == 0)
    def _():
        for c in range(n_chunks):
            _weight_chunk_copy(w_hbm, w_vmem, sem, c).start()
        for c in range(n_chunks):
            _weight_chunk_copy(w_hbm, w_vmem, sem, c).wait()
            cols = slice(c * W_CHUNK_COLS, (c + 1) * W_CHUNK_COLS)
            o_ref[:, cols] = h_ref[:, cols] + _dot(x_ref[...], w_vmem[:, cols])
        if norm is not None:
            finish(o_ref[...])

    @pl.when(pl.program_id(0) != 0)
    def _():
        finish(h_ref[...] + _dot(x_ref[...], w_vmem[...]))


def _proj_res_p(x, w, h, g, bm, norm, name):
    rows, kdim = x.shape
    row_spec = pl.BlockSpec((bm, D_MODEL), lambda i: (i, 0))
    out_specs, out_shape = [row_spec], [jax.ShapeDtypeStruct((rows, D_MODEL), F32)]
    if norm == "emit":
        out_specs.append(row_spec)
        out_shape.append(jax.ShapeDtypeStruct((rows, D_MODEL), BF16))
    return pl.pallas_call(
        functools.partial(_proj_res_kept_w_kernel, norm),
        grid=(rows // bm,),
        in_specs=[
            pl.BlockSpec((bm, kdim), lambda i: (i, 0)),
            pl.BlockSpec(memory_space=pl.ANY),
            row_spec,
            pl.BlockSpec((1, D_MODEL), lambda i: (0, 0)),
        ],
        out_specs=out_specs,
        out_shape=out_shape,
        scratch_shapes=[
            pltpu.VMEM((kdim, D_MODEL), BF16),
            pltpu.SemaphoreType.DMA((D_MODEL // W_CHUNK_COLS,)),
        ],
        compiler_params=_params(1),
        name=name,
    )(x, w, h, g)


def _swiglu_chunks(m_ref, wg_ref, wu_ref, a_ref):
    rows = a_ref.shape[0]
    rp = SWIGLU_ROWS if rows % SWIGLU_ROWS == 0 else rows
    for c0 in range(0, a_ref.shape[1], MXU_COLS):
        cols = slice(c0, c0 + MXU_COLS)
        for r0 in range(0, rows, rp):
            m = m_ref[r0:r0 + rp, :]
            gate = _dot(m, wg_ref[:, cols])
            up = _dot(m, wu_ref[:, cols])
            a_ref[r0:r0 + rp, cols] = (gate * (1.0 / (1.0 + jnp.exp(-gate))) * up).astype(BF16)


def _gate_up_p_kernel(n_cast, m_ref, wg_ref, wu_ref, *refs):
    _run_casts(refs[:n_cast], refs[n_cast + 1:])
    _swiglu_chunks(m_ref, wg_ref, wu_ref, refs[n_cast])


def _ffn_s_kernel(sample_out, n_pre, n_j, *refs):
    h_ref, refs = refs[0], refs[1:]
    if n_pre:
        (x_ref, wpre_ref), refs = refs[:2], refs[2:]
    g_ref, gfin_ref, wg_ref, wu_ref, wd_ref, o_ref, m_sc, a_sc = refs[:8]
    rest = list(refs[8:])
    hsrc = rest.pop(0) if n_pre else h_ref
    acc = rest.pop(0) if sample_out else o_ref
    s = pl.program_id(0)
    j = s - n_pre

    if n_pre:
        @pl.when(s < n_pre)
        def _():
            _proj_res_body(None, s, n_pre, x_ref, wpre_ref, h_ref, None, hsrc, None)

    @pl.when(s >= n_pre)
    def _():
        @pl.when(j == 0)
        def _():
            m_sc[...] = _rms(hsrc[...], g_ref[...]).astype(BF16)

        _swiglu_chunks(m_sc, wg_ref, wu_ref, a_sc)
        _proj_res_body(None, j, n_j, a_sc, wd_ref, hsrc, None, acc, None)

        if sample_out:
            @pl.when(j == n_j - 1)
            def _():
                y = _rms(acc[0:N_SAMPLE_ROWS, :], gfin_ref[...])
                for t in range(SAMPLE_T):
                    o_ref[:, t, :] = y[t * N_SAMPLE_SEQ:(t + 1) * N_SAMPLE_SEQ, :]


def _ffn_s(h, g, g_fin, w_gu, w_dn, bn, sample_out, name, pre=None):
    rows = h.shape[0]
    n_j = D_FF // bn
    n_pre = 0 if pre is None else pre[0].shape[1] // pre[2]
    ffn_step = lambda s: jnp.maximum(s - n_pre, 0)
    row_spec = pl.BlockSpec((1, D_MODEL), lambda s: (0, 0))
    o_shape = (N_SAMPLE_SEQ, SAMPLE_T, D_MODEL) if sample_out else (rows, D_MODEL)
    o_index = (0,) * len(o_shape)
    in_specs, args = [_resident((rows, D_MODEL), lambda s: (0, 0))], [h]
    scratch = [pltpu.VMEM((rows, D_MODEL), BF16), pltpu.VMEM((rows, bn), BF16)]
    if n_pre:
        x, w_pre, bk = pre
        pre_step = lambda s: jnp.minimum(s, n_pre - 1)
        in_specs += [pl.BlockSpec((rows, bk), lambda s: (0, pre_step(s))),
                     pl.BlockSpec((bk, D_MODEL), lambda s: (pre_step(s), 0))]
        args += [x, w_pre]
        scratch.append(pltpu.VMEM((rows, D_MODEL), F32))
    if sample_out:
        scratch.append(pltpu.VMEM((rows, D_MODEL), F32))
    in_specs += [
        row_spec, row_spec,
        pl.BlockSpec((D_MODEL, bn), lambda s: (0, ffn_step(s))),
        pl.BlockSpec((D_MODEL, bn), lambda s: (0, n_j + ffn_step(s))),
        pl.BlockSpec((bn, D_MODEL), lambda s: (ffn_step(s), 0)),
    ]
    return pl.pallas_call(
        functools.partial(_ffn_s_kernel, sample_out, n_pre, n_j),
        grid=(n_pre + n_j,),
        in_specs=in_specs,
        out_specs=pl.BlockSpec(o_shape, lambda s: o_index),
        out_shape=jax.ShapeDtypeStruct(o_shape, F32),
        scratch_shapes=scratch,
        compiler_params=_params(1),
        name=name,
    )(*args, g, g_fin, w_gu, w_gu, w_dn)


def _gate_up_p(m, w_gu, bm, bn, casts, name):
    rows = m.shape[0]
    n_i, n_j = rows // bm, D_FF // bn
    return pl.pallas_call(
        functools.partial(_gate_up_p_kernel, len(casts)),
        grid=(n_i, n_j),
        in_specs=[
            pl.BlockSpec((bm, D_MODEL), lambda i, j: (i, 0)),
            pl.BlockSpec((D_MODEL, bn), lambda i, j: (0, j)),
            pl.BlockSpec((D_MODEL, bn), lambda i, j: (0, n_j + j)),
        ] + [c.in_spec for c in casts],
        out_specs=[pl.BlockSpec((bm, bn), lambda i, j: (i, j))] + [c.out_spec for c in casts],
        out_shape=[jax.ShapeDtypeStruct((rows, D_FF), BF16)] + [c.out_shape for c in casts],
        compiler_params=_params(2),
        name=name,
    )(m, w_gu, w_gu, *[c.src for c in casts])


def _down_pool_kernel(tiles_per_seq, n_t, a_ref, w_hbm, h_ref, meta_ref, g_ref, wp_ref, sc_ref, gnext_ref,
                      o_ref, ntail_ref, m_ref, h2_even, h2_odd, ncarry, nbuf, pbuf, qbuf, w_vmem, sem):
    s = pl.program_id(0)
    bm = h_ref.shape[0]
    g = g_ref[...]
    rows = HALO + bm
    r0 = FRONT + HALO
    split = (3 * D_MODEL) // 4

    @pl.when((s - 1) % tiles_per_seq == 0)
    def _():
        nbuf[FRONT:r0, :] = _rms(meta_ref[...], g)

    @pl.when(jnp.logical_and(s >= 1, (s - 1) % tiles_per_seq != 0))
    def _():
        nbuf[FRONT:r0, :] = ncarry[...]

    def project(h2_new, c0, c1):
        h2_new[:, c0:c1] = h_ref[:, c0:c1] + _dot(a_ref[...], w_vmem[:, c0:c1])

    def step(h2_new, h2_old):
        if h2_old is None:
            n_chunks = D_MODEL // W_CHUNK_COLS
            for c in range(n_chunks):
                _weight_chunk_copy(w_hbm, w_vmem, sem, c).start()
            for c in range(n_chunks):
                _weight_chunk_copy(w_hbm, w_vmem, sem, c).wait()
                project(h2_new, c * W_CHUNK_COLS, (c + 1) * W_CHUNK_COLS)
            return
        if h2_new is not None:
            project(h2_new, 0, split)
        for buf in (nbuf, pbuf, qbuf):
            buf[0:FRONT, :] = jnp.zeros((FRONT, buf.shape[1]), F32)
        n = _rms(h2_old[...], g)
        nbuf[r0:r0 + bm, :] = n
        tail = n[bm - HALO:, :]
        ncarry[...] = tail
        ntail_ref[...] = tail
        for gi, win in enumerate(POOL_WINDOWS):
            cols = slice(gi * POOL_GROUP, (gi + 1) * POOL_GROUP)
            src, src_cols, shift, level = nbuf, cols, 1, 0
            while shift < win:
                dst = (pbuf, qbuf)[level % 2]
                dst[FRONT:FRONT + rows, :] = (src[FRONT:FRONT + rows, src_cols]
                                              + src[FRONT - shift:FRONT - shift + rows, src_cols])
                src, src_cols, shift, level = dst, slice(None), 2 * shift, level + 1
            p = src[r0:r0 + bm, src_cols] * (1.0 / win) - n[:, cols]
            y = _dot(p.astype(BF16), wp_ref[gi].astype(BF16)) * sc_ref[:, cols]
            o_ref[:, cols] = h2_old[:, cols] + y
        if h2_new is not None:
            project(h2_new, split, D_MODEL)
        m_ref[...] = _rms(o_ref[...], gnext_ref[...]).astype(BF16)

    @pl.when(s == 0)
    def _():
        step(h2_even, None)

    @pl.when(jnp.logical_and(s % 2 == 0, jnp.logical_and(s > 0, s < n_t)))
    def _():
        step(h2_even, h2_odd)

    @pl.when(jnp.logical_and(s % 2 == 1, s < n_t))
    def _():
        step(h2_odd, h2_even)

    @pl.when(s == n_t)
    def _():
        step(None, (h2_even, h2_odd)[(n_t - 1) % 2])


def _down_pool_p(a, w, h, h_s, g, wp, scale, g_next, bm):
    rows, kdim = a.shape
    n_t = rows // bm
    ng = len(POOL_WINDOWS)
    cur = lambda s: (jnp.minimum(s, n_t - 1), 0)
    prev = lambda s: (jnp.maximum(s - 1, 0), 0)
    vec = pl.BlockSpec((1, D_MODEL), lambda s: (0, 0))
    return pl.pallas_call(
        functools.partial(_down_pool_kernel, SEQ // bm, n_t),
        grid=(n_t + 1,),
        in_specs=[
            pl.BlockSpec((bm, kdim), cur),
            pl.BlockSpec(memory_space=pl.ANY),
            pl.BlockSpec((bm, D_MODEL), cur),
            _resident((HALO, D_MODEL), lambda s: (META_ROW0 // HALO, 0)),
            vec,
            _resident((None, ng, POOL_GROUP, POOL_GROUP), lambda s: (0, 0, 0, 0)),
            vec, vec,
        ],
        out_specs=[
            pl.BlockSpec((bm, D_MODEL), prev),
            pl.BlockSpec((HALO, D_MODEL), prev),
            pl.BlockSpec((bm, D_MODEL), prev),
        ],
        out_shape=[
            jax.ShapeDtypeStruct((rows, D_MODEL), F32),
            jax.ShapeDtypeStruct((n_t * HALO, D_MODEL), F32),
            jax.ShapeDtypeStruct((rows, D_MODEL), BF16),
        ],
        scratch_shapes=[
            pltpu.VMEM((bm, D_MODEL), F32),
            pltpu.VMEM((bm, D_MODEL), F32),
            pltpu.VMEM((HALO, D_MODEL), F32),
            pltpu.VMEM((FRONT + HALO + bm, D_MODEL), F32),
            pltpu.VMEM((FRONT + HALO + bm, POOL_GROUP), F32),
            pltpu.VMEM((FRONT + HALO + bm, POOL_GROUP), F32),
            pltpu.VMEM((kdim, D_MODEL), BF16),
            pltpu.SemaphoreType.DMA((D_MODEL // W_CHUNK_COLS,)),
        ],
        compiler_params=_params(1),
        name="down_pool_p",
    )(a, w, h, h_s, g, wp, scale, g_next)


def _pool_s_kernel(hfull_ref, hcol_ref, hp_ref, g_ref, wp_ref, sc_ref, o_ref, tail_ref, inv_sc, nbuf, sum_sc):
    j = pl.program_id(0)
    ns = N_SAMPLE_ROWS
    nb = N_SAMPLE_SEQ
    nh = POOL_HIST * nb

    @pl.when(j == 0)
    def _():
        xf = hfull_ref[...]
        inv_sc[...] = lax.rsqrt(jnp.mean(xf * xf, axis=-1, keepdims=True) + EPS)

    x = hcol_ref[...]
    n = ((x * inv_sc[...]) * g_ref[...])[0:ns, :]
    nbuf[0:nh, :] = hp_ref[...]
    nbuf[nh:nh + ns, :] = n
    tail_ref[...] = nbuf[ns:ns + nh, :]

    for gi, win in enumerate(POOL_WINDOWS):
        @pl.when(j == gi)
        def _(win=win):
            acc = n
            for k in range(1, win):
                acc = acc + nbuf[nh - k * nb:nh - k * nb + ns, :]
            sum_sc[...] = acc * (1.0 / win)

    p = sum_sc[...] - n
    y = _dot(p.astype(BF16), wp_ref[...].astype(BF16)) * sc_ref[...]
    o_ref[...] = x[0:ns, :] + y


def _pool_s(h_s, state_pool, g, wp, scale):
    ns = N_SAMPLE_ROWS
    nb = N_SAMPLE_SEQ
    nh = POOL_HIST * nb
    pg = POOL_GROUP
    n_g = len(POOL_WINDOWS)
    state_spec = pl.BlockSpec((nh, pg), lambda j: (0, j))
    return pl.pallas_call(
        _pool_s_kernel,
        grid=(n_g,),
        in_specs=[
            _resident((S_ROWS, D_MODEL), lambda j: (0, 0)),
            pl.BlockSpec((S_ROWS, pg), lambda j: (0, j)),
            state_spec,
            pl.BlockSpec((1, pg), lambda j: (0, j)),
            pl.BlockSpec((None, None, pg, pg), lambda j: (0, j, 0, 0)),
            pl.BlockSpec((1, pg), lambda j: (0, j)),
        ],
        out_specs=[pl.BlockSpec((ns, pg), lambda j: (0, j)), state_spec],
        out_shape=[
            jax.ShapeDtypeStruct((ns, D_MODEL), F32),
            jax.ShapeDtypeStruct((nh, D_MODEL), F32),
        ],
        scratch_shapes=[
            pltpu.VMEM((S_ROWS, 1), F32),
            pltpu.VMEM((nh + ns, pg), F32),
            pltpu.VMEM((ns, pg), F32),
        ],
        compiler_params=_params(1),
        name="pool_s",
    )(h_s, h_s, state_pool, g, wp, scale)


BM_P = 1024
BN_MIX = 512
BM_CONV_OUT = 512
BM_GATE_UP = 2048
BN_P = 512
BM_DOWN = 512
BM_POOL = 256
BN_S = 256
BN_S_FF = 512
BK_S = 512

assert SEQ % BM_P == 0 and SEQ % BM_POOL == 0 and BM_POOL % HALO == 0 and META_ROW0 % HALO == 0
assert D_MODEL % BN_MIX == 0 and D_MODEL % BN_S == 0 and BN_MIX % MXU_COLS == 0 and BN_S % MXU_COLS == 0
assert D_FF % BN_P == 0 and D_FF % BN_S_FF == 0 and D_FF % BK_S == 0 and D_MODEL % BK_S == 0


def kernel(x_prompt, x_sample, state_conv, state_pool, meta_tokens, norm_mix, norm_ffn, norm_final,
           conv_w_in, conv_w_dw, conv_w_out, pool_w, pool_scale, ffn_w_gate_up, ffn_w_down):
    d = D_MODEL
    nb, nt = N_SAMPLE_SEQ, SAMPLE_T
    batch = x_prompt.shape[0]
    assert x_prompt.shape == (batch, SEQ, d) and x_sample.shape == (nb, nt, d)
    assert state_conv.shape == (1, nb, CONV_HIST, d) and state_pool.shape == (1, nb, POOL_HIST, d)
    assert meta_tokens.shape == (N_META, d) and ffn_w_gate_up.shape == (2, d, 2 * D_FF)
    assert (batch * SEQ) % BM_GATE_UP == 0 and (batch * SEQ) % BM_DOWN == 0 and (batch * SEQ) % BM_CONV_OUT == 0

    h_p = x_prompt.reshape(batch * SEQ, d)
    hc = state_conv[0].transpose(1, 0, 2).reshape(CONV_HIST * nb, d)
    hp = state_pool[0].transpose(1, 0, 2).reshape(POOL_HIST * nb, d)
    row = lambda v: v.reshape(1, d)
    g_mix, g_ffn, g_fin = norm_mix, norm_ffn, row(norm_final)

    rows_p = batch * SEQ
    n_j_mix, n_j_ff = d // BN_MIX, D_FF // BN_P
    n_mix, n_ff = (rows_p // BM_P) * n_j_mix, (rows_p // BM_GATE_UP) * n_j_ff
    step_mix = lambda i, j: i * n_j_mix + j
    step_ff = lambda i, j: i * n_j_ff + j
    casts_mix = [_cast_job(conv_w_out, 0, n_mix, 0, step_mix), _cast_job(ffn_w_gate_up, 0, n_mix, 0, step_mix)]
    casts_ff0 = [_cast_job(ffn_w_down, 0, n_ff, 0, step_ff), _cast_job(ffn_w_gate_up, 1, n_ff, 1, step_ff)]
    casts_ff1 = [_cast_job(ffn_w_down, 1, n_ff, 0, step_ff)]

    z_s, uts, umeta, wb, wc, wv, h_s = _mix_conv_s(
        x_sample, meta_tokens, row(g_mix[0]), conv_w_in, conv_w_dw[0], hc, BN_S)
    z_p, utail_p, w_out, w_gu0 = _mix_conv_p(
        h_p, row(g_mix[0]), wb, wc, wv, conv_w_dw[0], umeta, BM_P, BN_MIX, casts_mix)
    h_p, m_p = _proj_res_p(z_p, w_out, h_p, row(g_ffn[0]), BM_CONV_OUT, "emit", "conv_out_p")
    a_p, w_dn0, w_gu1 = _gate_up_p(m_p, w_gu0, BM_GATE_UP, BN_P, casts_ff0, "gate_up0_p")
    h_s = _ffn_s(h_s, row(g_ffn[0]), g_fin, w_gu0, w_dn0, BN_S_FF, False, "conv_out_ffn0_s", pre=(z_s, w_out, BK_S))
    h_p1, ntail_p, m_p = _down_pool_p(a_p, w_dn0, h_p, h_s, row(g_mix[1]), pool_w, row(pool_scale[0]),
                                      row(g_ffn[1]), BM_POOL)
    h_s1, pool_tail_s = _pool_s(h_s, hp, row(g_mix[1]), pool_w, row(pool_scale[0]))
    a_p, w_dn1 = _gate_up_p(m_p, w_gu1, BM_GATE_UP, BN_P, casts_ff1, "gate_up1_p")
    y_s = _ffn_s(h_s1, row(g_ffn[1]), g_fin, w_gu1, w_dn1, BN_S_FF, True, "ffn1_s")
    (y_p,) = _proj_res_p(a_p, w_dn1, h_p1, g_fin, BM_DOWN, "final", "down1_p")

    y_prompt = y_p.reshape(batch, SEQ, d)
    y_sample = y_s
    tps = SEQ // BM_P
    new_conv_prompt = utail_p.reshape(batch, tps, TAIL, d)[:, tps - 1, TAIL - CONV_HIST:, :][None]
    tpp = SEQ // BM_POOL
    new_pool_prompt = ntail_p.reshape(batch, tpp, HALO, d)[:, tpp - 1, HALO - POOL_HIST:, :][None]
    new_conv_sample = uts.reshape(CONV_HIST, nb, d).transpose(1, 0, 2)[None]
    new_pool_sample = pool_tail_s.reshape(POOL_HIST, nb, d).transpose(1, 0, 2)[None]
    return (y_prompt, y_sample, new_conv_prompt, new_pool_prompt, new_conv_sample, new_pool_sample)
```

```python
import functools
from typing import NamedTuple

import jax
import jax.numpy as jnp
from jax import lax
from jax.experimental import pallas as pl
from jax.experimental.pallas import tpu as pltpu

D_MODEL = 2048
D_FF = 5632
N_META = 16
SEQ = 2048
N_SAMPLE_SEQ = 128
SAMPLE_T = 8
CONV_HIST = 2
POOL_WINDOWS = (2, 4, 8, 16)
POOL_GROUP = D_MODEL // len(POOL_WINDOWS)
POOL_HIST = 15
EPS = 1e-6

N_SAMPLE_ROWS = N_SAMPLE_SEQ * SAMPLE_T
S_ROWS = N_SAMPLE_ROWS + N_META
META_ROW0 = N_SAMPLE_ROWS
TAIL = 8
HALO = 16
MXU_COLS = 256
FRONT = 8
SWIGLU_ROWS = 1024
W_CHUNK_COLS = 512

V7X_SCOPED_VMEM_BYTES = 60000 * 1024

BF16 = jnp.bfloat16
F32 = jnp.float32


def _rms(x, g):
    ms = jnp.mean(x * x, axis=-1, keepdims=True)
    return (x * lax.rsqrt(ms + EPS)) * g


def _dot(a, b):
    return jnp.dot(a, b, preferred_element_type=F32)


def _params(n_axes):
    return pltpu.CompilerParams(
        dimension_semantics=("arbitrary",) * n_axes,
        vmem_limit_bytes=V7X_SCOPED_VMEM_BYTES,
    )


def _resident(shape, index_map):
    return pl.BlockSpec(shape, index_map, pipeline_mode=pl.Buffered(1))


class _CastJob(NamedTuple):
    src: jax.Array
    in_spec: pl.BlockSpec
    out_spec: pl.BlockSpec
    out_shape: jax.ShapeDtypeStruct


def _cast_job(w_stack, layer, n_steps, axis, step_of):
    _, r, c = w_stack.shape
    if axis == 0:
        blk = (r // n_steps, c)
        in_idx = lambda *ids: (layer, step_of(*ids), 0)
        out_idx = lambda *ids: (step_of(*ids), 0)
    else:
        blk = (r, c // n_steps)
        in_idx = lambda *ids: (layer, 0, step_of(*ids))
        out_idx = lambda *ids: (0, step_of(*ids))
    assert blk[0] * (n_steps if axis == 0 else 1) == r and blk[1] * (n_steps if axis == 1 else 1) == c
    return _CastJob(w_stack, pl.BlockSpec((None,) + blk, in_idx), pl.BlockSpec(blk, out_idx),
                    jax.ShapeDtypeStruct((r, c), BF16))


def _run_casts(src_refs, dst_refs):
    for src, dst in zip(src_refs, dst_refs):
        dst[...] = src[...].astype(BF16)


def _conv3(w, u, ubuf, off):
    t = u.shape[0]
    return (w[2:3] * u + w[1:2] * ubuf[off - 1:off - 1 + t, :] + w[0:1] * ubuf[off - 2:off - 2 + t, :])


def _mix_p_kernel(tiles_per_seq, n_cast, *refs):
    h_ref, g_ref, wb_ref, wc_ref, wv_ref, wdw_ref, umeta_ref = refs[:7]
    cast_src, refs = refs[7:7 + n_cast], refs[7 + n_cast:]
    z_ref, utail_ref = refs[:2]
    cast_dst, (n_sc, ubuf, carry) = refs[2:2 + n_cast], refs[2 + n_cast:]
    i = pl.program_id(0)
    j = pl.program_id(1)
    bm = h_ref.shape[0]
    _run_casts(cast_src, cast_dst)

    @pl.when(j == 0)
    def _():
        n_sc[...] = _rms(h_ref[...], g_ref[...]).astype(BF16)

    @pl.when(i % tiles_per_seq == 0)
    def _():
        ubuf[0:TAIL, :] = umeta_ref[...]

    @pl.when(i % tiles_per_seq != 0)
    def _():
        ubuf[0:TAIL, :] = carry[j]

    n = n_sc[...]
    for c0 in range(0, z_ref.shape[1], MXU_COLS):
        cols = slice(c0, c0 + MXU_COLS)
        b = _dot(n, wb_ref[:, cols])
        u = _dot(n, wc_ref[:, cols]) * _dot(n, wv_ref[:, cols])
        ubuf[TAIL:TAIL + bm, cols] = u
        z_ref[:, cols] = (b * _conv3(wdw_ref[:, cols], u, ubuf.at[:, cols], TAIL)).astype(BF16)
    tail = ubuf[bm:bm + TAIL, :]
    carry[j] = tail
    utail_ref[...] = tail


def _mix_s_kernel(xs_ref, meta_ref, g_ref, wb_ref, wc_ref, wv_ref, wdw_ref, hc_ref,
                  z_ref, uts_ref, umeta_ref, wb_o, wc_o, wv_o, h0_ref, n_sc, ubuf, mbuf):
    j = pl.program_id(0)
    ns = N_SAMPLE_ROWS
    nb = N_SAMPLE_SEQ
    nm = S_ROWS - ns

    @pl.when(j == 0)
    def _():
        for t in range(SAMPLE_T):
            h0_ref[t * nb:(t + 1) * nb, :] = xs_ref[:, t, :]
        h0_ref[ns:, :] = meta_ref[...]
        n_sc[...] = _rms(h0_ref[...], g_ref[...]).astype(BF16)

    wb_o[...] = wb_ref[...].astype(BF16)
    wc_o[...] = wc_ref[...].astype(BF16)
    wv_o[...] = wv_ref[...].astype(BF16)
    n = n_sc[...]
    nh = CONV_HIST * nb
    ubuf[0:nh, :] = hc_ref[...]
    mbuf[0:TAIL, :] = jnp.zeros((TAIL, mbuf.shape[1]), F32)
    for c0 in range(0, z_ref.shape[1], MXU_COLS):
        cols = slice(c0, c0 + MXU_COLS)
        b = _dot(n, wb_o[:, cols])
        u = _dot(n, wc_o[:, cols]) * _dot(n, wv_o[:, cols])
        w = wdw_ref[:, cols]

        us = u[0:ns, :]
        ubuf[nh:nh + ns, cols] = us
        conv_s = (w[2:3] * us + w[1:2] * ubuf[nb:nb + ns, cols] + w[0:1] * ubuf[0:ns, cols])
        z_ref[0:ns, cols] = (b[0:ns, :] * conv_s).astype(BF16)

        um = u[ns:, :]
        mbuf[TAIL:TAIL + nm, cols] = um
        z_ref[ns:, cols] = (b[ns:, :] * _conv3(w, um, mbuf.at[:, cols], TAIL)).astype(BF16)
    uts_ref[...] = ubuf[ns:ns + nh, :]
    umeta_ref[...] = mbuf[N_META:N_META + TAIL, :]


def _mix_conv_p(h, g, wb, wc, wv, w_dw, umeta, bm, bn, casts):
    rows = h.shape[0]
    n_i, n_j = rows // bm, D_MODEL // bn
    wspec = pl.BlockSpec((D_MODEL, bn), lambda i, j: (0, j))
    return pl.pallas_call(
        functools.partial(_mix_p_kernel, SEQ // bm, len(casts)),
        grid=(n_i, n_j),
        in_specs=[
            pl.BlockSpec((bm, D_MODEL), lambda i, j: (i, 0)),
            pl.BlockSpec((1, D_MODEL), lambda i, j: (0, 0)),
            wspec, wspec, wspec,
            pl.BlockSpec((3, bn), lambda i, j: (0, j)),
            pl.BlockSpec((TAIL, bn), lambda i, j: (0, j)),
        ] + [c.in_spec for c in casts],
        out_specs=[
            pl.BlockSpec((bm, bn), lambda i, j: (i, j)),
            pl.BlockSpec((TAIL, bn), lambda i, j: (i, j)),
        ] + [c.out_spec for c in casts],
        out_shape=[
            jax.ShapeDtypeStruct((rows, D_MODEL), BF16),
            jax.ShapeDtypeStruct((n_i * TAIL, D_MODEL), F32),
        ] + [c.out_shape for c in casts],
        scratch_shapes=[
            pltpu.VMEM((bm, D_MODEL), BF16),
            pltpu.VMEM((TAIL + bm, bn), F32),
            pltpu.VMEM((n_j, TAIL, bn), F32),
        ],
        compiler_params=_params(2),
        name="mix_conv_p",
    )(h, g, wb, wc, wv, w_dw, umeta, *[c.src for c in casts])


def _mix_conv_s(x_sample, meta, g, w_in, w_dw, hc, bn):
    n_j = D_MODEL // bn
    nb = N_SAMPLE_SEQ
    nm = S_ROWS - N_SAMPLE_ROWS
    wspec = lambda part: pl.BlockSpec((None, D_MODEL, bn), lambda j: (0, 0, part * n_j + j))
    wout = pl.BlockSpec((D_MODEL, bn), lambda j: (0, j))
    wshape = jax.ShapeDtypeStruct((D_MODEL, D_MODEL), BF16)
    return pl.pallas_call(
        _mix_s_kernel,
        grid=(n_j,),
        in_specs=[
            _resident((nb, SAMPLE_T, D_MODEL), lambda j: (0, 0, 0)),
            _resident((N_META, D_MODEL), lambda j: (0, 0)),
            pl.BlockSpec((1, D_MODEL), lambda j: (0, 0)),
            wspec(0), wspec(1), wspec(2),
            pl.BlockSpec((3, bn), lambda j: (0, j)),
            pl.BlockSpec((CONV_HIST * nb, bn), lambda j: (0, j)),
        ],
        out_specs=[
            pl.BlockSpec((S_ROWS, bn), lambda j: (0, j)),
            pl.BlockSpec((CONV_HIST * nb, bn), lambda j: (0, j)),
            pl.BlockSpec((TAIL, bn), lambda j: (0, j)),
            wout, wout, wout,
            pl.BlockSpec((S_ROWS, D_MODEL), lambda j: (0, 0)),
        ],
        out_shape=[
            jax.ShapeDtypeStruct((S_ROWS, D_MODEL), BF16),
            jax.ShapeDtypeStruct((CONV_HIST * nb, D_MODEL), F32),
            jax.ShapeDtypeStruct((TAIL, D_MODEL), F32),
            wshape, wshape, wshape,
            jax.ShapeDtypeStruct((S_ROWS, D_MODEL), F32),
        ],
        scratch_shapes=[
            pltpu.VMEM((S_ROWS, D_MODEL), BF16),
            pltpu.VMEM((CONV_HIST * nb + N_SAMPLE_ROWS, bn), F32),
            pltpu.VMEM((TAIL + nm, bn), F32),
        ],
        compiler_params=_params(1),
        name="mix_conv_s",
    )(x_sample, meta, g, w_in, w_in, w_in, w_dw, hc)


def _proj_res_body(norm, k, n_k, x_ref, w_ref, h_ref, g_ref, o_ref, m_ref):
    @pl.when(k == 0)
    def _():
        o_ref[...] = h_ref[...] + _dot(x_ref[...], w_ref[...])

    @pl.when(k != 0)
    def _():
        o_ref[...] += _dot(x_ref[...], w_ref[...])

    if norm is not None:
        @pl.when(k == n_k - 1)
        def _():
            y = _rms(o_ref[...], g_ref[...])
            if norm == "final":
                o_ref[...] = y
            else:
                m_ref[...] = y.astype(BF16)


def _weight_chunk_copy(w_hbm, w_vmem, sem, c):
    cols = pl.ds(c * W_CHUNK_COLS, W_CHUNK_COLS)
    return pltpu.make_async_copy(w_hbm.at[:, cols], w_vmem.at[:, cols], sem.at[c])


def _proj_res_kept_w_kernel(norm, x_ref, w_hbm, h_ref, g_ref, o_ref, *rest):
    m_ref = rest[0] if norm == "emit" else None
    w_vmem, sem = rest[-2:]
    n_chunks = w_vmem.shape[1] // W_CHUNK_COLS

    def finish(acc):
        o_ref[...] = _rms(acc, g_ref[...]) if norm == "final" else acc
        if norm == "emit":
            m_ref[...] = _rms(acc, g_ref[...]).astype(BF16)

    @pl.when(pl.program_id(0) == 0)
    def _():
        for c in range(n_chunks):
            _weight_chunk_copy(w_hbm, w_vmem, sem, c).start()
        for c in range(n_chunks):
            _weight_chunk_copy(w_hbm, w_vmem, sem, c).wait()
            cols = slice(c * W_CHUNK_COLS, (c + 1) * W_CHUNK_COLS)
            o_ref[:, cols] = h_ref[:, cols] + _dot(x_ref[...], w_vmem[:, cols])
        if norm is not None:
            finish(o_ref[...])

    @pl.when(pl.program_id(0) != 0)
    def _():
        finish(h_ref[...] + _dot(x_ref[...], w_vmem[...]))


def _proj_res_p(x, w, h, g, bm, norm, name):
    rows, kdim = x.shape
    row_spec = pl.BlockSpec((bm, D_MODEL), lambda i: (i, 0))
    out_specs, out_shape = [row_spec], [jax.ShapeDtypeStruct((rows, D_MODEL), F32)]
    if norm == "emit":
        out_specs.append(row_spec)
        out_shape.append(jax.ShapeDtypeStruct((rows, D_MODEL), BF16))
    return pl.pallas_call(
        functools.partial(_proj_res_kept_w_kernel, norm),
        grid=(rows // bm,),
        in_specs=[
            pl.BlockSpec((bm, kdim), lambda i: (i, 0)),
            pl.BlockSpec(memory_space=pl.ANY),
            row_spec,
            pl.BlockSpec((1, D_MODEL), lambda i: (0, 0)),
        ],
        out_specs=out_specs,
        out_shape=out_shape,
        scratch_shapes=[
            pltpu.VMEM((kdim, D_MODEL), BF16),
            pltpu.SemaphoreType.DMA((D_MODEL // W_CHUNK_COLS,)),
        ],
        compiler_params=_params(1),
        name=name,
    )(x, w, h, g)


def _swiglu_chunks(m_ref, wg_ref, wu_ref, a_ref):
    rows = a_ref.shape[0]
    rp = SWIGLU_ROWS if rows % SWIGLU_ROWS == 0 else rows
    for c0 in range(0, a_ref.shape[1], MXU_COLS):
        cols = slice(c0, c0 + MXU_COLS)
        for r0 in range(0, rows, rp):
            m = m_ref[r0:r0 + rp, :]
            gate = _dot(m, wg_ref[:, cols])
            up = _dot(m, wu_ref[:, cols])
            a_ref[r0:r0 + rp, cols] = (gate * (1.0 / (1.0 + jnp.exp(-gate))) * up).astype(BF16)


def _gate_up_p_kernel(n_cast, m_ref, wg_ref, wu_ref, *refs):
    _run_casts(refs[:n_cast], refs[n_cast + 1:])
    _swiglu_chunks(m_ref, wg_ref, wu_ref, refs[n_cast])


def _ffn_s_kernel(sample_out, n_pre, n_j, *refs):
    h_ref, refs = refs[0], refs[1:]
    if n_pre:
        (x_ref, wpre_ref), refs = refs[:2], refs[2:]
    g_ref, gfin_ref, wg_ref, wu_ref, wd_ref, o_ref, m_sc, a_sc = refs[:8]
    rest = list(refs[8:])
    hsrc = rest.pop(0) if n_pre else h_ref
    acc = rest.pop(0) if sample_out else o_ref
    s = pl.program_id(0)
    j = s - n_pre

    if n_pre:
        @pl.when(s < n_pre)
        def _():
            _proj_res_body(None, s, n_pre, x_ref, wpre_ref, h_ref, None, hsrc, None)

    @pl.when(s >= n_pre)
    def _():
        @pl.when(j == 0)
        def _():
            m_sc[...] = _rms(hsrc[...], g_ref[...]).astype(BF16)

        _swiglu_chunks(m_sc, wg_ref, wu_ref, a_sc)
        _proj_res_body(None, j, n_j, a_sc, wd_ref, hsrc, None, acc, None)

        if sample_out:
            @pl.when(j == n_j - 1)
            def _():
                y = _rms(acc[0:N_SAMPLE_ROWS, :], gfin_ref[...])
                for t in range(SAMPLE_T):
                    o_ref[:, t, :] = y[t * N_SAMPLE_SEQ:(t + 1) * N_SAMPLE_SEQ, :]


def _ffn_s(h, g, g_fin, w_gu, w_dn, bn, sample_out, name, pre=None):
    rows = h.shape[0]
    n_j = D_FF // bn
    n_pre = 0 if pre is None else pre[0].shape[1] // pre[2]
    ffn_step = lambda s: jnp.maximum(s - n_pre, 0)
    row_spec = pl.BlockSpec((1, D_MODEL), lambda s: (0, 0))
    o_shape = (N_SAMPLE_SEQ, SAMPLE_T, D_MODEL) if sample_out else (rows, D_MODEL)
    o_index = (0,) * len(o_shape)
    in_specs, args = [_resident((rows, D_MODEL), lambda s: (0, 0))], [h]
    scratch = [pltpu.VMEM((rows, D_MODEL), BF16), pltpu.VMEM((rows, bn), BF16)]
    if n_pre:
        x, w_pre, bk = pre
        pre_step = lambda s: jnp.minimum(s, n_pre - 1)
        in_specs += [pl.BlockSpec((rows, bk), lambda s: (0, pre_step(s))),
                     pl.BlockSpec((bk, D_MODEL), lambda s: (pre_step(s), 0))]
        args += [x, w_pre]
        scratch.append(pltpu.VMEM((rows, D_MODEL), F32))
    if sample_out:
        scratch.append(pltpu.VMEM((rows, D_MODEL), F32))
    in_specs += [
        row_spec, row_spec,
        pl.BlockSpec((D_MODEL, bn), lambda s: (0, ffn_step(s))),
        pl.BlockSpec((D_MODEL, bn), lambda s: (0, n_j + ffn_step(s))),
        pl.BlockSpec((bn, D_MODEL), lambda s: (ffn_step(s), 0)),
    ]
    return pl.pallas_call(
        functools.partial(_ffn_s_kernel, sample_out, n_pre, n_j),
        grid=(n_pre + n_j,),
        in_specs=in_specs,
        out_specs=pl.BlockSpec(o_shape, lambda s: o_index),
        out_shape=jax.ShapeDtypeStruct(o_shape, F32),
        scratch_shapes=scratch,
        compiler_params=_params(1),
        name=name,
    )(*args, g, g_fin, w_gu, w_gu, w_dn)


def _gate_up_p(m, w_gu, bm, bn, casts, name):
    rows = m.shape[0]
    n_i, n_j = rows // bm, D_FF // bn
    return pl.pallas_call(
        functools.partial(_gate_up_p_kernel, len(casts)),
        grid=(n_i, n_j),
        in_specs=[
            pl.BlockSpec((bm, D_MODEL), lambda i, j: (i, 0)),
            pl.BlockSpec((D_MODEL, bn), lambda i, j: (0, j)),
            pl.BlockSpec((D_MODEL, bn), lambda i, j: (0, n_j + j)),
        ] + [c.in_spec for c in casts],
        out_specs=[pl.BlockSpec((bm, bn), lambda i, j: (i, j))] + [c.out_spec for c in casts],
        out_shape=[jax.ShapeDtypeStruct((rows, D_FF), BF16)] + [c.out_shape for c in casts],
        compiler_params=_params(2),
        name=name,
    )(m, w_gu, w_gu, *[c.src for c in casts])


def _down_pool_kernel(tiles_per_seq, n_t, a_ref, w_hbm, h_ref, meta_ref, g_ref, wp_ref, sc_ref, gnext_ref,
                      o_ref, ntail_ref, m_ref, h2_even, h2_odd, ncarry, nbuf, pbuf, qbuf, w_vmem, sem):
    s = pl.program_id(0)
    bm = h_ref.shape[0]
    g = g_ref[...]
    rows = HALO + bm
    r0 = FRONT + HALO
    split = (3 * D_MODEL) // 4

    @pl.when((s - 1) % tiles_per_seq == 0)
    def _():
        nbuf[FRONT:r0, :] = _rms(meta_ref[...], g)

    @pl.when(jnp.logical_and(s >= 1, (s - 1) % tiles_per_seq != 0))
    def _():
        nbuf[FRONT:r0, :] = ncarry[...]

    def project(h2_new, c0, c1):
        h2_new[:, c0:c1] = h_ref[:, c0:c1] + _dot(a_ref[...], w_vmem[:, c0:c1])

    def step(h2_new, h2_old):
        if h2_old is None:
            n_chunks = D_MODEL // W_CHUNK_COLS
            for c in range(n_chunks):
                _weight_chunk_copy(w_hbm, w_vmem, sem, c).start()
            for c in range(n_chunks):
                _weight_chunk_copy(w_hbm, w_vmem, sem, c).wait()
                project(h2_new, c * W_CHUNK_COLS, (c + 1) * W_CHUNK_COLS)
            return
        if h2_new is not None:
            project(h2_new, 0, split)
        for buf in (nbuf, pbuf, qbuf):
            buf[0:FRONT, :] = jnp.zeros((FRONT, buf.shape[1]), F32)
        n = _rms(h2_old[...], g)
        nbuf[r0:r0 + bm, :] = n
        tail = n[bm - HALO:, :]
        ncarry[...] = tail
        ntail_ref[...] = tail
        for gi, win in enumerate(POOL_WINDOWS):
            cols = slice(gi * POOL_GROUP, (gi + 1) * POOL_GROUP)
            src, src_cols, shift, level = nbuf, cols, 1, 0
            while shift < win:
                dst = (pbuf, qbuf)[level % 2]
                dst[FRONT:FRONT + rows, :] = (src[FRONT:FRONT + rows, src_cols]
                                              + src[FRONT - shift:FRONT - shift + rows, src_cols])
                src, src_cols, shift, level = dst, slice(None), 2 * shift, level + 1
            p = src[r0:r0 + bm, src_cols] * (1.0 / win) - n[:, cols]
            y = _dot(p.astype(BF16), wp_ref[gi].astype(BF16)) * sc_ref[:, cols]
            o_ref[:, cols] = h2_old[:, cols] + y
        if h2_new is not None:
            project(h2_new, split, D_MODEL)
        m_ref[...] = _rms(o_ref[...], gnext_ref[...]).astype(BF16)

    @pl.when(s == 0)
    def _():
        step(h2_even, None)

    @pl.when(jnp.logical_and(s % 2 == 0, jnp.logical_and(s > 0, s < n_t)))
    def _():
        step(h2_even, h2_odd)

    @pl.when(jnp.logical_and(s % 2 == 1, s < n_t))
    def _():
        step(h2_odd, h2_even)

    @pl.when(s == n_t)
    def _():
        step(None, (h2_even, h2_odd)[(n_t - 1) % 2])


def _down_pool_p(a, w, h, h_s, g, wp, scale, g_next, bm):
    rows, kdim = a.shape
    n_t = rows // bm
    ng = len(POOL_WINDOWS)
    cur = lambda s: (jnp.minimum(s, n_t - 1), 0)
    prev = lambda s: (jnp.maximum(s - 1, 0), 0)
    vec = pl.BlockSpec((1, D_MODEL), lambda s: (0, 0))
    return pl.pallas_call(
        functools.partial(_down_pool_kernel, SEQ // bm, n_t),
        grid=(n_t + 1,),
        in_specs=[
            pl.BlockSpec((bm, kdim), cur),
            pl.BlockSpec(memory_space=pl.ANY),
            pl.BlockSpec((bm, D_MODEL), cur),
            _resident((HALO, D_MODEL), lambda s: (META_ROW0 // HALO, 0)),
            vec,
            _resident((None, ng, POOL_GROUP, POOL_GROUP), lambda s: (0, 0, 0, 0)),
            vec, vec,
        ],
        out_specs=[
            pl.BlockSpec((bm, D_MODEL), prev),
            pl.BlockSpec((HALO, D_MODEL), prev),
            pl.BlockSpec((bm, D_MODEL), prev),
        ],
        out_shape=[
            jax.ShapeDtypeStruct((rows, D_MODEL), F32),
            jax.ShapeDtypeStruct((n_t * HALO, D_MODEL), F32),
            jax.ShapeDtypeStruct((rows, D_MODEL), BF16),
        ],
        scratch_shapes=[
            pltpu.VMEM((bm, D_MODEL), F32),
            pltpu.VMEM((bm, D_MODEL), F32),
            pltpu.VMEM((HALO, D_MODEL), F32),
            pltpu.VMEM((FRONT + HALO + bm, D_MODEL), F32),
            pltpu.VMEM((FRONT + HALO + bm, POOL_GROUP), F32),
            pltpu.VMEM((FRONT + HALO + bm, POOL_GROUP), F32),
            pltpu.VMEM((kdim, D_MODEL), BF16),
            pltpu.SemaphoreType.DMA((D_MODEL // W_CHUNK_COLS,)),
        ],
        compiler_params=_params(1),
        name="down_pool_p",
    )(a, w, h, h_s, g, wp, scale, g_next)


def _pool_s_kernel(hfull_ref, hcol_ref, hp_ref, g_ref, wp_ref, sc_ref, o_ref, tail_ref, inv_sc, nbuf, sum_sc):
    j = pl.program_id(0)
    ns = N_SAMPLE_ROWS
    nb = N_SAMPLE_SEQ
    nh = POOL_HIST * nb

    @pl.when(j == 0)
    def _():
        xf = hfull_ref[...]
        inv_sc[...] = lax.rsqrt(jnp.mean(xf * xf, axis=-1, keepdims=True) + EPS)

    x = hcol_ref[...]
    n = ((x * inv_sc[...]) * g_ref[...])[0:ns, :]
    nbuf[0:nh, :] = hp_ref[...]
    nbuf[nh:nh + ns, :] = n
    tail_ref[...] = nbuf[ns:ns + nh, :]

    for gi, win in enumerate(POOL_WINDOWS):
        @pl.when(j == gi)
        def _(win=win):
            acc = n
            for k in range(1, win):
                acc = acc + nbuf[nh - k * nb:nh - k * nb + ns, :]
            sum_sc[...] = acc * (1.0 / win)

    p = sum_sc[...] - n
    y = _dot(p.astype(BF16), wp_ref[...].astype(BF16)) * sc_ref[...]
    o_ref[...] = x[0:ns, :] + y


def _pool_s(h_s, state_pool, g, wp, scale):
    ns = N_SAMPLE_ROWS
    nb = N_SAMPLE_SEQ
    nh = POOL_HIST * nb
    pg = POOL_GROUP
    n_g = len(POOL_WINDOWS)
    state_spec = pl.BlockSpec((nh, pg), lambda j: (0, j))
    return pl.pallas_call(
        _pool_s_kernel,
        grid=(n_g,),
        in_specs=[
            _resident((S_ROWS, D_MODEL), lambda j: (0, 0)),
            pl.BlockSpec((S_ROWS, pg), lambda j: (0, j)),
            state_spec,
            pl.BlockSpec((1, pg), lambda j: (0, j)),
            pl.BlockSpec((None, None, pg, pg), lambda j: (0, j, 0, 0)),
            pl.BlockSpec((1, pg), lambda j: (0, j)),
        ],
        out_specs=[pl.BlockSpec((ns, pg), lambda j: (0, j)), state_spec],
        out_shape=[
            jax.ShapeDtypeStruct((ns, D_MODEL), F32),
            jax.ShapeDtypeStruct((nh, D_MODEL), F32),
        ],
        scratch_shapes=[
            pltpu.VMEM((S_ROWS, 1), F32),
            pltpu.VMEM((nh + ns, pg), F32),
            pltpu.VMEM((ns, pg), F32),
        ],
        compiler_params=_params(1),
        name="pool_s",
    )(h_s, h_s, state_pool, g, wp, scale)


BM_P = 1024
BN_MIX = 512
BM_CONV_OUT = 512
BM_GATE_UP = 2048
BM_GATE_UP1 = 4096
BN_P = 512
BM_DOWN = 512
BM_POOL = 256
BN_S = 256
BN_S_FF = 512
BK_S = 512

assert SEQ % BM_P == 0 and SEQ % BM_POOL == 0 and BM_POOL % HALO == 0 and META_ROW0 % HALO == 0
assert D_MODEL % BN_MIX == 0 and D_MODEL % BN_S == 0 and BN_MIX % MXU_COLS == 0 and BN_S % MXU_COLS == 0
assert D_FF % BN_P == 0 and D_FF % BN_S_FF == 0 and D_FF % BK_S == 0 and D_MODEL % BK_S == 0


def kernel(x_prompt, x_sample, state_conv, state_pool, meta_tokens, norm_mix, norm_ffn, norm_final,
           conv_w_in, conv_w_dw, conv_w_out, pool_w, pool_scale, ffn_w_gate_up, ffn_w_down):
    d = D_MODEL
    nb, nt = N_SAMPLE_SEQ, SAMPLE_T
    batch = x_prompt.shape[0]
    assert x_prompt.shape == (batch, SEQ, d) and x_sample.shape == (nb, nt, d)
    assert state_conv.shape == (1, nb, CONV_HIST, d) and state_pool.shape == (1, nb, POOL_HIST, d)
    assert meta_tokens.shape == (N_META, d) and ffn_w_gate_up.shape == (2, d, 2 * D_FF)
    assert (batch * SEQ) % BM_GATE_UP == 0 and (batch * SEQ) % BM_DOWN == 0 and (batch * SEQ) % BM_CONV_OUT == 0
    assert (batch * SEQ) % BM_GATE_UP1 == 0

    h_p = x_prompt.reshape(batch * SEQ, d)
    hc = state_conv[0].transpose(1, 0, 2).reshape(CONV_HIST * nb, d)
    hp = state_pool[0].transpose(1, 0, 2).reshape(POOL_HIST * nb, d)
    row = lambda v: v.reshape(1, d)
    g_mix, g_ffn, g_fin = norm_mix, norm_ffn, row(norm_final)

    rows_p = batch * SEQ
    n_j_mix, n_j_ff = d // BN_MIX, D_FF // BN_P
    n_mix, n_ff = (rows_p // BM_P) * n_j_mix, (rows_p // BM_GATE_UP) * n_j_ff
    step_mix = lambda i, j: i * n_j_mix + j
    step_ff = lambda i, j: i * n_j_ff + j
    casts_mix = [_cast_job(conv_w_out, 0, n_mix, 0, step_mix), _cast_job(ffn_w_gate_up, 0, n_mix, 0, step_mix)]
    casts_ff0 = [_cast_job(ffn_w_down, 0, n_ff, 0, step_ff), _cast_job(ffn_w_gate_up, 1, n_ff, 1, step_ff)]
    casts_ff1 = [_cast_job(ffn_w_down, 1, (rows_p // BM_GATE_UP1) * n_j_ff, 0, step_ff)]

    z_s, uts, umeta, wb, wc, wv, h_s = _mix_conv_s(
        x_sample, meta_tokens, row(g_mix[0]), conv_w_in, conv_w_dw[0], hc, BN_S)
    z_p, utail_p, w_out, w_gu0 = _mix_conv_p(
        h_p, row(g_mix[0]), wb, wc, wv, conv_w_dw[0], umeta, BM_P, BN_MIX, casts_mix)
    h_p, m_p = _proj_res_p(z_p, w_out, h_p, row(g_ffn[0]), BM_CONV_OUT, "emit", "conv_out_p")
    a_p, w_dn0, w_gu1 = _gate_up_p(m_p, w_gu0, BM_GATE_UP, BN_P, casts_ff0, "gate_up0_p")
    h_s = _ffn_s(h_s, row(g_ffn[0]), g_fin, w_gu0, w_dn0, BN_S_FF, False, "conv_out_ffn0_s", pre=(z_s, w_out, BK_S))
    h_p1, ntail_p, m_p = _down_pool_p(a_p, w_dn0, h_p, h_s, row(g_mix[1]), pool_w, row(pool_scale[0]),
                                      row(g_ffn[1]), BM_POOL)
    h_s1, pool_tail_s = _pool_s(h_s, hp, row(g_mix[1]), pool_w, row(pool_scale[0]))
    a_p, w_dn1 = _gate_up_p(m_p, w_gu1, BM_GATE_UP1, BN_P, casts_ff1, "gate_up1_p")
    y_s = _ffn_s(h_s1, row(g_ffn[1]), g_fin, w_gu1, w_dn1, BN_S_FF, True, "ffn1_s")
    (y_p,) = _proj_res_p(a_p, w_dn1, h_p1, g_fin, BM_DOWN, "final", "down1_p")

    y_prompt = y_p.reshape(batch, SEQ, d)
    y_sample = y_s
    tps = SEQ // BM_P
    new_conv_prompt = utail_p.reshape(batch, tps, TAIL, d)[:, tps - 1, TAIL - CONV_HIST:, :][None]
    tpp = SEQ // BM_POOL
    new_pool_prompt = ntail_p.reshape(batch, tpp, HALO, d)[:, tpp - 1, HALO - POOL_HIST:, :][None]
    new_conv_sample = uts.reshape(CONV_HIST, nb, d).transpose(1, 0, 2)[None]
    new_pool_sample = pool_tail_s.reshape(POOL_HIST, nb, d).transpose(1, 0, 2)[None]
    return (y_prompt, y_sample, new_conv_prompt, new_pool_prompt, new_conv_sample, new_pool_sample)
```

```python
import functools
from typing import NamedTuple

import jax
import jax.numpy as jnp
from jax import lax
from jax.experimental import pallas as pl
from jax.experimental.pallas import tpu as pltpu

D_MODEL = 2048
D_FF = 5632
N_META = 16
SEQ = 2048
N_SAMPLE_SEQ = 128
SAMPLE_T = 8
CONV_HIST = 2
POOL_WINDOWS = (2, 4, 8, 16)
POOL_GROUP = D_MODEL // len(POOL_WINDOWS)
POOL_HIST = 15
EPS = 1e-6

N_SAMPLE_ROWS = N_SAMPLE_SEQ * SAMPLE_T
S_ROWS = N_SAMPLE_ROWS + N_META
META_ROW0 = N_SAMPLE_ROWS
TAIL = 8
HALO = 16
MXU_COLS = 256
FRONT = 8
SWIGLU_ROWS = 1024
W_CHUNK_COLS = 512

V7X_SCOPED_VMEM_BYTES = 60000 * 1024

BF16 = jnp.bfloat16
F32 = jnp.float32


def _rms(x, g):
    ms = jnp.mean(x * x, axis=-1, keepdims=True)
    return (x * lax.rsqrt(ms + EPS)) * g


def _dot(a, b):
    return jnp.dot(a, b, preferred_element_type=F32)


def _params(n_axes):
    return pltpu.CompilerParams(
        dimension_semantics=("arbitrary",) * n_axes,
        vmem_limit_bytes=V7X_SCOPED_VMEM_BYTES,
    )


def _resident(shape, index_map):
    return pl.BlockSpec(shape, index_map, pipeline_mode=pl.Buffered(1))


class _CastJob(NamedTuple):
    src: jax.Array
    in_spec: pl.BlockSpec
    out_spec: pl.BlockSpec
    out_shape: jax.ShapeDtypeStruct


def _cast_job(w_stack, layer, n_steps, axis, step_of):
    _, r, c = w_stack.shape
    if axis == 0:
        blk = (r // n_steps, c)
        in_idx = lambda *ids: (layer, step_of(*ids), 0)
        out_idx = lambda *ids: (step_of(*ids), 0)
    else:
        blk = (r, c // n_steps)
        in_idx = lambda *ids: (layer, 0, step_of(*ids))
        out_idx = lambda *ids: (0, step_of(*ids))
    assert blk[0] * (n_steps if axis == 0 else 1) == r and blk[1] * (n_steps if axis == 1 else 1) == c
    return _CastJob(w_stack, pl.BlockSpec((None,) + blk, in_idx), pl.BlockSpec(blk, out_idx),
                    jax.ShapeDtypeStruct((r, c), BF16))


def _run_casts(src_refs, dst_refs):
    for src, dst in zip(src_refs, dst_refs):
        dst[...] = src[...].astype(BF16)


def _conv3(w, u, ubuf, off):
    t = u.shape[0]
    return (w[2:3] * u + w[1:2] * ubuf[off - 1:off - 1 + t, :] + w[0:1] * ubuf[off - 2:off - 2 + t, :])


def _mix_p_kernel(tiles_per_seq, n_cast, *refs):
    h_ref, g_ref, wb_ref, wc_ref, wv_ref, wdw_ref, umeta_ref = refs[:7]
    cast_src, refs = refs[7:7 + n_cast], refs[7 + n_cast:]
    z_ref, utail_ref = refs[:2]
    cast_dst, (n_sc, ubuf, carry) = refs[2:2 + n_cast], refs[2 + n_cast:]
    i = pl.program_id(0)
    j = pl.program_id(1)
    bm = h_ref.shape[0]
    _run_casts(cast_src, cast_dst)

    @pl.when(j == 0)
    def _():
        n_sc[...] = _rms(h_ref[...], g_ref[...]).astype(BF16)

    @pl.when(i % tiles_per_seq == 0)
    def _():
        ubuf[0:TAIL, :] = umeta_ref[...]

    @pl.when(i % tiles_per_seq != 0)
    def _():
        ubuf[0:TAIL, :] = carry[j]

    n = n_sc[...]
    for c0 in range(0, z_ref.shape[1], MXU_COLS):
        cols = slice(c0, c0 + MXU_COLS)
        b = _dot(n, wb_ref[:, cols])
        u = _dot(n, wc_ref[:, cols]) * _dot(n, wv_ref[:, cols])
        ubuf[TAIL:TAIL + bm, cols] = u
        z_ref[:, cols] = (b * _conv3(wdw_ref[:, cols], u, ubuf.at[:, cols], TAIL)).astype(BF16)
    tail = ubuf[bm:bm + TAIL, :]
    carry[j] = tail
    utail_ref[...] = tail


def _mix_s_kernel(xs_ref, meta_ref, g_ref, wb_ref, wc_ref, wv_ref, wdw_ref, hc_ref,
                  z_ref, uts_ref, umeta_ref, wb_o, wc_o, wv_o, h0_ref, n_sc, ubuf, mbuf):
    j = pl.program_id(0)
    ns = N_SAMPLE_ROWS
    nb = N_SAMPLE_SEQ
    nm = S_ROWS - ns

    @pl.when(j == 0)
    def _():
        for t in range(SAMPLE_T):
            h0_ref[t * nb:(t + 1) * nb, :] = xs_ref[:, t, :]
        h0_ref[ns:, :] = meta_ref[...]
        n_sc[...] = _rms(h0_ref[...], g_ref[...]).astype(BF16)

    wb_o[...] = wb_ref[...].astype(BF16)
    wc_o[...] = wc_ref[...].astype(BF16)
    wv_o[...] = wv_ref[...].astype(BF16)
    n = n_sc[...]
    nh = CONV_HIST * nb
    ubuf[0:nh, :] = hc_ref[...]
    mbuf[0:TAIL, :] = jnp.zeros((TAIL, mbuf.shape[1]), F32)
    for c0 in range(0, z_ref.shape[1], MXU_COLS):
        cols = slice(c0, c0 + MXU_COLS)
        b = _dot(n, wb_o[:, cols])
        u = _dot(n, wc_o[:, cols]) * _dot(n, wv_o[:, cols])
        w = wdw_ref[:, cols]

        us = u[0:ns, :]
        ubuf[nh:nh + ns, cols] = us
        conv_s = (w[2:3] * us + w[1:2] * ubuf[nb:nb + ns, cols] + w[0:1] * ubuf[0:ns, cols])
        z_ref[0:ns, cols] = (b[0:ns, :] * conv_s).astype(BF16)

        um = u[ns:, :]
        mbuf[TAIL:TAIL + nm, cols] = um
        z_ref[ns:, cols] = (b[ns:, :] * _conv3(w, um, mbuf.at[:, cols], TAIL)).astype(BF16)
    uts_ref[...] = ubuf[ns:ns + nh, :]
    umeta_ref[...] = mbuf[N_META:N_META + TAIL, :]


def _mix_conv_p(h, g, wb, wc, wv, w_dw, umeta, bm, bn, casts):
    rows = h.shape[0]
    n_i, n_j = rows // bm, D_MODEL // bn
    wspec = pl.BlockSpec((D_MODEL, bn), lambda i, j: (0, j))
    return pl.pallas_call(
        functools.partial(_mix_p_kernel, SEQ // bm, len(casts)),
        grid=(n_i, n_j),
        in_specs=[
            pl.BlockSpec((bm, D_MODEL), lambda i, j: (i, 0)),
            pl.BlockSpec((1, D_MODEL), lambda i, j: (0, 0)),
            wspec, wspec, wspec,
            pl.BlockSpec((3, bn), lambda i, j: (0, j)),
            pl.BlockSpec((TAIL, bn), lambda i, j: (0, j)),
        ] + [c.in_spec for c in casts],
        out_specs=[
            pl.BlockSpec((bm, bn), lambda i, j: (i, j)),
            pl.BlockSpec((TAIL, bn), lambda i, j: (i, j)),
        ] + [c.out_spec for c in casts],
        out_shape=[
            jax.ShapeDtypeStruct((rows, D_MODEL), BF16),
            jax.ShapeDtypeStruct((n_i * TAIL, D_MODEL), F32),
        ] + [c.out_shape for c in casts],
        scratch_shapes=[
            pltpu.VMEM((bm, D_MODEL), BF16),
            pltpu.VMEM((TAIL + bm, bn), F32),
            pltpu.VMEM((n_j, TAIL, bn), F32),
        ],
        compiler_params=_params(2),
        name="mix_conv_p",
    )(h, g, wb, wc, wv, w_dw, umeta, *[c.src for c in casts])


def _mix_conv_s(x_sample, meta, g, w_in, w_dw, hc, bn):
    n_j = D_MODEL // bn
    nb = N_SAMPLE_SEQ
    nm = S_ROWS - N_SAMPLE_ROWS
    wspec = lambda part: pl.BlockSpec((None, D_MODEL, bn), lambda j: (0, 0, part * n_j + j))
    wout = pl.BlockSpec((D_MODEL, bn), lambda j: (0, j))
    wshape = jax.ShapeDtypeStruct((D_MODEL, D_MODEL), BF16)
    return pl.pallas_call(
        _mix_s_kernel,
        grid=(n_j,),
        in_specs=[
            _resident((nb, SAMPLE_T, D_MODEL), lambda j: (0, 0, 0)),
            _resident((N_META, D_MODEL), lambda j: (0, 0)),
            pl.BlockSpec((1, D_MODEL), lambda j: (0, 0)),
            wspec(0), wspec(1), wspec(2),
            pl.BlockSpec((3, bn), lambda j: (0, j)),
            pl.BlockSpec((CONV_HIST * nb, bn), lambda j: (0, j)),
        ],
        out_specs=[
            pl.BlockSpec((S_ROWS, bn), lambda j: (0, j)),
            pl.BlockSpec((CONV_HIST * nb, bn), lambda j: (0, j)),
            pl.BlockSpec((TAIL, bn), lambda j: (0, j)),
            wout, wout, wout,
            pl.BlockSpec((S_ROWS, D_MODEL), lambda j: (0, 0)),
        ],
        out_shape=[
            jax.ShapeDtypeStruct((S_ROWS, D_MODEL), BF16),
            jax.ShapeDtypeStruct((CONV_HIST * nb, D_MODEL), F32),
            jax.ShapeDtypeStruct((TAIL, D_MODEL), F32),
            wshape, wshape, wshape,
            jax.ShapeDtypeStruct((S_ROWS, D_MODEL), F32),
        ],
        scratch_shapes=[
            pltpu.VMEM((S_ROWS, D_MODEL), BF16),
            pltpu.VMEM((CONV_HIST * nb + N_SAMPLE_ROWS, bn), F32),
            pltpu.VMEM((TAIL + nm, bn), F32),
        ],
        compiler_params=_params(1),
        name="mix_conv_s",
    )(x_sample, meta, g, w_in, w_in, w_in, w_dw, hc)


def _proj_res_body(norm, k, n_k, x_ref, w_ref, h_ref, g_ref, o_ref, m_ref):
    @pl.when(k == 0)
    def _():
        o_ref[...] = h_ref[...] + _dot(x_ref[...], w_ref[...])

    @pl.when(k != 0)
    def _():
        o_ref[...] += _dot(x_ref[...], w_ref[...])

    if norm is not None:
        @pl.when(k == n_k - 1)
        def _():
            y = _rms(o_ref[...], g_ref[...])
            if norm == "final":
                o_ref[...] = y
            else:
                m_ref[...] = y.astype(BF16)


def _weight_chunk_copy(w_hbm, w_vmem, sem, c):
    cols = pl.ds(c * W_CHUNK_COLS, W_CHUNK_COLS)
    return pltpu.make_async_copy(w_hbm.at[:, cols], w_vmem.at[:, cols], sem.at[c])


def _proj_res_kept_w_kernel(norm, x_ref, w_hbm, h_ref, g_ref, o_ref, *rest):
    m_ref = rest[0] if norm == "emit" else None
    w_vmem, sem = rest[-2:]
    n_chunks = w_vmem.shape[1] // W_CHUNK_COLS

    def finish(acc):
        o_ref[...] = _rms(acc, g_ref[...]) if norm == "final" else acc
        if norm == "emit":
            m_ref[...] = _rms(acc, g_ref[...]).astype(BF16)

    @pl.when(pl.program_id(0) == 0)
    def _():
        for c in range(n_chunks):
            _weight_chunk_copy(w_hbm, w_vmem, sem, c).start()
        for c in range(n_chunks):
            _weight_chunk_copy(w_hbm, w_vmem, sem, c).wait()
            cols = slice(c * W_CHUNK_COLS, (c + 1) * W_CHUNK_COLS)
            o_ref[:, cols] = h_ref[:, cols] + _dot(x_ref[...], w_vmem[:, cols])
        if norm is not None:
            finish(o_ref[...])

    @pl.when(pl.program_id(0) != 0)
    def _():
        finish(h_ref[...] + _dot(x_ref[...], w_vmem[...]))


def _proj_res_p(x, w, h, g, bm, norm, name):
    rows, kdim = x.shape
    row_spec = pl.BlockSpec((bm, D_MODEL), lambda i: (i, 0))
    out_specs, out_shape = [row_spec], [jax.ShapeDtypeStruct((rows, D_MODEL), F32)]
    if norm == "emit":
        out_specs.append(row_spec)
        out_shape.append(jax.ShapeDtypeStruct((rows, D_MODEL), BF16))
    return pl.pallas_call(
        functools.partial(_proj_res_kept_w_kernel, norm),
        grid=(rows // bm,),
        in_specs=[
            pl.BlockSpec((bm, kdim), lambda i: (i, 0)),
            pl.BlockSpec(memory_space=pl.ANY),
            row_spec,
            pl.BlockSpec((1, D_MODEL), lambda i: (0, 0)),
        ],
        out_specs=out_specs,
        out_shape=out_shape,
        scratch_shapes=[
            pltpu.VMEM((kdim, D_MODEL), BF16),
            pltpu.SemaphoreType.DMA((D_MODEL // W_CHUNK_COLS,)),
        ],
        compiler_params=_params(1),
        name=name,
    )(x, w, h, g)


def _swiglu_chunks(m_ref, wg_ref, wu_ref, a_ref):
    rows = a_ref.shape[0]
    rp = SWIGLU_ROWS if rows % SWIGLU_ROWS == 0 else rows
    for c0 in range(0, a_ref.shape[1], MXU_COLS):
        cols = slice(c0, c0 + MXU_COLS)
        for r0 in range(0, rows, rp):
            m = m_ref[r0:r0 + rp, :]
            gate = _dot(m, wg_ref[:, cols])
            up = _dot(m, wu_ref[:, cols])
            a_ref[r0:r0 + rp, cols] = (gate * (1.0 / (1.0 + jnp.exp(-gate))) * up).astype(BF16)


def _gate_up_p_kernel(n_cast, m_ref, wg_ref, wu_ref, *refs):
    _run_casts(refs[:n_cast], refs[n_cast + 1:])
    _swiglu_chunks(m_ref, wg_ref, wu_ref, refs[n_cast])


def _ffn_s_kernel(sample_out, n_pre, n_j, *refs):
    h_ref, refs = refs[0], refs[1:]
    if n_pre:
        (x_ref, wpre_ref), refs = refs[:2], refs[2:]
    g_ref, gfin_ref, wg_ref, wu_ref, wd_ref, o_ref, m_sc, a_sc = refs[:8]
    rest = list(refs[8:])
    hsrc = rest.pop(0) if n_pre else h_ref
    acc = rest.pop(0) if sample_out else o_ref
    s = pl.program_id(0)
    j = s - n_pre

    if n_pre:
        @pl.when(s < n_pre)
        def _():
            _proj_res_body(None, s, n_pre, x_ref, wpre_ref, h_ref, None, hsrc, None)

    @pl.when(s >= n_pre)
    def _():
        @pl.when(j == 0)
        def _():
            m_sc[...] = _rms(hsrc[...], g_ref[...]).astype(BF16)

        _swiglu_chunks(m_sc, wg_ref, wu_ref, a_sc)
        _proj_res_body(None, j, n_j, a_sc, wd_ref, hsrc, None, acc, None)

        if sample_out:
            @pl.when(j == n_j - 1)
            def _():
                y = _rms(acc[0:N_SAMPLE_ROWS, :], gfin_ref[...])
                for t in range(SAMPLE_T):
                    o_ref[:, t, :] = y[t * N_SAMPLE_SEQ:(t + 1) * N_SAMPLE_SEQ, :]


def _ffn_s(h, g, g_fin, w_gu, w_dn, bn, sample_out, name, pre=None):
    rows = h.shape[0]
    n_j = D_FF // bn
    n_pre = 0 if pre is None else pre[0].shape[1] // pre[2]
    ffn_step = lambda s: jnp.maximum(s - n_pre, 0)
    row_spec = pl.BlockSpec((1, D_MODEL), lambda s: (0, 0))
    o_shape = (N_SAMPLE_SEQ, SAMPLE_T, D_MODEL) if sample_out else (rows, D_MODEL)
    o_index = (0,) * len(o_shape)
    in_specs, args = [_resident((rows, D_MODEL), lambda s: (0, 0))], [h]
    scratch = [pltpu.VMEM((rows, D_MODEL), BF16), pltpu.VMEM((rows, bn), BF16)]
    if n_pre:
        x, w_pre, bk = pre
        pre_step = lambda s: jnp.minimum(s, n_pre - 1)
        in_specs += [pl.BlockSpec((rows, bk), lambda s: (0, pre_step(s))),
                     pl.BlockSpec((bk, D_MODEL), lambda s: (pre_step(s), 0))]
        args += [x, w_pre]
        scratch.append(pltpu.VMEM((rows, D_MODEL), F32))
    if sample_out:
        scratch.append(pltpu.VMEM((rows, D_MODEL), F32))
    in_specs += [
        row_spec, row_spec,
        pl.BlockSpec((D_MODEL, bn), lambda s: (0, ffn_step(s))),
        pl.BlockSpec((D_MODEL, bn), lambda s: (0, n_j + ffn_step(s))),
        pl.BlockSpec((bn, D_MODEL), lambda s: (ffn_step(s), 0)),
    ]
    return pl.pallas_call(
        functools.partial(_ffn_s_kernel, sample_out, n_pre, n_j),
        grid=(n_pre + n_j,),
        in_specs=in_specs,
        out_specs=pl.BlockSpec(o_shape, lambda s: o_index),
        out_shape=jax.ShapeDtypeStruct(o_shape, F32),
        scratch_shapes=scratch,
        compiler_params=_params(1),
        name=name,
    )(*args, g, g_fin, w_gu, w_gu, w_dn)


def _gate_up_p(m, w_gu, bm, bn, casts, name):
    rows = m.shape[0]
    n_i, n_j = rows // bm, D_FF // bn
    return pl.pallas_call(
        functools.partial(_gate_up_p_kernel, len(casts)),
        grid=(n_i, n_j),
        in_specs=[
            pl.BlockSpec((bm, D_MODEL), lambda i, j: (i, 0)),
            pl.BlockSpec((D_MODEL, bn), lambda i, j: (0, j)),
            pl.BlockSpec((D_MODEL, bn), lambda i, j: (0, n_j + j)),
        ] + [c.in_spec for c in casts],
        out_specs=[pl.BlockSpec((bm, bn), lambda i, j: (i, j))] + [c.out_spec for c in casts],
        out_shape=[jax.ShapeDtypeStruct((rows, D_FF), BF16)] + [c.out_shape for c in casts],
        compiler_params=_params(2),
        name=name,
    )(m, w_gu, w_gu, *[c.src for c in casts])


def _down_pool_kernel(tiles_per_seq, n_t, a_ref, w_hbm, h_ref, meta_ref, g_ref, wp_ref, sc_ref, gnext_ref,
                      o_ref, ntail_ref, m_ref, h2_even, h2_odd, ncarry, nbuf, pbuf, qbuf, w_vmem, sem):
    s = pl.program_id(0)
    bm = h_ref.shape[0]
    g = g_ref[...]
    rows = HALO + bm
    r0 = FRONT + HALO
    split = (3 * D_MODEL) // 4

    @pl.when((s - 1) % tiles_per_seq == 0)
    def _():
        nbuf[FRONT:r0, :] = _rms(meta_ref[...], g)

    @pl.when(jnp.logical_and(s >= 1, (s - 1) % tiles_per_seq != 0))
    def _():
        nbuf[FRONT:r0, :] = ncarry[...]

    def project(h2_new, c0, c1):
        h2_new[:, c0:c1] = h_ref[:, c0:c1] + _dot(a_ref[...], w_vmem[:, c0:c1])

    def step(h2_new, h2_old):
        if h2_old is None:
            n_chunks = D_MODEL // W_CHUNK_COLS
            for c in range(n_chunks):
                _weight_chunk_copy(w_hbm, w_vmem, sem, c).start()
            for c in range(n_chunks):
                _weight_chunk_copy(w_hbm, w_vmem, sem, c).wait()
                project(h2_new, c * W_CHUNK_COLS, (c + 1) * W_CHUNK_COLS)
            return
        if h2_new is not None:
            project(h2_new, 0, split)
        for buf in (nbuf, pbuf, qbuf):
            buf[0:FRONT, :] = jnp.zeros((FRONT, buf.shape[1]), F32)
        n = _rms(h2_old[...], g)
        nbuf[r0:r0 + bm, :] = n
        tail = n[bm - HALO:, :]
        ncarry[...] = tail
        ntail_ref[...] = tail
        for gi, win in enumerate(POOL_WINDOWS):
            cols = slice(gi * POOL_GROUP, (gi + 1) * POOL_GROUP)
            src, src_cols, shift, level = nbuf, cols, 1, 0
            while shift < win:
                dst = (pbuf, qbuf)[level % 2]
                dst[FRONT:FRONT + rows, :] = (src[FRONT:FRONT + rows, src_cols]
                                              + src[FRONT - shift:FRONT - shift + rows, src_cols])
                src, src_cols, shift, level = dst, slice(None), 2 * shift, level + 1
            p = src[r0:r0 + bm, src_cols] * (1.0 / win) - n[:, cols]
            y = _dot(p.astype(BF16), wp_ref[gi].astype(BF16)) * sc_ref[:, cols]
            o_ref[:, cols] = h2_old[:, cols] + y
        if h2_new is not None:
            project(h2_new, split, D_MODEL)
        m_ref[...] = _rms(o_ref[...], gnext_ref[...]).astype(BF16)

    @pl.when(s == 0)
    def _():
        step(h2_even, None)

    @pl.when(jnp.logical_and(s % 2 == 0, jnp.logical_and(s > 0, s < n_t)))
    def _():
        step(h2_even, h2_odd)

    @pl.when(jnp.logical_and(s % 2 == 1, s < n_t))
    def _():
        step(h2_odd, h2_even)

    @pl.when(s == n_t)
    def _():
        step(None, (h2_even, h2_odd)[(n_t - 1) % 2])


def _down_pool_p(a, w, h, h_s, g, wp, scale, g_next, bm):
    rows, kdim = a.shape
    n_t = rows // bm
    ng = len(POOL_WINDOWS)
    cur = lambda s: (jnp.minimum(s, n_t - 1), 0)
    prev = lambda s: (jnp.maximum(s - 1, 0), 0)
    vec = pl.BlockSpec((1, D_MODEL), lambda s: (0, 0))
    return pl.pallas_call(
        functools.partial(_down_pool_kernel, SEQ // bm, n_t),
        grid=(n_t + 1,),
        in_specs=[
            pl.BlockSpec((bm, kdim), cur),
            pl.BlockSpec(memory_space=pl.ANY),
            pl.BlockSpec((bm, D_MODEL), cur),
            _resident((HALO, D_MODEL), lambda s: (META_ROW0 // HALO, 0)),
            vec,
            _resident((None, ng, POOL_GROUP, POOL_GROUP), lambda s: (0, 0, 0, 0)),
            vec, vec,
        ],
        out_specs=[
            pl.BlockSpec((bm, D_MODEL), prev),
            pl.BlockSpec((HALO, D_MODEL), prev),
            pl.BlockSpec((bm, D_MODEL), prev),
        ],
        out_shape=[
            jax.ShapeDtypeStruct((rows, D_MODEL), F32),
            jax.ShapeDtypeStruct((n_t * HALO, D_MODEL), F32),
            jax.ShapeDtypeStruct((rows, D_MODEL), BF16),
        ],
        scratch_shapes=[
            pltpu.VMEM((bm, D_MODEL), F32),
            pltpu.VMEM((bm, D_MODEL), F32),
            pltpu.VMEM((HALO, D_MODEL), F32),
            pltpu.VMEM((FRONT + HALO + bm, D_MODEL), F32),
            pltpu.VMEM((FRONT + HALO + bm, POOL_GROUP), F32),
            pltpu.VMEM((FRONT + HALO + bm, POOL_GROUP), F32),
            pltpu.VMEM((kdim, D_MODEL), BF16),
            pltpu.SemaphoreType.DMA((D_MODEL // W_CHUNK_COLS,)),
        ],
        compiler_params=_params(1),
        name="down_pool_p",
    )(a, w, h, h_s, g, wp, scale, g_next)


def _pool_s_kernel(hfull_ref, hcol_ref, hp_ref, g_ref, wp_ref, sc_ref, o_ref, tail_ref, inv_sc, nbuf, sum_sc):
    j = pl.program_id(0)
    ns = N_SAMPLE_ROWS
    nb = N_SAMPLE_SEQ
    nh = POOL_HIST * nb

    @pl.when(j == 0)
    def _():
        xf = hfull_ref[...]
        inv_sc[...] = lax.rsqrt(jnp.mean(xf * xf, axis=-1, keepdims=True) + EPS)

    x = hcol_ref[...]
    n = ((x * inv_sc[...]) * g_ref[...])[0:ns, :]
    nbuf[0:nh, :] = hp_ref[...]
    nbuf[nh:nh + ns, :] = n
    tail_ref[...] = nbuf[ns:ns + nh, :]

    for gi, win in enumerate(POOL_WINDOWS):
        @pl.when(j == gi)
        def _(win=win):
            acc = n
            for k in range(1, win):
                acc = acc + nbuf[nh - k * nb:nh - k * nb + ns, :]
            sum_sc[...] = acc * (1.0 / win)

    p = sum_sc[...] - n
    y = _dot(p.astype(BF16), wp_ref[...].astype(BF16)) * sc_ref[...]
    o_ref[...] = x[0:ns, :] + y


def _pool_s(h_s, state_pool, g, wp, scale):
    ns = N_SAMPLE_ROWS
    nb = N_SAMPLE_SEQ
    nh = POOL_HIST * nb
    pg = POOL_GROUP
    n_g = len(POOL_WINDOWS)
    state_spec = pl.BlockSpec((nh, pg), lambda j: (0, j))
    return pl.pallas_call(
        _pool_s_kernel,
        grid=(n_g,),
        in_specs=[
            _resident((S_ROWS, D_MODEL), lambda j: (0, 0)),
            pl.BlockSpec((S_ROWS, pg), lambda j: (0, j)),
            state_spec,
            pl.BlockSpec((1, pg), lambda j: (0, j)),
            pl.BlockSpec((None, None, pg, pg), lambda j: (0, j, 0, 0)),
            pl.BlockSpec((1, pg), lambda j: (0, j)),
        ],
        out_specs=[pl.BlockSpec((ns, pg), lambda j: (0, j)), state_spec],
        out_shape=[
            jax.ShapeDtypeStruct((ns, D_MODEL), F32),
            jax.ShapeDtypeStruct((nh, D_MODEL), F32),
        ],
        scratch_shapes=[
            pltpu.VMEM((S_ROWS, 1), F32),
            pltpu.VMEM((nh + ns, pg), F32),
            pltpu.VMEM((ns, pg), F32),
        ],
        compiler_params=_params(1),
        name="pool_s",
    )(h_s, h_s, state_pool, g, wp, scale)


BM_P = 1024
BN_MIX = 512
BM_CONV_OUT = 512
BM_GATE_UP = 2048
BM_GATE_UP1 = 4096
BN_P = 512
BN_P0 = 256
BM_DOWN = 512
BM_POOL = 256
BN_S = 256
BN_S_FF = 512
BK_S = 512

assert SEQ % BM_P == 0 and SEQ % BM_POOL == 0 and BM_POOL % HALO == 0 and META_ROW0 % HALO == 0
assert D_MODEL % BN_MIX == 0 and D_MODEL % BN_S == 0 and BN_MIX % MXU_COLS == 0 and BN_S % MXU_COLS == 0
assert D_FF % BN_P == 0 and D_FF % BN_S_FF == 0 and D_FF % BK_S == 0 and D_MODEL % BK_S == 0


def kernel(x_prompt, x_sample, state_conv, state_pool, meta_tokens, norm_mix, norm_ffn, norm_final,
           conv_w_in, conv_w_dw, conv_w_out, pool_w, pool_scale, ffn_w_gate_up, ffn_w_down):
    d = D_MODEL
    nb, nt = N_SAMPLE_SEQ, SAMPLE_T
    batch = x_prompt.shape[0]
    assert x_prompt.shape == (batch, SEQ, d) and x_sample.shape == (nb, nt, d)
    assert state_conv.shape == (1, nb, CONV_HIST, d) and state_pool.shape == (1, nb, POOL_HIST, d)
    assert meta_tokens.shape == (N_META, d) and ffn_w_gate_up.shape == (2, d, 2 * D_FF)
    assert (batch * SEQ) % BM_GATE_UP == 0 and (batch * SEQ) % BM_DOWN == 0 and (batch * SEQ) % BM_CONV_OUT == 0
    assert (batch * SEQ) % BM_GATE_UP1 == 0

    h_p = x_prompt.reshape(batch * SEQ, d)
    hc = state_conv[0].transpose(1, 0, 2).reshape(CONV_HIST * nb, d)
    hp = state_pool[0].transpose(1, 0, 2).reshape(POOL_HIST * nb, d)
    row = lambda v: v.reshape(1, d)
    g_mix, g_ffn, g_fin = norm_mix, norm_ffn, row(norm_final)

    rows_p = batch * SEQ
    n_j_mix, n_j_ff = d // BN_MIX, D_FF // BN_P
    n_mix, n_ff = (rows_p // BM_P) * n_j_mix, (rows_p // BM_GATE_UP) * n_j_ff
    step_mix = lambda i, j: i * n_j_mix + j
    step_ff = lambda i, j: i * n_j_ff + j
    casts_mix = [_cast_job(conv_w_out, 0, n_mix, 0, step_mix), _cast_job(ffn_w_gate_up, 0, n_mix, 0, step_mix)]
    n_j_ff0 = D_FF // BN_P0
    n_ff0 = (rows_p // BM_GATE_UP1) * n_j_ff0
    step_ff0 = lambda i, j: i * n_j_ff0 + j
    casts_ff0 = [_cast_job(ffn_w_down, 0, n_ff0, 0, step_ff0), _cast_job(ffn_w_gate_up, 1, n_ff0, 1, step_ff0)]
    casts_ff1 = [_cast_job(ffn_w_down, 1, (rows_p // BM_GATE_UP1) * n_j_ff, 0, step_ff)]

    z_s, uts, umeta, wb, wc, wv, h_s = _mix_conv_s(
        x_sample, meta_tokens, row(g_mix[0]), conv_w_in, conv_w_dw[0], hc, BN_S)
    z_p, utail_p, w_out, w_gu0 = _mix_conv_p(
        h_p, row(g_mix[0]), wb, wc, wv, conv_w_dw[0], umeta, BM_P, BN_MIX, casts_mix)
    h_p, m_p = _proj_res_p(z_p, w_out, h_p, row(g_ffn[0]), BM_CONV_OUT, "emit", "conv_out_p")
    a_p, w_dn0, w_gu1 = _gate_up_p(m_p, w_gu0, BM_GATE_UP1, BN_P0, casts_ff0, "gate_up0_p")
    h_s = _ffn_s(h_s, row(g_ffn[0]), g_fin, w_gu0, w_dn0, BN_S_FF, False, "conv_out_ffn0_s", pre=(z_s, w_out, BK_S))
    h_p1, ntail_p, m_p = _down_pool_p(a_p, w_dn0, h_p, h_s, row(g_mix[1]), pool_w, row(pool_scale[0]),
                                      row(g_ffn[1]), BM_POOL)
    h_s1, pool_tail_s = _pool_s(h_s, hp, row(g_mix[1]), pool_w, row(pool_scale[0]))
    a_p, w_dn1 = _gate_up_p(m_p, w_gu1, BM_GATE_UP1, BN_P, casts_ff1, "gate_up1_p")
    y_s = _ffn_s(h_s1, row(g_ffn[1]), g_fin, w_gu1, w_dn1, BN_S_FF, True, "ffn1_s")
    (y_p,) = _proj_res_p(a_p, w_dn1, h_p1, g_fin, BM_DOWN, "final", "down1_p")

    y_prompt = y_p.reshape(batch, SEQ, d)
    y_sample = y_s
    tps = SEQ // BM_P
    new_conv_prompt = utail_p.reshape(batch, tps, TAIL, d)[:, tps - 1, TAIL - CONV_HIST:, :][None]
    tpp = SEQ // BM_POOL
    new_pool_prompt = ntail_p.reshape(batch, tpp, HALO, d)[:, tpp - 1, HALO - POOL_HIST:, :][None]
    new_conv_sample = uts.reshape(CONV_HIST, nb, d).transpose(1, 0, 2)[None]
    new_pool_sample = pool_tail_s.reshape(POOL_HIST, nb, d).transpose(1, 0, 2)[None]
    return (y_prompt, y_sample, new_conv_prompt, new_pool_prompt, new_conv_sample, new_pool_sample)
```
